```python
import jax, jax.numpy as jnp
from jax import lax
import numpy as np

D_MODEL = 2048
BATCH = 8
SEQ = 2048
DEPTH = 2

EPS = 1e-6
D_MIX = 2 * D_MODEL
CONV_K = 4
CHUNK = 128
SSD_WIDTH = D_MIX // 2
SSD_HEAD_DIM = 64
SSD_HEADS = SSD_WIDTH // SSD_HEAD_DIM
SSD_GROUPS = 8
SSD_STATE = 128
SSD_XBC = SSD_WIDTH + 2 * SSD_GROUPS * SSD_STATE
LRU_WIDTH = D_MIX // 4
LRU_BLOCKS = 16
LRU_BLOCK = LRU_WIDTH // LRU_BLOCKS
LRU_C = 8.0
MLSTM_WIDTH = D_MIX // 4
MLSTM_HEADS = 4
MLSTM_HEAD_DIM = MLSTM_WIDTH // MLSTM_HEADS
MLSTM_QKV_BLOCK = 4
MLSTM_NBLK = MLSTM_WIDTH // MLSTM_QKV_BLOCK
MLSTM_KSCALE = MLSTM_HEAD_DIM ** -0.5
D_FF = ((8 * D_MODEL + 3 * 256 - 1) // (3 * 256)) * 256
IN_SIZES = (SSD_WIDTH, SSD_XBC, SSD_HEADS, LRU_WIDTH, LRU_WIDTH, MLSTM_WIDTH, MLSTM_WIDTH)
D_IN = sum(IN_SIZES)
IN_SPLITS = [sum(IN_SIZES[:i + 1]) for i in range(len(IN_SIZES) - 1)]

kernel_name = "hymba_ssd_rglru_mlstm_hybrid"


def rmsnorm(x, w):
    xf = x.astype(jnp.float32)
    y = xf * lax.rsqrt(jnp.mean(xf * xf, axis=-1, keepdims=True) + EPS)
    return (y * w).astype(x.dtype)


def causal_conv(x, w, b):
    c = x.shape[-1]
    y = lax.conv_general_dilated(x, w[:, None, :].astype(x.dtype), window_strides=(1,),
                                 padding=[(CONV_K - 1, 0)],
                                 dimension_numbers=('NWC', 'WIO', 'NWC'),
                                 feature_group_count=c)
    return y + b


def ssd_chunked(x, dt, a, bm, cm):
    b, s, h, p = x.shape
    g, n = bm.shape[2], bm.shape[3]
    j = h // g
    c = s // CHUNK
    l = CHUNK
    xd = (x * dt[..., None]).reshape(b, c, l, g, j, p)
    cs = jnp.cumsum((dt * a).reshape(b, c, l, g, j), axis=2)
    bc = bm.reshape(b, c, l, g, n)
    cc = cm.reshape(b, c, l, g, n)
    causal = jnp.tril(jnp.ones((l, l), dtype=bool))[None, None, :, :, None, None]
    seg = cs[:, :, :, None] - cs[:, :, None, :]
    decay = jnp.exp(jnp.where(causal, seg, -jnp.inf))
    scores = jnp.einsum('bctgn,bcsgn->bctsg', cc, bc)
    y_diag = jnp.einsum('bctsgj,bcsgjp->bctgjp', scores[..., None] * decay, xd)
    to_end = jnp.exp(cs[:, :, -1:] - cs)
    states = jnp.einsum('bcsgn,bcsgjp->bcgjpn', bc, xd * to_end[..., None])
    chunk_decay = jnp.exp(cs[:, :, -1])

    def step(h_prev, inp):
        st, dec = inp
        return h_prev * dec[..., None, None] + st, h_prev

    h0 = jnp.zeros((b, g, j, p, n), jnp.float32)
    _, h_in = lax.scan(step, h0, (jnp.moveaxis(states, 1, 0), jnp.moveaxis(chunk_decay, 1, 0)))
    h_in = jnp.moveaxis(h_in, 0, 1)
    y_off = jnp.einsum('bctgn,bcgjpn->bctgjp', cc, h_in) * jnp.exp(cs)[..., None]
    return (y_diag + y_off).reshape(b, s, h, p)


def ssd_mixer(z, xbc, dt_raw, conv_w, conv_b, dt_bias, a_log, d_skip, norm_w):
    b, s, _ = z.shape
    xbc = jax.nn.silu(causal_conv(xbc, conv_w, conv_b).astype(jnp.float32))
    xs, bm, cm = jnp.split(xbc, [SSD_WIDTH, SSD_WIDTH + SSD_GROUPS * SSD_STATE], axis=-1)
    xs = xs.reshape(b, s, SSD_HEADS, SSD_HEAD_DIM)
    bm = bm.reshape(b, s, SSD_GROUPS, SSD_STATE)
    cm = cm.reshape(b, s, SSD_GROUPS, SSD_STATE)
    dt = jax.nn.softplus(dt_raw.astype(jnp.float32) + dt_bias)
    a = -jnp.exp(a_log.astype(jnp.float32))
    y = ssd_chunked(xs, dt, a, bm, cm) + d_skip[:, None] * xs
    y = y.reshape(b, s, SSD_WIDTH) * jax.nn.silu(z.astype(jnp.float32))
    yg = y.reshape(b, s, SSD_GROUPS, SSD_WIDTH // SSD_GROUPS)
    yg = yg * lax.rsqrt(jnp.mean(yg * yg, axis=-1, keepdims=True) + EPS)
    return yg.reshape(b, s, SSD_WIDTH) * norm_w


def rglru_mixer(gate, xr, conv_w, conv_b, w_a, b_a, w_x, b_x, lam):
    b, s, _ = xr.shape
    xc = causal_conv(xr, conv_w, conv_b).astype(jnp.float32)
    xb = xc.reshape(b, s, LRU_BLOCKS, LRU_BLOCK)
    r = jax.nn.sigmoid(jnp.einsum('bsnc,ncd->bsnd', xb, w_a).reshape(b, s, LRU_WIDTH) + b_a)
    i = jax.nn.sigmoid(jnp.einsum('bsnc,ncd->bsnd', xb, w_x).reshape(b, s, LRU_WIDTH) + b_x)
    log_a = -LRU_C * r * jax.nn.softplus(-lam.astype(jnp.float32))
    a = jnp.exp(log_a)
    u = jnp.sqrt(-jnp.expm1(2.0 * log_a)) * (i * xc)

    def combine(e1, e2):
        a1, u1 = e1
        a2, u2 = e2
        return a1 * a2, a2 * u1 + u2

    _, hs = lax.associative_scan(combine, (a, u), axis=1)
    return hs * jax.nn.gelu(gate.astype(jnp.float32), approximate=True)


def mlstm_chunked(q, k, v, i_pre, log_f):
    b, s, h, d = q.shape
    c = s // CHUNK
    l = CHUNK
    qc = q.reshape(b, c, l, h, d)
    kc = k.reshape(b, c, l, h, d)
    vc = v.reshape(b, c, l, h, d)
    ic = i_pre.reshape(b, c, l, h)
    bcum = jnp.cumsum(log_f.reshape(b, c, l, h), axis=2)
    causal = jnp.tril(jnp.ones((l, l), dtype=bool))[None, None, :, :, None]
    dmat = jnp.where(causal, bcum[:, :, :, None] - bcum[:, :, None, :] + ic[:, :, None, :], -jnp.inf)
    w_end = bcum[:, :, -1:] - bcum + ic
    m_loc = jnp.max(w_end, axis=2)
    p_end = jnp.exp(w_end - m_loc[:, :, None])
    c_loc = jnp.einsum('bcshd,bcshe->bchde', vc * p_end[..., None], kc)
    n_loc = jnp.einsum('bcsh,bcshe->bche', p_end, kc)
    g_tot = bcum[:, :, -1]

    def step(carry, inp):
        cm, nm, mm = carry
        cl, nl, ml, gl = inp
        m_new = jnp.maximum(gl + mm, ml)
        s_old = jnp.exp(gl + mm - m_new)
        s_loc = jnp.exp(ml - m_new)
        c_new = s_old[..., None, None] * cm + s_loc[..., None, None] * cl
        n_new = s_old[..., None] * nm + s_loc[..., None] * nl
        return (c_new, n_new, m_new), (cm, nm, mm)

    init = (jnp.zeros((b, h, d, d), jnp.float32), jnp.zeros((b, h, d), jnp.float32),
            jnp.zeros((b, h), jnp.float32))
    xs = (jnp.moveaxis(c_loc, 1, 0), jnp.moveaxis(n_loc, 1, 0), jnp.moveaxis(m_loc, 1, 0),
          jnp.moveaxis(g_tot, 1, 0))
    _, (c_in, n_in, m_in) = lax.scan(step, init, xs)
    c_in = jnp.moveaxis(c_in, 0, 1)
    n_in = jnp.moveaxis(n_in, 0, 1)
    m_in = jnp.moveaxis(m_in, 0, 1)
    inter_log = bcum + m_in[:, :, None]
    m_t = jnp.maximum(jnp.max(dmat, axis=3), inter_log)
    p_intra = jnp.exp(dmat - m_t[:, :, :, None])
    s_inter = jnp.exp(inter_log - m_t)
    qk = jnp.einsum('bcthd,bcshd->bctsh', qc, kc) * p_intra
    num = (jnp.einsum('bctsh,bcshd->bcthd', qk, vc)
           + s_inter[..., None] * jnp.einsum('bchde,bcthe->bcthd', c_in, qc))
    den = jnp.sum(qk, axis=3) + s_inter * jnp.einsum('bche,bcthe->bcth', n_in, qc)
    out = num / jnp.maximum(jnp.abs(den), jnp.exp(-m_t))[..., None]
    return out.reshape(b, s, h, d)


def mlstm_mixer(mx, o_pre, conv_w, conv_b, w_q, w_k, w_v, w_if, b_if, norm_w):
    b, s, _ = mx.shape
    mxf = mx.astype(jnp.float32)
    mc = jax.nn.silu(causal_conv(mx, conv_w, conv_b).astype(jnp.float32))

    def blockdiag(t, w):
        tb = t.reshape(b, s, MLSTM_NBLK, MLSTM_QKV_BLOCK)
        return jnp.einsum('bsnc,ncd->bsnd', tb, w).reshape(b, s, MLSTM_WIDTH)

    q = blockdiag(mc, w_q)
    k = blockdiag(mc, w_k)
    v = blockdiag(mxf, w_v)
    gates = jnp.concatenate([q, k, v], axis=-1) @ w_if + b_if
    i_pre, f_pre = jnp.split(gates, 2, axis=-1)
    log_f = jax.nn.log_sigmoid(f_pre)
    shp = (b, s, MLSTM_HEADS, MLSTM_HEAD_DIM)
    hh = mlstm_chunked(q.reshape(shp), k.reshape(shp) * MLSTM_KSCALE, v.reshape(shp), i_pre, log_f)
    mu = jnp.mean(hh, axis=-1, keepdims=True)
    var = jnp.mean(jnp.square(hh - mu), axis=-1, keepdims=True)
    hn = ((hh - mu) * lax.rsqrt(var + EPS)).reshape(b, s, MLSTM_WIDTH) * norm_w
    return jax.nn.sigmoid(o_pre.astype(jnp.float32)) * hn


def setup_inputs(seed: int = 0) -> dict:
    key = jax.random.key(seed)
    ks = iter(jax.random.split(key, 40))
    L = DEPTH

    def nrm(shape, scale):
        return jax.random.normal(next(ks), shape, jnp.float32) * scale

    def gain(shape):
        return 1.0 + nrm(shape, 0.02)

    x = nrm((BATCH, SEQ, D_MODEL), 1.0)
    norm1_w = gain((L, D_MODEL))
    w_in = nrm((L, D_MODEL, D_IN), D_MODEL ** -0.5)
    ssd_conv_w = nrm((L, CONV_K, SSD_XBC), 0.5)
    ssd_conv_b = nrm((L, SSD_XBC), 0.02)
    u_dt = jax.random.uniform(next(ks), (L, SSD_HEADS), jnp.float32)
    dt0 = jnp.exp(u_dt * (np.log(0.1) - np.log(0.001)) + np.log(0.001))
    ssd_dt_bias = dt0 + jnp.log(-jnp.expm1(-dt0))
    ssd_a_log = jnp.log(jax.random.uniform(next(ks), (L, SSD_HEADS), jnp.float32, 1.0, 16.0))
    ssd_d = gain((L, SSD_HEADS))
    ssd_norm_w = gain((L, SSD_WIDTH))
    lru_conv_w = nrm((L, CONV_K, LRU_WIDTH), 0.5)
    lru_conv_b = nrm((L, LRU_WIDTH), 0.02)
    lru_w_a = nrm((L, LRU_BLOCKS, LRU_BLOCK, LRU_BLOCK), LRU_BLOCK ** -0.5)
    lru_b_a = nrm((L, LRU_WIDTH), 0.02)
    lru_w_x = nrm((L, LRU_BLOCKS, LRU_BLOCK, LRU_BLOCK), LRU_BLOCK ** -0.5)
    lru_b_x = nrm((L, LRU_WIDTH), 0.02)
    a0 = jax.random.uniform(next(ks), (L, LRU_WIDTH), jnp.float32, 0.9, 0.999)
    sig = a0 ** (1.0 / LRU_C)
    lru_lambda = jnp.log(sig) - jnp.log1p(-sig)
    ml_conv_w = nrm((L, CONV_K, MLSTM_WIDTH), 0.5)
    ml_conv_b = nrm((L, MLSTM_WIDTH), 0.02)
    ml_w_q = nrm((L, MLSTM_NBLK, MLSTM_QKV_BLOCK, MLSTM_QKV_BLOCK), MLSTM_QKV_BLOCK ** -0.5)
    ml_w_k = nrm((L, MLSTM_NBLK, MLSTM_QKV_BLOCK, MLSTM_QKV_BLOCK), MLSTM_QKV_BLOCK ** -0.5)
    ml_w_v = nrm((L, MLSTM_NBLK, MLSTM_QKV_BLOCK, MLSTM_QKV_BLOCK), MLSTM_QKV_BLOCK ** -0.5)
    ml_w_if = nrm((L, 3 * MLSTM_WIDTH, 2 * MLSTM_HEADS), 0.02)
    b_i = nrm((L, MLSTM_HEADS), 0.1)
    b_f = jnp.linspace(3.0, 6.0, MLSTM_HEADS, dtype=jnp.float32)[None, :] + nrm((L, MLSTM_HEADS), 0.1)
    ml_b_if = jnp.concatenate([b_i, b_f], axis=-1)
    ml_norm_w = gain((L, MLSTM_WIDTH))
    w_out = nrm((L, D_MIX, D_MODEL), D_MIX ** -0.5)
    norm2_w = gain((L, D_MODEL))
    w_gate_up = nrm((L, D_MODEL, 2 * D_FF), D_MODEL ** -0.5)
    w_down = nrm((L, D_FF, D_MODEL), D_FF ** -0.5)
    norm_f_w = gain((D_MODEL,))
    return {"x": x, "norm1_w": norm1_w, "w_in": w_in,
            "ssd_conv_w": ssd_conv_w, "ssd_conv_b": ssd_conv_b, "ssd_dt_bias": ssd_dt_bias,
            "ssd_a_log": ssd_a_log, "ssd_d": ssd_d, "ssd_norm_w": ssd_norm_w,
            "lru_conv_w": lru_conv_w, "lru_conv_b": lru_conv_b, "lru_w_a": lru_w_a,
            "lru_b_a": lru_b_a, "lru_w_x": lru_w_x, "lru_b_x": lru_b_x, "lru_lambda": lru_lambda,
            "ml_conv_w": ml_conv_w, "ml_conv_b": ml_conv_b, "ml_w_q": ml_w_q, "ml_w_k": ml_w_k,
            "ml_w_v": ml_w_v, "ml_w_if": ml_w_if, "ml_b_if": ml_b_if, "ml_norm_w": ml_norm_w,
            "w_out": w_out, "norm2_w": norm2_w, "w_gate_up": w_gate_up, "w_down": w_down,
            "norm_f_w": norm_f_w}


def reference(x, norm1_w, w_in, ssd_conv_w, ssd_conv_b, ssd_dt_bias, ssd_a_log, ssd_d, ssd_norm_w,
              lru_conv_w, lru_conv_b, lru_w_a, lru_b_a, lru_w_x, lru_b_x, lru_lambda,
              ml_conv_w, ml_conv_b, ml_w_q, ml_w_k, ml_w_v, ml_w_if, ml_b_if, ml_norm_w,
              w_out, norm2_w, w_gate_up, w_down, norm_f_w):
    h = x
    for l in range(DEPTH):
        u = rmsnorm(h, norm1_w[l])
        proj = jnp.einsum('bsd,de->bse', u, w_in[l])
        z, xbc, dt_raw, lru_gate, lru_x, ml_x, ml_o = jnp.split(proj, IN_SPLITS, axis=-1)
        y_ssd = ssd_mixer(z, xbc, dt_raw, ssd_conv_w[l], ssd_conv_b[l], ssd_dt_bias[l],
                          ssd_a_log[l], ssd_d[l], ssd_norm_w[l])
        y_lru = rglru_mixer(lru_gate, lru_x, lru_conv_w[l], lru_conv_b[l], lru_w_a[l], lru_b_a[l],
                            lru_w_x[l], lru_b_x[l], lru_lambda[l])
        y_ml = mlstm_mixer(ml_x, ml_o, ml_conv_w[l], ml_conv_b[l], ml_w_q[l], ml_w_k[l], ml_w_v[l],
                           ml_w_if[l], ml_b_if[l], ml_norm_w[l])
        y = jnp.concatenate([y_ssd, y_lru, y_ml], axis=-1).astype(h.dtype)
        h = h + jnp.einsum('bse,ed->bsd', y, w_out[l])
        u = rmsnorm(h, norm2_w[l])
        gt, up = jnp.split(jnp.einsum('bsd,df->bsf', u, w_gate_up[l]), 2, axis=-1)
        h = h + jnp.einsum('bsf,fd->bsd', jax.nn.silu(gt) * up, w_down[l])
    return rmsnorm(h, norm_f_w)
```

```python
import functools

import jax
import jax.numpy as jnp
from jax import lax
from jax.experimental import pallas as pl
from jax.experimental.pallas import tpu as pltpu

F32 = jnp.float32
BF16 = jnp.bfloat16

EPS = 1e-6
CONV_K = 4
CHUNK = 128
TAIL = 8
LANES = 128
MXU_DIM = 256

SSD_WIDTH = 2048
SSD_HEAD_DIM = 64
SSD_HEADS = 32
SSD_GROUPS = 8
SSD_STATE = 128
SSD_XBC = SSD_WIDTH + 2 * SSD_GROUPS * SSD_STATE
SSD_GROUP_WIDTH = SSD_WIDTH // SSD_GROUPS
SSD_HEADS_PER_GROUP = SSD_HEADS // SSD_GROUPS
LRU_WIDTH = 1024
LRU_BLOCK = 64
LRU_C = 8.0
ML_WIDTH = 1024
ML_HEADS = 4
ML_HEAD_DIM = 256
ML_QKV_BLOCK = 4
ML_KSCALE = ML_HEAD_DIM ** -0.5

COL_XBC = 0
COL_Z = COL_XBC + SSD_XBC
COL_LRU_GATE = COL_Z + SSD_WIDTH
COL_LRU_X = COL_LRU_GATE + LRU_WIDTH
COL_ML_X = COL_LRU_X + LRU_WIDTH
COL_ML_O = COL_ML_X + ML_WIDTH
COL_DT = COL_ML_O + ML_WIDTH
D_PROJ = COL_DT + LANES

VMEM_LIMIT_BYTES = 56 * 1024 * 1024


def _params(*semantics):
    return pltpu.CompilerParams(dimension_semantics=semantics, vmem_limit_bytes=VMEM_LIMIT_BYTES)


def _dot(a, b):
    return jnp.dot(a, b, preferred_element_type=F32)


def _dot_nt(a, b):
    return lax.dot_general(a, b, (((1,), (1,)), ((), ())), preferred_element_type=F32)


def _dot_tn(a, b):
    return lax.dot_general(a, b, (((0,), (0,)), ((), ())), preferred_element_type=F32)


def _split3(x):
    hi = x.astype(BF16)
    r1 = x - hi.astype(F32)
    mid = r1.astype(BF16)
    lo = (r1 - mid.astype(F32)).astype(BF16)
    return hi, mid, lo


def _dot_exact_lhs01(m01, x):
    hi, mid, lo = _split3(x)
    return (_dot(m01, hi) + _dot(m01, mid)) + _dot(m01, lo)


def _dot_exact_rhs01(x, m01):
    hi, mid, lo = _split3(x)
    return (_dot(hi, m01) + _dot(mid, m01)) + _dot(lo, m01)


def _sigmoid(x):
    return 1.0 / (1.0 + jnp.exp(-x))


def _silu(x):
    return x * _sigmoid(x)


def _softplus(x):
    return jnp.maximum(x, 0.0) + jnp.log1p(jnp.exp(-jnp.abs(x)))


def _rmsnorm(x, w):
    return x * lax.rsqrt(jnp.mean(x * x, axis=-1, keepdims=True) + EPS) * w


def _causal_conv(cur, buf_ref, w_ref, b_ref, first_chunk):
    rows = cur.shape[0]

    @pl.when(first_chunk)
    def _():
        buf_ref[0:TAIL, :] = jnp.zeros((TAIL, cur.shape[1]), F32)

    buf_ref[TAIL:TAIL + rows, :] = cur
    acc = cur * w_ref[CONV_K - 1:CONV_K, :] + b_ref[...]
    for s in range(1, CONV_K):
        acc = acc + buf_ref[TAIL - s:TAIL - s + rows, :] * w_ref[CONV_K - 1 - s:CONV_K - s, :]
    buf_ref[0:TAIL, :] = cur[rows - TAIL:rows, :]
    return acc


def _tril_ones(n):
    r = lax.broadcasted_iota(jnp.int32, (n, n), 0)
    c = lax.broadcasted_iota(jnp.int32, (n, n), 1)
    return r >= c


def _norm_matmul_kernel(x_ref, nw_ref, w_ref, o_ref, u_ref):
    @pl.when(pl.program_id(1) == 0)
    def _():
        u_ref[...] = _rmsnorm(x_ref[...], nw_ref[...]).astype(BF16)

    o_ref[...] = _dot(u_ref[...], w_ref[...])


def _norm_matmul(x, nw, w, tm, tn):
    m, k = x.shape
    n = w.shape[1]
    return pl.pallas_call(
        _norm_matmul_kernel,
        grid=(m // tm, n // tn),
        in_specs=[pl.BlockSpec((tm, k), lambda i, j: (i, 0)),
                  pl.BlockSpec((1, k), lambda i, j: (0, 0)),
                  pl.BlockSpec((k, tn), lambda i, j: (0, j))],
        out_specs=pl.BlockSpec((tm, tn), lambda i, j: (i, j)),
        out_shape=jax.ShapeDtypeStruct((m, n), F32),
        scratch_shapes=[pltpu.VMEM((tm, k), BF16)],
        compiler_params=_params("parallel", "arbitrary"),
        name="norm_in_proj",
    )(x, nw, w)


def _ssd_kernel(xbc_ref, z_ref, dt_ref, cw_ref, cb_ref, dtb_ref, alog_ref, dsk_ref, nw_ref, e_ref,
                y_ref, buf_ref, state_ref):
    first = pl.program_id(1) == 0

    @pl.when(first)
    def _():
        state_ref[...] = jnp.zeros(state_ref.shape, F32)

    xbc = _silu(_causal_conv(xbc_ref[...], buf_ref, cw_ref, cb_ref, first))
    xs = xbc[:, :SSD_WIDTH]
    bm = xbc[:, SSD_WIDTH:SSD_WIDTH + SSD_GROUPS * SSD_STATE].astype(BF16)
    cm = xbc[:, SSD_WIDTH + SSD_GROUPS * SSD_STATE:].astype(BF16)

    dt = _softplus(dt_ref[...] + dtb_ref[...])
    da = dt * (-jnp.exp(alog_ref[...]))
    causal = _tril_ones(CHUNK)
    cs = _dot_exact_lhs01(causal.astype(BF16), da)
    cs_t = cs.T

    expand = e_ref[...]
    dt_e = _dot_exact_rhs01(dt, expand)
    cs_e = _dot_exact_rhs01(cs, expand)
    cs_last_e = cs_e[CHUNK - 1:CHUNK, :]
    xd = xs * dt_e
    xd_to_end = (xd * jnp.exp(cs_last_e - cs_e)).astype(BF16)
    decay_from_start = jnp.exp(cs_e)
    chunk_decay = jnp.exp(cs_last_e)

    head_of_lane = lax.broadcasted_iota(jnp.int32, (CHUNK, SSD_GROUP_WIDTH), 1) // SSD_HEAD_DIM
    ys = []
    for g in range(SSD_GROUPS):
        gs = slice(g * SSD_GROUP_WIDTH, (g + 1) * SSD_GROUP_WIDTH)
        b_g = bm[:, g * SSD_STATE:(g + 1) * SSD_STATE]
        c_g = cm[:, g * SSD_STATE:(g + 1) * SSD_STATE]
        scores = _dot_nt(c_g, b_g)
        xd_g = xd[:, gs]
        lhs, rhs = [], []
        for j in range(SSD_HEADS_PER_GROUP):
            h = g * SSD_HEADS_PER_GROUP + j
            seg = cs[:, h:h + 1] - cs_t[h:h + 1, :]
            decay = jnp.exp(jnp.where(causal, seg, -jnp.inf))
            lhs.append((scores * decay).astype(BF16))
            rhs.append(jnp.where(head_of_lane == j, xd_g, 0.0).astype(BF16))
        y_diag = _dot(jnp.concatenate(lhs, axis=1), jnp.concatenate(rhs, axis=0))
        state = state_ref[g]
        y_off = _dot(c_g, state.astype(BF16)) * decay_from_start[:, gs]
        state_ref[g] = state * chunk_decay[:, gs] + _dot_tn(b_g, xd_to_end[:, gs])
        ys.append(y_diag + y_off)

    y = jnp.concatenate(ys, axis=1) + dsk_ref[...] * xs
    y = y * _silu(z_ref[...])
    outs = []
    for g in range(SSD_GROUPS):
        yg = y[:, g * SSD_GROUP_WIDTH:(g + 1) * SSD_GROUP_WIDTH]
        outs.append(yg * lax.rsqrt(jnp.mean(yg * yg, axis=-1, keepdims=True) + EPS))
    y_ref[...] = (jnp.concatenate(outs, axis=1) * nw_ref[...]).astype(BF16)


def _ssd_mixer(proj, conv_w, conv_b, dt_bias, a_log, d_skip, norm_w):
    b, s, _ = proj.shape
    pad = LANES - SSD_HEADS
    dtb = jnp.pad(dt_bias, (0, pad))[None, :]
    alog = jnp.pad(a_log, (0, pad))[None, :]
    dsk = jnp.repeat(d_skip, SSD_HEAD_DIM)[None, :]
    head = jnp.arange(LANES)[:, None]
    expand = (head == (jnp.arange(SSD_WIDTH)[None, :] // SSD_HEAD_DIM)).astype(BF16)

    def const(shape):
        return pl.BlockSpec(shape, lambda i, c: (0, 0))

    return pl.pallas_call(
        _ssd_kernel,
        grid=(b, s // CHUNK),
        in_specs=[pl.BlockSpec((None, CHUNK, SSD_XBC), lambda i, c: (i, c, COL_XBC // SSD_XBC)),
                  pl.BlockSpec((None, CHUNK, SSD_WIDTH), lambda i, c: (i, c, COL_Z // SSD_WIDTH)),
                  pl.BlockSpec((None, CHUNK, LANES), lambda i, c: (i, c, COL_DT // LANES)),
                  const((CONV_K, SSD_XBC)), const((1, SSD_XBC)), const((1, LANES)), const((1, LANES)),
                  const((1, SSD_WIDTH)), const((1, SSD_WIDTH)), const((LANES, SSD_WIDTH))],
        out_specs=pl.BlockSpec((None, CHUNK, SSD_WIDTH), lambda i, c: (i, c, 0)),
        out_shape=jax.ShapeDtypeStruct((b, s, SSD_WIDTH), BF16),
        scratch_shapes=[pltpu.VMEM((TAIL + CHUNK, SSD_XBC), F32),
                        pltpu.VMEM((SSD_GROUPS, SSD_STATE, SSD_GROUP_WIDTH), F32)],
        compiler_params=_params("parallel", "arbitrary"),
        name="ssd_mixer",
    )(proj, proj, proj, conv_w, conv_b[None, :], dtb, alog, dsk, norm_w[None, :], expand)


def _lru_kernel(gate_ref, x_ref, cw_ref, cb_ref, wa_ref, ba_ref, wx_ref, bx_ref, lam_ref,
                y_ref, buf_ref, a_ref, u_ref, h_ref):
    first = pl.program_id(1) == 0
    rows = x_ref.shape[0]

    @pl.when(first)
    def _():
        h_ref[...] = jnp.zeros(h_ref.shape, F32)

    xc = _causal_conv(x_ref[...], buf_ref, cw_ref, cb_ref, first)
    xcb = xc.astype(BF16)
    ra, ix = [], []
    for q in range(LRU_WIDTH // MXU_DIM):
        blk = xcb[:, q * MXU_DIM:(q + 1) * MXU_DIM]
        ra.append(_dot(blk, wa_ref[q]))
        ix.append(_dot(blk, wx_ref[q]))
    r = _sigmoid(jnp.concatenate(ra, axis=1) + ba_ref[...])
    i = _sigmoid(jnp.concatenate(ix, axis=1) + bx_ref[...])
    log_a = (-LRU_C * r) * _softplus(-lam_ref[...])
    a_ref[...] = jnp.exp(log_a)
    u_ref[...] = jnp.sqrt(1.0 - jnp.exp(2.0 * log_a)) * (i * xc)

    def step(t, h):
        h = a_ref[pl.ds(t, 1), :] * h + u_ref[pl.ds(t, 1), :]
        u_ref[pl.ds(t, 1), :] = h
        return h

    h_ref[0:1, :] = lax.fori_loop(0, rows, step, h_ref[0:1, :], unroll=8)
    y_ref[...] = (u_ref[...] * jax.nn.gelu(gate_ref[...], approximate=True)).astype(BF16)


def _lru_mixer(proj, conv_w, conv_b, w_a, b_a, w_x, b_x, lam):
    b, s, _ = proj.shape
    rows = CHUNK
    nq = LRU_WIDTH // MXU_DIM

    def const(shape):
        return pl.BlockSpec(shape, lambda i, c: (0,) * len(shape))

    return pl.pallas_call(
        _lru_kernel,
        grid=(b, s // rows),
        in_specs=[pl.BlockSpec((None, rows, LRU_WIDTH), lambda i, c: (i, c, COL_LRU_GATE // LRU_WIDTH)),
                  pl.BlockSpec((None, rows, LRU_WIDTH), lambda i, c: (i, c, COL_LRU_X // LRU_WIDTH)),
                  const((CONV_K, LRU_WIDTH)), const((1, LRU_WIDTH)),
                  const((nq, MXU_DIM, MXU_DIM)), const((1, LRU_WIDTH)),
                  const((nq, MXU_DIM, MXU_DIM)), const((1, LRU_WIDTH)), const((1, LRU_WIDTH))],
        out_specs=pl.BlockSpec((None, rows, LRU_WIDTH), lambda i, c: (i, c, 0)),
        out_shape=jax.ShapeDtypeStruct((b, s, LRU_WIDTH), BF16),
        scratch_shapes=[pltpu.VMEM((TAIL + rows, LRU_WIDTH), F32),
                        pltpu.VMEM((rows, LRU_WIDTH), F32),
                        pltpu.VMEM((rows, LRU_WIDTH), F32),
                        pltpu.VMEM((TAIL, LRU_WIDTH), F32)],
        compiler_params=_params("parallel", "arbitrary"),
        name="rglru_mixer",
    )(proj, proj, conv_w, conv_b[None, :], w_a, b_a[None, :], w_x, b_x[None, :], lam[None, :])


def _splat(x11, shape):
    return jnp.broadcast_to(x11, shape)


def _mlstm_kernel(x_ref, o_ref, cw_ref, cb_ref, wq_ref, wk_ref, wv_ref, wif_ref, bif_ref, nw_ref,
                  y_ref, buf_ref, c_ref, n_ref, m_ref):
    first = pl.program_id(1) == 0

    @pl.when(first)
    def _():
        c_ref[...] = jnp.zeros(c_ref.shape, F32)
        n_ref[...] = jnp.zeros(n_ref.shape, F32)
        m_ref[...] = jnp.zeros(m_ref.shape, F32)

    mx = x_ref[...]
    mc = _silu(_causal_conv(mx, buf_ref, cw_ref, cb_ref, first)).astype(BF16)
    mxb = mx.astype(BF16)
    qs, ks, vs = [], [], []
    for h in range(ML_HEADS):
        hs = slice(h * ML_HEAD_DIM, (h + 1) * ML_HEAD_DIM)
        qs.append(_dot(mc[:, hs], wq_ref[h]))
        ks.append(_dot(mc[:, hs], wk_ref[h]))
        vs.append(_dot(mxb[:, hs], wv_ref[h]))

    gates = bif_ref[...]
    for part, vals in enumerate((qs, ks, vs)):
        for h in range(ML_HEADS):
            r0 = part * ML_WIDTH + h * ML_HEAD_DIM
            gates = gates + _dot(vals[h].astype(BF16), wif_ref[r0:r0 + ML_HEAD_DIM, :])
    i_pre = gates[:, :LANES]
    log_f = -_softplus(-gates[:, LANES:])

    causal = _tril_ones(CHUNK)
    bcum = _dot_exact_lhs01(causal.astype(BF16), log_f)
    bcum_t = bcum.T
    i_t = i_pre.T
    g_tot = bcum[CHUNK - 1:CHUNK, :]
    w_end = g_tot - bcum + i_pre

    for h in range(ML_HEADS):
        hs = slice(h * ML_HEAD_DIM, (h + 1) * ML_HEAD_DIM)
        q = qs[h]
        qb = q.astype(BF16)
        kb = (ks[h] * ML_KSCALE).astype(BF16)
        v = vs[h]
        c_in = c_ref[h]
        n_in = n_ref[h][0:1, :]
        m_in = m_ref[h][0:1, 0:1]

        bc = bcum[:, h:h + 1]
        dmat = jnp.where(causal, bc - bcum_t[h:h + 1, :] + i_t[h:h + 1, :], -jnp.inf)
        inter_log = bc + m_in
        m_t = jnp.maximum(jnp.max(dmat, axis=1, keepdims=True), inter_log)
        p_intra = jnp.exp(dmat - m_t)
        s_inter = jnp.exp(inter_log - m_t)
        qk = _dot_nt(qb, kb) * p_intra
        num = _dot(qk.astype(BF16), v.astype(BF16)) + s_inter * _dot(qb, c_in.astype(BF16))
        den = (jnp.sum(qk, axis=1, keepdims=True)
               + s_inter * jnp.sum(q * n_in, axis=1, keepdims=True))
        out = num / jnp.maximum(jnp.abs(den), jnp.exp(-m_t))

        mu = jnp.mean(out, axis=1, keepdims=True)
        cen = out - mu
        var = jnp.mean(cen * cen, axis=1, keepdims=True)
        hn = cen * lax.rsqrt(var + EPS) * nw_ref[:, hs]
        y_ref[:, hs] = (_sigmoid(o_ref[:, hs]) * hn).astype(BF16)

        we = w_end[:, h:h + 1]
        m_loc = jnp.max(we, axis=0, keepdims=True)
        p_end = jnp.exp(we - m_loc)
        kf = ks[h] * ML_KSCALE
        c_loc = _dot_tn(kb, (v * p_end).astype(BF16))
        n_loc = jnp.sum(kf * p_end, axis=0, keepdims=True)
        g_h = g_tot[:, h:h + 1]
        m_new = jnp.maximum(g_h + m_in, m_loc)
        s_old = jnp.exp(g_h + m_in - m_new)
        s_loc = jnp.exp(m_loc - m_new)
        c_ref[h] = s_old * c_in + s_loc * c_loc
        n_ref[h] = _splat(s_old * n_in + s_loc * n_loc, (TAIL, ML_HEAD_DIM))
        m_ref[h] = _splat(m_new, (TAIL, LANES))


def _mlstm_mixer(proj, conv_w, conv_b, wq, wk, wv, w_if, b_if, norm_w):
    b, s, _ = proj.shape

    def const(shape):
        return pl.BlockSpec(shape, lambda i, c: (0,) * len(shape))

    return pl.pallas_call(
        _mlstm_kernel,
        grid=(b, s // CHUNK),
        in_specs=[pl.BlockSpec((None, CHUNK, ML_WIDTH), lambda i, c: (i, c, COL_ML_X // ML_WIDTH)),
                  pl.BlockSpec((None, CHUNK, ML_WIDTH), lambda i, c: (i, c, COL_ML_O // ML_WIDTH)),
                  const((CONV_K, ML_WIDTH)), const((1, ML_WIDTH)),
                  const((ML_HEADS, ML_HEAD_DIM, ML_HEAD_DIM)),
                  const((ML_HEADS, ML_HEAD_DIM, ML_HEAD_DIM)),
                  const((ML_HEADS, ML_HEAD_DIM, ML_HEAD_DIM)),
                  const((3 * ML_WIDTH, 2 * LANES)), const((1, 2 * LANES)), const((1, ML_WIDTH))],
        out_specs=pl.BlockSpec((None, CHUNK, ML_WIDTH), lambda i, c: (i, c, 0)),
        out_shape=jax.ShapeDtypeStruct((b, s, ML_WIDTH), BF16),
        scratch_shapes=[pltpu.VMEM((TAIL + CHUNK, ML_WIDTH), F32),
                        pltpu.VMEM((ML_HEADS, ML_HEAD_DIM, ML_HEAD_DIM), F32),
                        pltpu.VMEM((ML_HEADS, TAIL, ML_HEAD_DIM), F32),
                        pltpu.VMEM((ML_HEADS, TAIL, LANES), F32)],
        compiler_params=_params("parallel", "arbitrary"),
        name="mlstm_mixer",
    )(proj, proj, conv_w, conv_b[None, :], wq, wk, wv, w_if, b_if, norm_w[None, :])


def _out_proj_kernel(h_ref, y1_ref, y2_ref, y3_ref, w1_ref, w2_ref, w3_ref, o_ref):
    acc = _dot(y1_ref[...], w1_ref[...]) + _dot(y2_ref[...], w2_ref[...])
    o_ref[...] = h_ref[...] + (acc + _dot(y3_ref[...], w3_ref[...]))


def _out_proj(h, y_ssd, y_lru, y_ml, w_out, tm, tn):
    m, n = h.shape
    return pl.pallas_call(
        _out_proj_kernel,
        grid=(n // tn, m // tm),
        in_specs=[pl.BlockSpec((tm, tn), lambda j, i: (i, j)),
                  pl.BlockSpec((tm, SSD_WIDTH), lambda j, i: (i, 0)),
                  pl.BlockSpec((tm, LRU_WIDTH), lambda j, i: (i, 0)),
                  pl.BlockSpec((tm, ML_WIDTH), lambda j, i: (i, 0)),
                  pl.BlockSpec((SSD_WIDTH, tn), lambda j, i: (0, j)),
                  pl.BlockSpec((LRU_WIDTH, tn), lambda j, i: (SSD_WIDTH // LRU_WIDTH, j)),
                  pl.BlockSpec((ML_WIDTH, tn), lambda j, i: ((SSD_WIDTH + LRU_WIDTH) // ML_WIDTH, j))],
        out_specs=pl.BlockSpec((tm, tn), lambda j, i: (i, j)),
        out_shape=jax.ShapeDtypeStruct((m, n), F32),
        compiler_params=_params("parallel", "parallel"),
        name="out_proj",
    )(h, y_ssd, y_lru, y_ml, w_out, w_out, w_out)


def _ffn_kernel(h_ref, nw_ref, wg_ref, wu_ref, wd_ref, fw_ref, o_ref, u_ref, acc_ref, *, final_norm):
    j = pl.program_id(1)

    @pl.when(j == 0)
    def _():
        u_ref[...] = _rmsnorm(h_ref[...], nw_ref[...]).astype(BF16)
        acc_ref[...] = jnp.zeros(acc_ref.shape, F32)

    u = u_ref[...]
    gate = _dot(u, wg_ref[...])
    up = _dot(u, wu_ref[...])
    acc_ref[...] += _dot((_silu(gate) * up).astype(BF16), wd_ref[...])

    @pl.when(j == pl.num_programs(1) - 1)
    def _():
        out = h_ref[...] + acc_ref[...]
        if final_norm:
            out = _rmsnorm(out, fw_ref[...])
        o_ref[...] = out


def _ffn(h, nw, w_gate_up, w_down, fw, tm, tf, final_norm):
    m, d = h.shape
    f = w_down.shape[0]
    nf = f // tf
    return pl.pallas_call(
        functools.partial(_ffn_kernel, final_norm=final_norm),
        grid=(m // tm, nf),
        in_specs=[pl.BlockSpec((tm, d), lambda i, j: (i, 0)),
                  pl.BlockSpec((1, d), lambda i, j: (0, 0)),
                  pl.BlockSpec((d, tf), lambda i, j: (0, j)),
                  pl.BlockSpec((d, tf), lambda i, j: (0, j + nf)),
                  pl.BlockSpec((tf, d), lambda i, j: (j, 0)),
                  pl.BlockSpec((1, d), lambda i, j: (0, 0))],
        out_specs=pl.BlockSpec((tm, d), lambda i, j: (i, 0)),
        out_shape=jax.ShapeDtypeStruct((m, d), F32),
        scratch_shapes=[pltpu.VMEM((tm, d), BF16), pltpu.VMEM((tm, d), F32)],
        compiler_params=_params("parallel", "arbitrary"),
        name="ffn_final" if final_norm else "ffn",
    )(h, nw, w_gate_up, w_gate_up, w_down, fw)


def _block_diag_tiles(w, tile):
    nblk, c, _ = w.shape
    per = tile // c
    w4 = w.reshape(nblk // per, per, c, c)
    eye = jnp.eye(per, dtype=w.dtype)
    return jnp.einsum('qicd,ij->qicjd', w4, eye).reshape(nblk // per, tile, tile)


def _reorder_w_in(w):
    o_z, o_xbc = 0, SSD_WIDTH
    o_dt = o_xbc + SSD_XBC
    o_rest = o_dt + SSD_HEADS
    pad = jnp.zeros((w.shape[0], LANES - SSD_HEADS), w.dtype)
    return jnp.concatenate([w[:, o_xbc:o_dt], w[:, o_z:o_xbc], w[:, o_rest:], w[:, o_dt:o_rest], pad],
                           axis=1)


def _pad_gate_weights(w_if, b_if):
    zw = jnp.zeros((w_if.shape[0], LANES - ML_HEADS), w_if.dtype)
    w = jnp.concatenate([w_if[:, :ML_HEADS], zw, w_if[:, ML_HEADS:], zw], axis=1)
    zb = jnp.zeros((LANES - ML_HEADS,), b_if.dtype)
    b = jnp.concatenate([b_if[:ML_HEADS], zb, b_if[ML_HEADS:], zb])[None, :]
    return w, b


def _tile(total, want):
    return want if total % want == 0 else total


def kernel(x, norm1_w, w_in, ssd_conv_w, ssd_conv_b, ssd_dt_bias, ssd_a_log, ssd_d, ssd_norm_w, lru_conv_w, lru_conv_b, lru_w_a, lru_b_a, lru_w_x, lru_b_x, lru_lambda, ml_conv_w, ml_conv_b, ml_w_q, ml_w_k, ml_w_v, ml_w_if, ml_b_if, ml_norm_w, w_out, norm2_w, w_gate_up, w_down, norm_f_w):
    b, s, d = x.shape
    m = b * s
    depth = w_in.shape[0]
    tm = _tile(m, 512)
    h = x.reshape(m, d)
    for l in range(depth):
        w_in_l = _reorder_w_in(w_in[l]).astype(BF16)
        proj = _norm_matmul(h, norm1_w[l][None, :], w_in_l, tm, D_PROJ // 9).reshape(b, s, D_PROJ)
        y_ssd = _ssd_mixer(proj, ssd_conv_w[l], ssd_conv_b[l], ssd_dt_bias[l], ssd_a_log[l], ssd_d[l],
                           ssd_norm_w[l])
        y_lru = _lru_mixer(proj, lru_conv_w[l], lru_conv_b[l],
                           _block_diag_tiles(lru_w_a[l], MXU_DIM).astype(BF16), lru_b_a[l],
                           _block_diag_tiles(lru_w_x[l], MXU_DIM).astype(BF16), lru_b_x[l],
                           lru_lambda[l])
        w_if, b_if = _pad_gate_weights(ml_w_if[l], ml_b_if[l])
        y_ml = _mlstm_mixer(proj, ml_conv_w[l], ml_conv_b[l],
                            _block_diag_tiles(ml_w_q[l], ML_HEAD_DIM).astype(BF16),
                            _block_diag_tiles(ml_w_k[l], ML_HEAD_DIM).astype(BF16),
                            _block_diag_tiles(ml_w_v[l], ML_HEAD_DIM).astype(BF16),
                            w_if.astype(BF16), b_if, ml_norm_w[l])
        h = _out_proj(h, y_ssd.reshape(m, -1), y_lru.reshape(m, -1), y_ml.reshape(m, -1),
                      w_out[l].astype(BF16), tm, 1024)
        h = _ffn(h, norm2_w[l][None, :], w_gate_up[l].astype(BF16), w_down[l].astype(BF16),
                 norm_f_w[None, :], tm, 512, final_norm=(l == depth - 1))
    return h.reshape(b, s, d)
```

```python
import functools

import jax
import jax.numpy as jnp
from jax import lax
from jax.experimental import pallas as pl
from jax.experimental.pallas import tpu as pltpu

F32 = jnp.float32
BF16 = jnp.bfloat16

EPS = 1e-6
LOG2E = 1.4426950408889634
CONV_K = 4
CHUNK = 128
TAIL = 8
LANES = 128
MXU_DIM = 256

SSD_WIDTH = 2048
SSD_HEAD_DIM = 64
SSD_HEADS = 32
SSD_GROUPS = 8
SSD_STATE = 128
SSD_BC = SSD_GROUPS * SSD_STATE
SSD_XBC = SSD_WIDTH + 2 * SSD_BC
SSD_GROUP_WIDTH = SSD_WIDTH // SSD_GROUPS
SSD_HEADS_PER_GROUP = SSD_HEADS // SSD_GROUPS
LRU_WIDTH = 1024
LRU_C = 8.0
ML_WIDTH = 1024
ML_HEADS = 4
ML_HEAD_DIM = 256
ML_KSCALE = ML_HEAD_DIM ** -0.5
D_MIX = SSD_WIDTH + LRU_WIDTH + ML_WIDTH

COL_XBC = 0
COL_LRU_X = COL_XBC + SSD_XBC
COL_ML_X = COL_LRU_X + LRU_WIDTH
CONV_WIDTH = COL_ML_X + ML_WIDTH
COL_Z = CONV_WIDTH
COL_LRU_GATE = COL_Z + SSD_WIDTH
COL_ML_O = COL_LRU_GATE + LRU_WIDTH
COL_DT = COL_ML_O + ML_WIDTH
D_PROJ = COL_DT + LANES
OUT_SSD = 0
OUT_LRU = OUT_SSD + SSD_WIDTH
OUT_ML = OUT_LRU + LRU_WIDTH

VMEM_LIMIT_BYTES = 56 * 1024 * 1024


def _params(*semantics):
    return pltpu.CompilerParams(dimension_semantics=semantics, vmem_limit_bytes=VMEM_LIMIT_BYTES)


def _dot(a, b):
    return jnp.dot(a, b, preferred_element_type=F32)


def _dot_nt(a, b):
    return lax.dot_general(a, b, (((1,), (1,)), ((), ())), preferred_element_type=F32)


def _dot_tn(a, b):
    return lax.dot_general(a, b, (((0,), (0,)), ((), ())), preferred_element_type=F32)


def _split3(x):
    hi = x.astype(BF16)
    r1 = x - hi.astype(F32)
    mid = r1.astype(BF16)
    lo = (r1 - mid.astype(F32)).astype(BF16)
    return hi, mid, lo


def _dot_exact_lhs01(m01, x):
    hi, mid, lo = _split3(x)
    return (_dot(m01, hi) + _dot(m01, mid)) + _dot(m01, lo)


def _dot_exact_rhs01(x, m01):
    hi, mid, lo = _split3(x)
    return (_dot(hi, m01) + _dot(mid, m01)) + _dot(lo, m01)


def _sigmoid(x):
    return 1.0 / (1.0 + jnp.exp2(x * (-LOG2E)))


def _silu(x):
    return x * _sigmoid(x)


def _softplus(x):
    return jnp.maximum(x, 0.0) + jnp.log1p(jnp.exp(-jnp.abs(x)))


def _rmsnorm(x, w):
    return x * lax.rsqrt(jnp.mean(x * x, axis=-1, keepdims=True) + EPS) * w


def _tril_ones(n):
    r = lax.broadcasted_iota(jnp.int32, (n, n), 0)
    c = lax.broadcasted_iota(jnp.int32, (n, n), 1)
    return r >= c


def _norm_matmul_kernel(x_ref, nw_ref, w_ref, o_ref, u_ref):
    @pl.when(pl.program_id(1) == 0)
    def _():
        u_ref[...] = _rmsnorm(x_ref[...], nw_ref[...]).astype(BF16)

    o_ref[...] = _dot(u_ref[...], w_ref[...])


def _norm_matmul(x, nw, w, tm, tn):
    m, k = x.shape
    n = w.shape[1]
    return pl.pallas_call(
        _norm_matmul_kernel,
        grid=(m // tm, n // tn),
        in_specs=[pl.BlockSpec((tm, k), lambda i, j: (i, 0)),
                  pl.BlockSpec((1, k), lambda i, j: (0, 0)),
                  pl.BlockSpec((k, tn), lambda i, j: (0, j))],
        out_specs=pl.BlockSpec((tm, tn), lambda i, j: (i, j)),
        out_shape=jax.ShapeDtypeStruct((m, n), F32),
        scratch_shapes=[pltpu.VMEM((tm, k), BF16)],
        compiler_params=_params("parallel", "arbitrary"),
        name="norm_in_proj",
    )(x, nw, w)


def _causal_conv(cur, buf_ref, w_ref, b_ref):
    rows = cur.shape[0]
    buf_ref[TAIL:TAIL + rows, :] = cur
    acc = cur * w_ref[CONV_K - 1:CONV_K, :] + b_ref[...]
    for s in range(1, CONV_K):
        acc = acc + buf_ref[TAIL - s:TAIL - s + rows, :] * w_ref[CONV_K - 1 - s:CONV_K - s, :]
    buf_ref[0:TAIL, :] = cur[rows - TAIL:rows, :]
    return acc


def _ssd_body(xbc, z, dt_raw, dtb_ref, alog_ref, dsk_ref, nw_ref, e_ref, state_ref, y_ref):
    xs = xbc[:, :SSD_WIDTH]
    bm = xbc[:, SSD_WIDTH:SSD_WIDTH + SSD_BC].astype(BF16)
    cm = xbc[:, SSD_WIDTH + SSD_BC:].astype(BF16)

    dt = _softplus(dt_raw + dtb_ref[...])
    da = dt * (-jnp.exp(alog_ref[...]))
    causal = _tril_ones(CHUNK)
    cs = _dot_exact_lhs01(causal.astype(BF16), da) * LOG2E
    cs_t = cs.T

    expand = e_ref[...]
    dt_e = _dot_exact_rhs01(dt, expand)
    cs_e = _dot_exact_rhs01(cs, expand)
    cs_last_e = cs_e[CHUNK - 1:CHUNK, :]
    xd = xs * dt_e
    xd_to_end = (xd * jnp.exp2(cs_last_e - cs_e)).astype(BF16)
    decay_from_start = jnp.exp2(cs_e)
    chunk_decay = jnp.exp2(cs_last_e)

    head_of_lane = lax.broadcasted_iota(jnp.int32, (CHUNK, SSD_GROUP_WIDTH), 1) // SSD_HEAD_DIM
    ys = []
    for g in range(SSD_GROUPS):
        gs = slice(g * SSD_GROUP_WIDTH, (g + 1) * SSD_GROUP_WIDTH)
        b_g = bm[:, g * SSD_STATE:(g + 1) * SSD_STATE]
        c_g = cm[:, g * SSD_STATE:(g + 1) * SSD_STATE]
        scores = _dot_nt(c_g, b_g)
        xd_g = xd[:, gs]
        lhs, rhs = [], []
        for j in range(SSD_HEADS_PER_GROUP):
            h = g * SSD_HEADS_PER_GROUP + j
            seg = cs[:, h:h + 1] - cs_t[h:h + 1, :]
            decay = jnp.exp2(jnp.where(causal, seg, -jnp.inf))
            lhs.append((scores * decay).astype(BF16))
            rhs.append(jnp.where(head_of_lane == j, xd_g, 0.0).astype(BF16))
        y_diag = _dot(jnp.concatenate(lhs, axis=1), jnp.concatenate(rhs, axis=0))
        state = state_ref[g]
        y_off = _dot(c_g, state.astype(BF16)) * decay_from_start[:, gs]
        state_ref[g] = state * chunk_decay[:, gs] + _dot_tn(b_g, xd_to_end[:, gs])
        ys.append(y_diag + y_off)

    y = jnp.concatenate(ys, axis=1) + dsk_ref[...] * xs
    y = y * _silu(z)
    for g in range(SSD_GROUPS):
        gs = slice(g * SSD_GROUP_WIDTH, (g + 1) * SSD_GROUP_WIDTH)
        yg = y[:, gs]
        yn = yg * lax.rsqrt(jnp.mean(yg * yg, axis=-1, keepdims=True) + EPS)
        y_ref[:, OUT_SSD + gs.start:OUT_SSD + gs.stop] = (yn * nw_ref[:, gs]).astype(BF16)


def _splat(x11, shape):
    return jnp.broadcast_to(x11, shape)


def _mlstm_body(mx, mc, o_pre_ref, wq_ref, wk_ref, wv_ref, wif_ref, bif_ref, nw_ref,
                c_ref, n_ref, m_ref, y_ref):
    mcb = mc.astype(BF16)
    mxb = mx.astype(BF16)
    qs, ks, vs = [], [], []
    for h in range(ML_HEADS):
        hs = slice(h * ML_HEAD_DIM, (h + 1) * ML_HEAD_DIM)
        qs.append(_dot(mcb[:, hs], wq_ref[h]))
        ks.append(_dot(mcb[:, hs], wk_ref[h]))
        vs.append(_dot(mxb[:, hs], wv_ref[h]))

    gates = bif_ref[...]
    for part, vals in enumerate((qs, ks, vs)):
        for h in range(ML_HEADS):
            r0 = part * ML_WIDTH + h * ML_HEAD_DIM
            gates = gates + _dot(vals[h].astype(BF16), wif_ref[r0:r0 + ML_HEAD_DIM, :])
    i_pre = gates[:, :LANES] * LOG2E
    log_f = -_softplus(-gates[:, LANES:])

    causal = _tril_ones(CHUNK)
    bcum = _dot_exact_lhs01(causal.astype(BF16), log_f) * LOG2E
    bcum_t = bcum.T
    i_t = i_pre.T
    g_tot = bcum[CHUNK - 1:CHUNK, :]
    w_end = g_tot - bcum + i_pre

    for h in range(ML_HEADS):
        hs = slice(h * ML_HEAD_DIM, (h + 1) * ML_HEAD_DIM)
        q = qs[h]
        qb = q.astype(BF16)
        kf = ks[h] * ML_KSCALE
        kb = kf.astype(BF16)
        v = vs[h]
        c_in = c_ref[h]
        n_in = n_ref[h][0:1, :]
        m_in = m_ref[h][0:1, 0:1]

        bc = bcum[:, h:h + 1]
        dmat = jnp.where(causal, bc - bcum_t[h:h + 1, :] + i_t[h:h + 1, :], -jnp.inf)
        inter_log = bc + m_in
        m_t = jnp.maximum(jnp.max(dmat, axis=1, keepdims=True), inter_log)
        p_intra = jnp.exp2(dmat - m_t)
        s_inter = jnp.exp2(inter_log - m_t)
        qk = _dot_nt(qb, kb) * p_intra
        num = _dot(qk.astype(BF16), v.astype(BF16)) + s_inter * _dot(qb, c_in.astype(BF16))
        den = (jnp.sum(qk, axis=1, keepdims=True)
               + s_inter * jnp.sum(q * n_in, axis=1, keepdims=True))
        out = num / jnp.maximum(jnp.abs(den), jnp.exp2(-m_t))

        mu = jnp.mean(out, axis=1, keepdims=True)
        cen = out - mu
        var = jnp.mean(cen * cen, axis=1, keepdims=True)
        hn = cen * lax.rsqrt(var + EPS) * nw_ref[:, hs]
        y_ref[:, OUT_ML + hs.start:OUT_ML + hs.stop] = (_sigmoid(o_pre_ref[:, hs]) * hn).astype(BF16)

        we = w_end[:, h:h + 1]
        m_loc = jnp.max(we, axis=0, keepdims=True)
        p_end = jnp.exp2(we - m_loc)
        c_loc = _dot_tn(kb, (v * p_end).astype(BF16))
        n_loc = jnp.sum(kf * p_end, axis=0, keepdims=True)
        g_h = g_tot[:, h:h + 1]
        m_new = jnp.maximum(g_h + m_in, m_loc)
        s_old = jnp.exp2(g_h + m_in - m_new)
        s_loc = jnp.exp2(m_loc - m_new)
        c_ref[h] = s_old * c_in + s_loc * c_loc
        n_ref[h] = _splat(s_old * n_in + s_loc * n_loc, (TAIL, ML_HEAD_DIM))
        m_ref[h] = _splat(m_new, (TAIL, LANES))


def _lru_gates(xc, wa_ref, ba_ref, wx_ref, bx_ref, lam_ref, a_ref, u_ref):
    xcb = xc.astype(BF16)
    ra, ix = [], []
    for q in range(LRU_WIDTH // MXU_DIM):
        blk = xcb[:, q * MXU_DIM:(q + 1) * MXU_DIM]
        ra.append(_dot(blk, wa_ref[q]))
        ix.append(_dot(blk, wx_ref[q]))
    r = _sigmoid(jnp.concatenate(ra, axis=1) + ba_ref[...])
    i = _sigmoid(jnp.concatenate(ix, axis=1) + bx_ref[...])
    log_a = (-LRU_C * r) * _softplus(-lam_ref[...])
    a_ref[...] = jnp.exp(log_a)
    u_ref[...] = jnp.sqrt(1.0 - jnp.exp(2.0 * log_a)) * (i * xc)


def _lru_scan(a_ref, u_ref, h_ref):
    def step(t, h):
        h = a_ref[pl.ds(t, 1), :] * h + u_ref[pl.ds(t, 1), :]
        u_ref[pl.ds(t, 1), :] = h
        return h

    h_ref[0:1, :] = lax.fori_loop(0, a_ref.shape[0], step, h_ref[0:1, :], unroll=8)


def _mixer_kernel(proj_ref, cw_ref, cb_ref,
                  dtb_ref, alog_ref, dsk_ref, snw_ref, e_ref,
                  wa_ref, ba_ref, wx_ref, bx_ref, lam_ref,
                  wq_ref, wk_ref, wv_ref, wif_ref, bif_ref, mnw_ref,
                  y_ref,
                  buf_ref, state_ref, a_ref, u_ref, h_ref, c_ref, n_ref, m_ref):
    @pl.when(pl.program_id(1) == 0)
    def _():
        buf_ref[0:TAIL, :] = jnp.zeros((TAIL, CONV_WIDTH), F32)
        for ref in (state_ref, h_ref, c_ref, n_ref, m_ref):
            ref[...] = jnp.zeros(ref.shape, F32)

    conv_in = proj_ref[:, :CONV_WIDTH]
    conv = _causal_conv(conv_in, buf_ref, cw_ref, cb_ref)
    _ssd_body(_silu(conv[:, COL_XBC:COL_XBC + SSD_XBC]), proj_ref[:, COL_Z:COL_Z + SSD_WIDTH],
              proj_ref[:, COL_DT:COL_DT + LANES], dtb_ref, alog_ref, dsk_ref, snw_ref, e_ref,
              state_ref, y_ref)
    _mlstm_body(conv_in[:, COL_ML_X:COL_ML_X + ML_WIDTH], _silu(conv[:, COL_ML_X:COL_ML_X + ML_WIDTH]),
                proj_ref.at[:, COL_ML_O:COL_ML_O + ML_WIDTH], wq_ref, wk_ref, wv_ref, wif_ref, bif_ref,
                mnw_ref, c_ref, n_ref, m_ref, y_ref)
    _lru_gates(conv[:, COL_LRU_X:COL_LRU_X + LRU_WIDTH], wa_ref, ba_ref, wx_ref, bx_ref, lam_ref,
               a_ref, u_ref)
    _lru_scan(a_ref, u_ref, h_ref)
    gate = proj_ref[:, COL_LRU_GATE:COL_LRU_GATE + LRU_WIDTH]
    y_ref[:, OUT_LRU:OUT_LRU + LRU_WIDTH] = (
        u_ref[...] * jax.nn.gelu(gate, approximate=True)).astype(BF16)


def _mixers(proj, p):
    b, s, _ = proj.shape
    pad = LANES - SSD_HEADS
    head = jnp.arange(LANES)[:, None]
    expand = (head == (jnp.arange(SSD_WIDTH)[None, :] // SSD_HEAD_DIM)).astype(BF16)
    w_if, b_if = _pad_gate_weights(p["ml_w_if"], p["ml_b_if"])
    operands = [
        jnp.concatenate([p["ssd_conv_w"], p["lru_conv_w"], p["ml_conv_w"]], axis=1),
        jnp.concatenate([p["ssd_conv_b"], p["lru_conv_b"], p["ml_conv_b"]])[None, :],
        jnp.pad(p["ssd_dt_bias"], (0, pad))[None, :],
        jnp.pad(p["ssd_a_log"], (0, pad))[None, :],
        jnp.repeat(p["ssd_d"], SSD_HEAD_DIM)[None, :],
        p["ssd_norm_w"][None, :],
        expand,
        _block_diag_tiles(p["lru_w_a"], MXU_DIM), p["lru_b_a"][None, :],
        _block_diag_tiles(p["lru_w_x"], MXU_DIM), p["lru_b_x"][None, :],
        p["lru_lambda"][None, :],
        _block_diag_tiles(p["ml_w_q"], ML_HEAD_DIM),
        _block_diag_tiles(p["ml_w_k"], ML_HEAD_DIM),
        _block_diag_tiles(p["ml_w_v"], ML_HEAD_DIM),
        w_if.astype(BF16), b_if, p["ml_norm_w"][None, :],
    ]

    def const(a):
        return pl.BlockSpec(a.shape, lambda i, c, nd=a.ndim: (0,) * nd)

    return pl.pallas_call(
        _mixer_kernel,
        grid=(b, s // CHUNK),
        in_specs=[pl.BlockSpec((None, CHUNK, D_PROJ), lambda i, c: (i, c, 0))]
                 + [const(a) for a in operands],
        out_specs=pl.BlockSpec((None, CHUNK, D_MIX), lambda i, c: (i, c, 0)),
        out_shape=jax.ShapeDtypeStruct((b, s, D_MIX), BF16),
        scratch_shapes=[pltpu.VMEM((TAIL + CHUNK, CONV_WIDTH), F32),
                        pltpu.VMEM((SSD_GROUPS, SSD_STATE, SSD_GROUP_WIDTH), F32),
                        pltpu.VMEM((CHUNK, LRU_WIDTH), F32),
                        pltpu.VMEM((CHUNK, LRU_WIDTH), F32),
                        pltpu.VMEM((TAIL, LRU_WIDTH), F32),
                        pltpu.VMEM((ML_HEADS, ML_HEAD_DIM, ML_HEAD_DIM), F32),
                        pltpu.VMEM((ML_HEADS, TAIL, ML_HEAD_DIM), F32),
                        pltpu.VMEM((ML_HEADS, TAIL, LANES), F32)],
        compiler_params=_params("parallel", "arbitrary"),
        name="mixers",
    )(proj, *operands)


def _out_proj_kernel(h_ref, y_ref, w_ref, o_ref):
    o_ref[...] = h_ref[...] + _dot(y_ref[...], w_ref[...])


def _out_proj(h, y, w_out, tm, tn):
    m, n = h.shape
    k = y.shape[1]
    return pl.pallas_call(
        _out_proj_kernel,
        grid=(n // tn, m // tm),
        in_specs=[pl.BlockSpec((tm, tn), lambda j, i: (i, j)),
                  pl.BlockSpec((tm, k), lambda j, i: (i, 0)),
                  pl.BlockSpec((k, tn), lambda j, i: (0, j))],
        out_specs=pl.BlockSpec((tm, tn), lambda j, i: (i, j)),
        out_shape=jax.ShapeDtypeStruct((m, n), F32),
        compiler_params=_params("parallel", "parallel"),
        name="out_proj",
    )(h, y, w_out)


def _ffn_kernel(h_ref, nw_ref, wg_ref, wu_ref, wd_ref, fw_ref, o_ref, u_ref, acc_ref, *, final_norm):
    j = pl.program_id(1)

    @pl.when(j == 0)
    def _():
        u_ref[...] = _rmsnorm(h_ref[...], nw_ref[...]).astype(BF16)
        acc_ref[...] = jnp.zeros(acc_ref.shape, F32)

    u = u_ref[...]
    gate = _dot(u, wg_ref[...])
    up = _dot(u, wu_ref[...])
    acc_ref[...] += _dot((_silu(gate) * up).astype(BF16), wd_ref[...])

    @pl.when(j == pl.num_programs(1) - 1)
    def _():
        out = h_ref[...] + acc_ref[...]
        if final_norm:
            out = _rmsnorm(out, fw_ref[...])
        o_ref[...] = out


def _ffn(h, nw, w_gate_up, w_down, fw, tm, tf, final_norm):
    m, d = h.shape
    f = w_down.shape[0]
    nf = f // tf
    return pl.pallas_call(
        functools.partial(_ffn_kernel, final_norm=final_norm),
        grid=(m // tm, nf),
        in_specs=[pl.BlockSpec((tm, d), lambda i, j: (i, 0)),
                  pl.BlockSpec((1, d), lambda i, j: (0, 0)),
                  pl.BlockSpec((d, tf), lambda i, j: (0, j)),
                  pl.BlockSpec((d, tf), lambda i, j: (0, j + nf)),
                  pl.BlockSpec((tf, d), lambda i, j: (j, 0)),
                  pl.BlockSpec((1, d), lambda i, j: (0, 0))],
        out_specs=pl.BlockSpec((tm, d), lambda i, j: (i, 0)),
        out_shape=jax.ShapeDtypeStruct((m, d), F32),
        scratch_shapes=[pltpu.VMEM((tm, d), BF16), pltpu.VMEM((tm, d), F32)],
        compiler_params=_params("parallel", "arbitrary"),
        name="ffn_final" if final_norm else "ffn",
    )(h, nw, w_gate_up, w_gate_up, w_down, fw)


def _block_diag_tiles(w, tile):
    nblk, c, _ = w.shape
    col = jnp.arange(tile)
    spread = (jnp.arange(c)[:, None] == (col % c)[None, :]).astype(w.dtype)
    same_block = ((col // c)[:, None] == (col // c)[None, :]).astype(w.dtype)
    rep = jnp.dot(w.reshape(nblk * c, c), spread, precision=lax.Precision.HIGHEST)
    return (rep.reshape(nblk * c // tile, tile, tile) * same_block).astype(BF16)


def _reorder_w_in(w):
    o_xbc = SSD_WIDTH
    o_dt = o_xbc + SSD_XBC
    o_lru_gate = o_dt + SSD_HEADS
    o_lru_x = o_lru_gate + LRU_WIDTH
    o_ml_x = o_lru_x + LRU_WIDTH
    o_ml_o = o_ml_x + ML_WIDTH
    w = w.astype(BF16)
    pad = jnp.zeros((w.shape[0], LANES - SSD_HEADS), BF16)
    return jnp.concatenate([w[:, o_xbc:o_dt], w[:, o_lru_x:o_ml_o], w[:, :o_xbc], w[:, o_lru_gate:o_lru_x],
                            w[:, o_ml_o:], w[:, o_dt:o_lru_gate], pad], axis=1)


def _pad_gate_weights(w_if, b_if):
    zw = jnp.zeros((w_if.shape[0], LANES - ML_HEADS), w_if.dtype)
    w = jnp.concatenate([w_if[:, :ML_HEADS], zw, w_if[:, ML_HEADS:], zw], axis=1)
    zb = jnp.zeros((LANES - ML_HEADS,), b_if.dtype)
    b = jnp.concatenate([b_if[:ML_HEADS], zb, b_if[ML_HEADS:], zb])[None, :]
    return w, b


def _tile(total, want):
    return want if total % want == 0 else total


def kernel(x, norm1_w, w_in, ssd_conv_w, ssd_conv_b, ssd_dt_bias, ssd_a_log, ssd_d, ssd_norm_w, lru_conv_w, lru_conv_b, lru_w_a, lru_b_a, lru_w_x, lru_b_x, lru_lambda, ml_conv_w, ml_conv_b, ml_w_q, ml_w_k, ml_w_v, ml_w_if, ml_b_if, ml_norm_w, w_out, norm2_w, w_gate_up, w_down, norm_f_w):
    b, s, d = x.shape
    m = b * s
    depth = w_in.shape[0]
    mixer_params = dict(
        ssd_conv_w=ssd_conv_w, ssd_conv_b=ssd_conv_b, ssd_dt_bias=ssd_dt_bias, ssd_a_log=ssd_a_log,
        ssd_d=ssd_d, ssd_norm_w=ssd_norm_w, lru_conv_w=lru_conv_w, lru_conv_b=lru_conv_b,
        lru_w_a=lru_w_a, lru_b_a=lru_b_a, lru_w_x=lru_w_x, lru_b_x=lru_b_x, lru_lambda=lru_lambda,
        ml_conv_w=ml_conv_w, ml_conv_b=ml_conv_b, ml_w_q=ml_w_q, ml_w_k=ml_w_k, ml_w_v=ml_w_v,
        ml_w_if=ml_w_if, ml_b_if=ml_b_if, ml_norm_w=ml_norm_w)
    tm = _tile(m, 512)
    h = x.reshape(m, d)
    for l in range(depth):
        proj = _norm_matmul(h, norm1_w[l][None, :], _reorder_w_in(w_in[l]), _tile(m, 1024), D_PROJ // 9)
        y = _mixers(proj.reshape(b, s, D_PROJ), {k: v[l] for k, v in mixer_params.items()})
        h = _out_proj(h, y.reshape(m, D_MIX), w_out[l].astype(BF16), tm, 1024)
        h = _ffn(h, norm2_w[l][None, :], w_gate_up[l].astype(BF16), w_down[l].astype(BF16),
                 norm_f_w[None, :], tm, 512, final_norm=(l == depth - 1))
    return h.reshape(b, s, d)
```

```python
import functools

import jax
import jax.numpy as jnp
from jax import lax
from jax.experimental import pallas as pl
from jax.experimental.pallas import tpu as pltpu

F32 = jnp.float32
BF16 = jnp.bfloat16

EPS = 1e-6
LOG2E = 1.4426950408889634
CONV_K = 4
CHUNK = 128
SUBLANES = 8
LANES = 128
MXU_DIM = 256
TILES = CHUNK // SUBLANES
HIST = (CONV_K - 1) * SUBLANES

SSD_WIDTH = 2048
SSD_HEAD_DIM = 64
SSD_HEADS = 32
SSD_GROUPS = 8
SSD_STATE = 128
SSD_BC = SSD_GROUPS * SSD_STATE
SSD_XBC = SSD_WIDTH + 2 * SSD_BC
SSD_GROUP_WIDTH = SSD_WIDTH // SSD_GROUPS
SSD_HEADS_PER_GROUP = SSD_HEADS // SSD_GROUPS
LRU_WIDTH = 1024
LRU_C = 8.0
ML_WIDTH = 1024
ML_HEADS = 4
ML_HEAD_DIM = 256
ML_KSCALE = ML_HEAD_DIM ** -0.5
D_MIX = SSD_WIDTH + LRU_WIDTH + ML_WIDTH

COL_XBC = 0
COL_LRU_X = COL_XBC + SSD_XBC
COL_ML_X = COL_LRU_X + LRU_WIDTH
CONV_WIDTH = COL_ML_X + ML_WIDTH
COL_Z = CONV_WIDTH
COL_LRU_GATE = COL_Z + SSD_WIDTH
COL_ML_O = COL_LRU_GATE + LRU_WIDTH
D_PROJ = COL_ML_O + ML_WIDTH
SRC_Z = 0
SRC_XBC = SRC_Z + SSD_WIDTH
SRC_DT = SRC_XBC + SSD_XBC
SRC_LRU_GATE = SRC_DT + SSD_HEADS
SRC_LRU_X = SRC_LRU_GATE + LRU_WIDTH
SRC_ML_X = SRC_LRU_X + LRU_WIDTH
SRC_ML_O = SRC_ML_X + ML_WIDTH
D_IN = SRC_ML_O + ML_WIDTH
OUT_SSD = 0
OUT_LRU = OUT_SSD + SSD_WIDTH
OUT_ML = OUT_LRU + LRU_WIDTH

VMEM_LIMIT_BYTES = 56 * 1024 * 1024


def _params(*semantics):
    return pltpu.CompilerParams(dimension_semantics=semantics, vmem_limit_bytes=VMEM_LIMIT_BYTES)


def _dot(a, b):
    return jnp.dot(a, b, preferred_element_type=F32)


def _dot_nt(a, b):
    return lax.dot_general(a, b, (((1,), (1,)), ((), ())), preferred_element_type=F32)


def _dot_tn(a, b):
    return lax.dot_general(a, b, (((0,), (0,)), ((), ())), preferred_element_type=F32)


def _split3(x):
    hi = x.astype(BF16)
    r1 = x - hi.astype(F32)
    mid = r1.astype(BF16)
    lo = (r1 - mid.astype(F32)).astype(BF16)
    return hi, mid, lo


def _dot_exact_lhs01(m01, x):
    hi, mid, lo = _split3(x)
    return (_dot(m01, hi) + _dot(m01, mid)) + _dot(m01, lo)


def _dot_exact_rhs01(x, m01):
    hi, mid, lo = _split3(x)
    return (_dot(hi, m01) + _dot(mid, m01)) + _dot(lo, m01)


def _sigmoid(x):
    return 1.0 / (1.0 + jnp.exp2(x * (-LOG2E)))


def _silu(x):
    return x * _sigmoid(x)


def _softplus(x):
    return jnp.maximum(x, 0.0) + jnp.log1p(jnp.exp(-jnp.abs(x)))


def _rmsnorm(x, w):
    return x * lax.rsqrt(jnp.mean(x * x, axis=-1, keepdims=True) + EPS) * w


def _store_chunk_unpermuted(val, stage_ref, dst_ref):
    rows, width = val.shape
    for i in range(width // LANES):
        stage_ref[i] = val[:, i * LANES:(i + 1) * LANES]
    for i in range(width // LANES):
        for c in range(rows // CHUNK):
            for j in range(SUBLANES):
                dst_ref[c * CHUNK + j * TILES:c * CHUNK + (j + 1) * TILES, i * LANES:(i + 1) * LANES] = (
                    stage_ref[i, pl.ds(c * CHUNK + j, TILES, stride=SUBLANES), :])


def _stage_scratch(rows, width):
    return pltpu.VMEM((width // LANES, rows, LANES), F32)


def _causal_mask():
    def pos(dim):
        r = lax.broadcasted_iota(jnp.int32, (CHUNK, CHUNK), dim)
        return (r % SUBLANES) * TILES + r // SUBLANES
    return pos(0) >= pos(1)


def _prep_w_in_kernel(w_ref, wm_ref, wdt_ref):
    def put(dst, src, width):
        wm_ref[:, dst:dst + width] = w_ref[:, src:src + width].astype(BF16)

    put(COL_XBC, SRC_XBC, SSD_XBC)
    put(COL_Z, SRC_Z, SSD_WIDTH)
    rest = w_ref[:, SRC_DT:D_IN][:, SSD_HEADS:].astype(BF16)
    for dst, src in ((COL_LRU_GATE, SRC_LRU_GATE), (COL_LRU_X, SRC_LRU_X), (COL_ML_X, SRC_ML_X),
                     (COL_ML_O, SRC_ML_O)):
        off = src - SRC_LRU_GATE
        wm_ref[:, dst:dst + LRU_WIDTH] = rest[:, off:off + LRU_WIDTH]
    lane = lax.broadcasted_iota(jnp.int32, wdt_ref.shape, 1)
    wdt_ref[...] = jnp.where(lane < SSD_HEADS, w_ref[:, SRC_DT:SRC_DT + LANES], 0.0).astype(BF16)


def _prep_w_in(w_in, rows):
    depth, d, _ = w_in.shape
    return pl.pallas_call(
        _prep_w_in_kernel,
        grid=(depth, d // rows),
        in_specs=[pl.BlockSpec((None, rows, D_IN), lambda l, i: (l, i, 0))],
        out_specs=[pl.BlockSpec((None, rows, D_PROJ), lambda l, i: (l, i, 0)),
                   pl.BlockSpec((None, rows, LANES), lambda l, i: (l, i, 0))],
        out_shape=[jax.ShapeDtypeStruct((depth, d, D_PROJ), BF16),
                   jax.ShapeDtypeStruct((depth, d, LANES), BF16)],
        compiler_params=_params("parallel", "parallel"),
        name="prep_w_in",
    )(w_in)


def _norm_matmul_kernel(x_ref, nw_ref, w_ref, wdt_ref, o_ref, dt_ref, u_ref):
    @pl.when(pl.program_id(1) == 0)
    def _():
        u = _rmsnorm(x_ref[...], nw_ref[...]).astype(BF16)
        u_ref[...] = u
        dt_ref[...] = _dot(u, wdt_ref[...])

    o_ref[...] = _dot(u_ref[...], w_ref[...])


def _norm_matmul(x, nw, w, wdt, tm, tn):
    m, k = x.shape
    n = w.shape[1]
    return pl.pallas_call(
        _norm_matmul_kernel,
        grid=(m // tm, n // tn),
        in_specs=[pl.BlockSpec((tm, k), lambda i, j: (i, 0)),
                  pl.BlockSpec((1, k), lambda i, j: (0, 0)),
                  pl.BlockSpec((k, tn), lambda i, j: (0, j)),
                  pl.BlockSpec((k, LANES), lambda i, j: (0, 0))],
        out_specs=[pl.BlockSpec((tm, tn), lambda i, j: (i, j)),
                   pl.BlockSpec((tm, LANES), lambda i, j: (i, 0))],
        out_shape=[jax.ShapeDtypeStruct((m, n), F32), jax.ShapeDtypeStruct((m, LANES), F32)],
        scratch_shapes=[pltpu.VMEM((tm, k), BF16)],
        compiler_params=_params("parallel", "arbitrary"),
        name="norm_in_proj",
    )(x, nw, w, wdt)


def _causal_conv(cur, hist_ref, w_ref, b_ref):
    width = cur.shape[1]
    sub = lax.broadcasted_iota(jnp.int32, (SUBLANES, width), 0)
    wrapped = []
    for i in range(CONV_K - 1):
        lo = CHUNK - HIST + i * SUBLANES
        from_cur = pltpu.roll(cur[lo:lo + SUBLANES, :], 1, axis=0)
        from_prev = pltpu.roll(hist_ref[i * SUBLANES:(i + 1) * SUBLANES, :], 1, axis=0)
        wrapped.append(jnp.where(sub == 0, from_prev, from_cur))
    acc = cur * w_ref[CONV_K - 1:CONV_K, :] + b_ref[...]
    for s in range(1, CONV_K):
        shifted = jnp.concatenate(wrapped[CONV_K - 1 - s:] + [cur[0:CHUNK - s * SUBLANES, :]], axis=0)
        acc = acc + shifted * w_ref[CONV_K - 1 - s:CONV_K - s, :]
    hist_ref[...] = cur[CHUNK - HIST:, :]
    return acc


def _ssd_body(xbc, z, dt_raw, causal, dtb_ref, alog_ref, dsk_ref, nw_ref, e_ref, state_ref, y_ref):
    xs = xbc[:, :SSD_WIDTH]
    bm = xbc[:, SSD_WIDTH:SSD_WIDTH + SSD_BC].astype(BF16)
    cm = xbc[:, SSD_WIDTH + SSD_BC:].astype(BF16)

    dt = _softplus(dt_raw + dtb_ref[...])
    da = dt * (-jnp.exp(alog_ref[...]))
    cs = _dot_exact_lhs01(causal.astype(BF16), da) * LOG2E
    cs_t = cs.T

    expand = e_ref[...]
    dt_e = _dot_exact_rhs01(dt, expand)
    cs_e = _dot_exact_rhs01(cs, expand)
    cs_last_e = cs_e[CHUNK - 1:CHUNK, :]
    xd = xs * dt_e
    xd_to_end = (xd * jnp.exp2(cs_last_e - cs_e)).astype(BF16)
    decay_from_start = jnp.exp2(cs_e)
    chunk_decay = jnp.exp2(cs_last_e)

    head_of_lane = lax.broadcasted_iota(jnp.int32, (CHUNK, SSD_GROUP_WIDTH), 1) // SSD_HEAD_DIM
    ys = []
    for g in range(SSD_GROUPS):
        gs = slice(g * SSD_GROUP_WIDTH, (g + 1) * SSD_GROUP_WIDTH)
        b_g = bm[:, g * SSD_STATE:(g + 1) * SSD_STATE]
        c_g = cm[:, g * SSD_STATE:(g + 1) * SSD_STATE]
        scores = _dot_nt(c_g, b_g)
        xd_g = xd[:, gs]
        lhs, rhs = [], []
        for j in range(SSD_HEADS_PER_GROUP):
            h = g * SSD_HEADS_PER_GROUP + j
            seg = cs[:, h:h + 1] - cs_t[h:h + 1, :]
            decay = jnp.exp2(jnp.where(causal, seg, -jnp.inf))
            lhs.append((scores * decay).astype(BF16))
            rhs.append(jnp.where(head_of_lane == j, xd_g, 0.0).astype(BF16))
        y_diag = _dot(jnp.concatenate(lhs, axis=1), jnp.concatenate(rhs, axis=0))
        state = state_ref[g]
        y_off = _dot(c_g, state.astype(BF16)) * decay_from_start[:, gs]
        state_ref[g] = state * chunk_decay[:, gs] + _dot_tn(b_g, xd_to_end[:, gs])
        ys.append(y_diag + y_off)

    y = jnp.concatenate(ys, axis=1) + dsk_ref[...] * xs
    y = y * _silu(z)
    for g in range(SSD_GROUPS):
        gs = slice(g * SSD_GROUP_WIDTH, (g + 1) * SSD_GROUP_WIDTH)
        yg = y[:, gs]
        yn = yg * lax.rsqrt(jnp.mean(yg * yg, axis=-1, keepdims=True) + EPS)
        y_ref[:, OUT_SSD + gs.start:OUT_SSD + gs.stop] = (yn * nw_ref[:, gs]).astype(BF16)


def _splat(x11, shape):
    return jnp.broadcast_to(x11, shape)


def _mlstm_body(mx, mc, o_pre_ref, causal, wq_ref, wk_ref, wv_ref, wif_ref, bif_ref, nw_ref,
                c_ref, n_ref, m_ref, y_ref):
    mcb = mc.astype(BF16)
    mxb = mx.astype(BF16)
    qs, ks, vs = [], [], []
    for h in range(ML_HEADS):
        hs = slice(h * ML_HEAD_DIM, (h + 1) * ML_HEAD_DIM)
        qs.append(_dot(mcb[:, hs], wq_ref[h]))
        ks.append(_dot(mcb[:, hs], wk_ref[h]))
        vs.append(_dot(mxb[:, hs], wv_ref[h]))

    gates = bif_ref[...]
    for part, vals in enumerate((qs, ks, vs)):
        for h in range(ML_HEADS):
            r0 = part * ML_WIDTH + h * ML_HEAD_DIM
            gates = gates + _dot(vals[h].astype(BF16), wif_ref[r0:r0 + ML_HEAD_DIM, :])
    i_pre = gates[:, :LANES] * LOG2E
    log_f = -_softplus(-gates[:, LANES:])

    bcum = _dot_exact_lhs01(causal.astype(BF16), log_f) * LOG2E
    bcum_t = bcum.T
    i_t = i_pre.T
    g_tot = bcum[CHUNK - 1:CHUNK, :]
    w_end = g_tot - bcum + i_pre

    for h in range(ML_HEADS):
        hs = slice(h * ML_HEAD_DIM, (h + 1) * ML_HEAD_DIM)
        q = qs[h]
        qb = q.astype(BF16)
        kf = ks[h] * ML_KSCALE
        kb = kf.astype(BF16)
        v = vs[h]
        c_in = c_ref[h]
        n_in = n_ref[h][0:1, :]
        m_in = m_ref[h][0:1, 0:1]

        bc = bcum[:, h:h + 1]
        dmat = jnp.where(causal, bc - bcum_t[h:h + 1, :] + i_t[h:h + 1, :], -jnp.inf)
        inter_log = bc + m_in
        m_t = jnp.maximum(jnp.max(dmat, axis=1, keepdims=True), inter_log)
        p_intra = jnp.exp2(dmat - m_t)
        s_inter = jnp.exp2(inter_log - m_t)
        qk = _dot_nt(qb, kb) * p_intra
        num = _dot(qk.astype(BF16), v.astype(BF16)) + s_inter * _dot(qb, c_in.astype(BF16))
        den = (jnp.sum(qk, axis=1, keepdims=True)
               + s_inter * jnp.sum(q * n_in, axis=1, keepdims=True))
        out = num / jnp.maximum(jnp.abs(den), jnp.exp2(-m_t))

        mu = jnp.mean(out, axis=1, keepdims=True)
        cen = out - mu
        var = jnp.mean(cen * cen, axis=1, keepdims=True)
        hn = cen * lax.rsqrt(var + EPS) * nw_ref[:, hs]
        y_ref[:, OUT_ML + hs.start:OUT_ML + hs.stop] = (_sigmoid(o_pre_ref[:, hs]) * hn).astype(BF16)

        we = w_end[:, h:h + 1]
        m_loc = jnp.max(we, axis=0, keepdims=True)
        p_end = jnp.exp2(we - m_loc)
        c_loc = _dot_tn(kb, (v * p_end).astype(BF16))
        n_loc = jnp.sum(kf * p_end, axis=0, keepdims=True)
        g_h = g_tot[:, h:h + 1]
        m_new = jnp.maximum(g_h + m_in, m_loc)
        s_old = jnp.exp2(g_h + m_in - m_new)
        s_loc = jnp.exp2(m_loc - m_new)
        c_ref[h] = s_old * c_in + s_loc * c_loc
        n_ref[h] = _splat(s_old * n_in + s_loc * n_loc, (SUBLANES, ML_HEAD_DIM))
        m_ref[h] = _splat(m_new, (SUBLANES, LANES))


def _lru_gates(xc, wa_ref, ba_ref, wx_ref, bx_ref, lam_ref):
    xcb = xc.astype(BF16)
    ra, ix = [], []
    for q in range(LRU_WIDTH // MXU_DIM):
        blk = xcb[:, q * MXU_DIM:(q + 1) * MXU_DIM]
        ra.append(_dot(blk, wa_ref[q]))
        ix.append(_dot(blk, wx_ref[q]))
    r = _sigmoid(jnp.concatenate(ra, axis=1) + ba_ref[...])
    i = _sigmoid(jnp.concatenate(ix, axis=1) + bx_ref[...])
    log_a = (-LRU_C * r) * _softplus(-lam_ref[...])
    a = jnp.exp(log_a)
    u = jnp.sqrt(1.0 - jnp.exp(2.0 * log_a)) * (i * xc)
    return a, u


def _lru_scan(a, u, h_ref):
    def tile(x, k):
        return x[k * SUBLANES:(k + 1) * SUBLANES, :]

    decay, local = [tile(a, 0)], [tile(u, 0)]
    for k in range(1, TILES):
        a_k = tile(a, k)
        local.append(a_k * local[-1] + tile(u, k))
        decay.append(a_k * decay[-1])
    p, q = decay[-1], local[-1]
    sub = lax.broadcasted_iota(jnp.int32, p.shape, 0)
    d = 1
    while d < SUBLANES:
        p_prev = jnp.where(sub >= d, pltpu.roll(p, d, axis=0), 1.0)
        q_prev = jnp.where(sub >= d, pltpu.roll(q, d, axis=0), 0.0)
        q = p * q_prev + q
        p = p * p_prev
        d *= 2
    h_prev = jnp.broadcast_to(h_ref[0:1, :], p.shape)
    run_end = q + p * h_prev
    run_in = jnp.where(sub == 0, h_prev, pltpu.roll(run_end, 1, axis=0))
    h_ref[0:1, :] = run_end[SUBLANES - 1:SUBLANES, :]
    return jnp.concatenate([local[k] + decay[k] * run_in for k in range(TILES)], axis=0)


def _mixer_kernel(proj_ref, dt_ref, cw_ref, cb_ref,
                  dtb_ref, alog_ref, dsk_ref, snw_ref, e_ref,
                  wa_ref, ba_ref, wx_ref, bx_ref, lam_ref,
                  wq_ref, wk_ref, wv_ref, wif_ref, bif_ref, mnw_ref,
                  y_ref,
                  hist_ref, state_ref, h_ref, c_ref, n_ref, m_ref):
    @pl.when(pl.program_id(1) == 0)
    def _():
        for ref in (hist_ref, state_ref, h_ref, c_ref, n_ref, m_ref):
            ref[...] = jnp.zeros(ref.shape, F32)

    causal = _causal_mask()
    conv_in = proj_ref[:, :CONV_WIDTH]
    conv = _causal_conv(conv_in, hist_ref, cw_ref, cb_ref)
    _ssd_body(_silu(conv[:, COL_XBC:COL_XBC + SSD_XBC]), proj_ref[:, COL_Z:COL_Z + SSD_WIDTH],
              dt_ref[...], causal, dtb_ref, alog_ref, dsk_ref, snw_ref, e_ref, state_ref, y_ref)
    _mlstm_body(conv_in[:, COL_ML_X:COL_ML_X + ML_WIDTH], _silu(conv[:, COL_ML_X:COL_ML_X + ML_WIDTH]),
                proj_ref.at[:, COL_ML_O:COL_ML_O + ML_WIDTH], causal, wq_ref, wk_ref, wv_ref, wif_ref,
                bif_ref, mnw_ref, c_ref, n_ref, m_ref, y_ref)
    a, u = _lru_gates(conv[:, COL_LRU_X:COL_LRU_X + LRU_WIDTH], wa_ref, ba_ref, wx_ref, bx_ref, lam_ref)
    hs = _lru_scan(a, u, h_ref)
    gate = proj_ref[:, COL_LRU_GATE:COL_LRU_GATE + LRU_WIDTH]
    y_ref[:, OUT_LRU:OUT_LRU + LRU_WIDTH] = (hs * jax.nn.gelu(gate, approximate=True)).astype(BF16)


def _mixers(proj, dt_raw, p):
    b, s, _ = proj.shape
    pad = LANES - SSD_HEADS
    head = jnp.arange(LANES)[:, None]
    expand = (head == (jnp.arange(SSD_WIDTH)[None, :] // SSD_HEAD_DIM)).astype(BF16)
    w_if, b_if = _pad_gate_weights(p["ml_w_if"], p["ml_b_if"])
    operands = [
        jnp.concatenate([p["ssd_conv_w"], p["lru_conv_w"], p["ml_conv_w"]], axis=1),
        jnp.concatenate([p["ssd_conv_b"], p["lru_conv_b"], p["ml_conv_b"]])[None, :],
        jnp.pad(p["ssd_dt_bias"], (0, pad))[None, :],
        jnp.pad(p["ssd_a_log"], (0, pad))[None, :],
        jnp.repeat(p["ssd_d"], SSD_HEAD_DIM)[None, :],
        p["ssd_norm_w"][None, :],
        expand,
        _block_diag_tiles(p["lru_w_a"], MXU_DIM), p["lru_b_a"][None, :],
        _block_diag_tiles(p["lru_w_x"], MXU_DIM), p["lru_b_x"][None, :],
        p["lru_lambda"][None, :],
        _block_diag_tiles(p["ml_w_q"], ML_HEAD_DIM),
        _block_diag_tiles(p["ml_w_k"], ML_HEAD_DIM),
        _block_diag_tiles(p["ml_w_v"], ML_HEAD_DIM),
        w_if.astype(BF16), b_if, p["ml_norm_w"][None, :],
    ]

    def const(a):
        return pl.BlockSpec(a.shape, lambda i, c, nd=a.ndim: (0,) * nd)

    return pl.pallas_call(
        _mixer_kernel,
        grid=(b, s // CHUNK),
        in_specs=[pl.BlockSpec((None, CHUNK, D_PROJ), lambda i, c: (i, c, 0)),
                  pl.BlockSpec((None, CHUNK, LANES), lambda i, c: (i, c, 0))]
                 + [const(a) for a in operands],
        out_specs=pl.BlockSpec((None, CHUNK, D_MIX), lambda i, c: (i, c, 0)),
        out_shape=jax.ShapeDtypeStruct((b, s, D_MIX), BF16),
        scratch_shapes=[pltpu.VMEM((HIST, CONV_WIDTH), F32),
                        pltpu.VMEM((SSD_GROUPS, SSD_STATE, SSD_GROUP_WIDTH), F32),
                        pltpu.VMEM((SUBLANES, LRU_WIDTH), F32),
                        pltpu.VMEM((ML_HEADS, ML_HEAD_DIM, ML_HEAD_DIM), F32),
                        pltpu.VMEM((ML_HEADS, SUBLANES, ML_HEAD_DIM), F32),
                        pltpu.VMEM((ML_HEADS, SUBLANES, LANES), F32)],
        compiler_params=_params("parallel", "arbitrary"),
        name="mixers",
    )(proj, dt_raw, *operands)


def _out_proj_kernel(h_ref, y_ref, w_ref, o_ref):
    o_ref[...] = h_ref[...] + _dot(y_ref[...], w_ref[...])


def _out_proj(h, y, w_out, tm, tn):
    m, n = h.shape
    k = y.shape[1]
    return pl.pallas_call(
        _out_proj_kernel,
        grid=(n // tn, m // tm),
        in_specs=[pl.BlockSpec((tm, tn), lambda j, i: (i, j)),
                  pl.BlockSpec((tm, k), lambda j, i: (i, 0)),
                  pl.BlockSpec((k, tn), lambda j, i: (0, j))],
        out_specs=pl.BlockSpec((tm, tn), lambda j, i: (i, j)),
        out_shape=jax.ShapeDtypeStruct((m, n), F32),
        compiler_params=_params("parallel", "parallel"),
        name="out_proj",
    )(h, y, w_out)


def _ffn_kernel(h_ref, nw_ref, wg_ref, wu_ref, wd_ref, fw_ref, o_ref, u_ref, acc_ref, *stage, last_layer):
    j = pl.program_id(1)

    @pl.when(j == 0)
    def _():
        u_ref[...] = _rmsnorm(h_ref[...], nw_ref[...]).astype(BF16)
        acc_ref[...] = jnp.zeros(acc_ref.shape, F32)

    u = u_ref[...]
    gate = _dot(u, wg_ref[...])
    up = _dot(u, wu_ref[...])
    acc_ref[...] += _dot((_silu(gate) * up).astype(BF16), wd_ref[...])

    @pl.when(j == pl.num_programs(1) - 1)
    def _():
        out = h_ref[...] + acc_ref[...]
        if last_layer:
            _store_chunk_unpermuted(_rmsnorm(out, fw_ref[...]), *stage, o_ref)
        else:
            o_ref[...] = out


def _ffn(h, nw, w_gate_up, w_down, fw, tm, tf, last_layer):
    m, d = h.shape
    f = w_down.shape[0]
    nf = f // tf
    return pl.pallas_call(
        functools.partial(_ffn_kernel, last_layer=last_layer),
        grid=(m // tm, nf),
        in_specs=[pl.BlockSpec((tm, d), lambda i, j: (i, 0)),
                  pl.BlockSpec((1, d), lambda i, j: (0, 0)),
                  pl.BlockSpec((d, tf), lambda i, j: (0, j)),
                  pl.BlockSpec((d, tf), lambda i, j: (0, j + nf)),
                  pl.BlockSpec((tf, d), lambda i, j: (j, 0)),
                  pl.BlockSpec((1, d), lambda i, j: (0, 0))],
        out_specs=pl.BlockSpec((tm, d), lambda i, j: (i, 0)),
        out_shape=jax.ShapeDtypeStruct((m, d), F32),
        scratch_shapes=([pltpu.VMEM((tm, d), BF16), pltpu.VMEM((tm, d), F32)]
                        + ([_stage_scratch(tm, d)] if last_layer else [])),
        compiler_params=_params("parallel", "arbitrary"),
        name="ffn_last" if last_layer else "ffn",
    )(h, nw, w_gate_up, w_gate_up, w_down, fw)


def _block_diag_tiles(w, tile):
    nblk, c, _ = w.shape
    col = jnp.arange(tile)
    spread = (jnp.arange(c)[:, None] == (col % c)[None, :]).astype(w.dtype)
    same_block = ((col // c)[:, None] == (col // c)[None, :]).astype(w.dtype)
    rep = jnp.dot(w.reshape(nblk * c, c), spread, precision=lax.Precision.HIGHEST)
    return (rep.reshape(nblk * c // tile, tile, tile) * same_block).astype(BF16)


def _pad_gate_weights(w_if, b_if):
    zw = jnp.zeros((w_if.shape[0], LANES - ML_HEADS), w_if.dtype)
    w = jnp.concatenate([w_if[:, :ML_HEADS], zw, w_if[:, ML_HEADS:], zw], axis=1)
    zb = jnp.zeros((LANES - ML_HEADS,), b_if.dtype)
    b = jnp.concatenate([b_if[:ML_HEADS], zb, b_if[ML_HEADS:], zb])[None, :]
    return w, b


def _tile(total, want):
    return want if total % want == 0 else total


def kernel(x, norm1_w, w_in, ssd_conv_w, ssd_conv_b, ssd_dt_bias, ssd_a_log, ssd_d, ssd_norm_w, lru_conv_w, lru_conv_b, lru_w_a, lru_b_a, lru_w_x, lru_b_x, lru_lambda, ml_conv_w, ml_conv_b, ml_w_q, ml_w_k, ml_w_v, ml_w_if, ml_b_if, ml_norm_w, w_out, norm2_w, w_gate_up, w_down, norm_f_w):
    b, s, d = x.shape
    m = b * s
    depth = w_in.shape[0]
    mixer_params = dict(
        ssd_conv_w=ssd_conv_w, ssd_conv_b=ssd_conv_b, ssd_dt_bias=ssd_dt_bias, ssd_a_log=ssd_a_log,
        ssd_d=ssd_d, ssd_norm_w=ssd_norm_w, lru_conv_w=lru_conv_w, lru_conv_b=lru_conv_b,
        lru_w_a=lru_w_a, lru_b_a=lru_b_a, lru_w_x=lru_w_x, lru_b_x=lru_b_x, lru_lambda=lru_lambda,
        ml_conv_w=ml_conv_w, ml_conv_b=ml_conv_b, ml_w_q=ml_w_q, ml_w_k=ml_w_k, ml_w_v=ml_w_v,
        ml_w_if=ml_w_if, ml_b_if=ml_b_if, ml_norm_w=ml_norm_w)
    w_main, w_dt = _prep_w_in(w_in, 256)
    tm = _tile(m, 512)
    h = x.reshape(b, s // CHUNK, SUBLANES, TILES, d).swapaxes(2, 3).reshape(m, d)
    for l in range(depth):
        proj, dt_raw = _norm_matmul(h, norm1_w[l][None, :], w_main[l], w_dt[l], _tile(m, 1024),
                                    D_PROJ // 8)
        y = _mixers(proj.reshape(b, s, D_PROJ), dt_raw.reshape(b, s, LANES),
                    {k: v[l] for k, v in mixer_params.items()})
        h = _out_proj(h, y.reshape(m, D_MIX), w_out[l].astype(BF16), tm, 1024)
        h = _ffn(h, norm2_w[l][None, :], w_gate_up[l].astype(BF16), w_down[l].astype(BF16),
                 norm_f_w[None, :], tm, 512, last_layer=(l == depth - 1))
    return h.reshape(b, s, d)
```

```python
import functools

import jax
import jax.numpy as jnp
from jax import lax
from jax.experimental import pallas as pl
from jax.experimental.pallas import tpu as pltpu

F32 = jnp.float32
BF16 = jnp.bfloat16

EPS = 1e-6
LOG2E = 1.4426950408889634
CONV_K = 4
CHUNK = 128
SUBLANES = 8
LANES = 128
MXU_DIM = 256
TILES = CHUNK // SUBLANES
HIST = (CONV_K - 1) * SUBLANES

SSD_WIDTH = 2048
SSD_HEAD_DIM = 64
SSD_HEADS = 32
SSD_GROUPS = 8
SSD_STATE = 128
SSD_BC = SSD_GROUPS * SSD_STATE
SSD_XBC = SSD_WIDTH + 2 * SSD_BC
SSD_GROUP_WIDTH = SSD_WIDTH // SSD_GROUPS
SSD_HEADS_PER_GROUP = SSD_HEADS // SSD_GROUPS
LRU_WIDTH = 1024
LRU_C = 8.0
ML_WIDTH = 1024
ML_HEADS = 4
ML_HEAD_DIM = 256
ML_KSCALE = ML_HEAD_DIM ** -0.5
D_MIX = SSD_WIDTH + LRU_WIDTH + ML_WIDTH

COL_XBC = 0
COL_LRU_X = COL_XBC + SSD_XBC
COL_ML_X = COL_LRU_X + LRU_WIDTH
CONV_WIDTH = COL_ML_X + ML_WIDTH
COL_Z = CONV_WIDTH
COL_LRU_GATE = COL_Z + SSD_WIDTH
COL_ML_O = COL_LRU_GATE + LRU_WIDTH
D_PROJ = COL_ML_O + ML_WIDTH
SRC_Z = 0
SRC_XBC = SRC_Z + SSD_WIDTH
SRC_DT = SRC_XBC + SSD_XBC
SRC_LRU_GATE = SRC_DT + SSD_HEADS
SRC_LRU_X = SRC_LRU_GATE + LRU_WIDTH
SRC_ML_X = SRC_LRU_X + LRU_WIDTH
SRC_ML_O = SRC_ML_X + ML_WIDTH
D_IN = SRC_ML_O + ML_WIDTH
OUT_SSD = 0
OUT_LRU = OUT_SSD + SSD_WIDTH
OUT_ML = OUT_LRU + LRU_WIDTH

VMEM_LIMIT_BYTES = 56 * 1024 * 1024


def _params(*semantics):
    return pltpu.CompilerParams(dimension_semantics=semantics, vmem_limit_bytes=VMEM_LIMIT_BYTES)


def _dot(a, b):
    return jnp.dot(a, b, preferred_element_type=F32)


def _dot_nt(a, b):
    return lax.dot_general(a, b, (((1,), (1,)), ((), ())), preferred_element_type=F32)


def _dot_tn(a, b):
    return lax.dot_general(a, b, (((0,), (0,)), ((), ())), preferred_element_type=F32)


def _split3(x):
    hi = x.astype(BF16)
    r1 = x - hi.astype(F32)
    mid = r1.astype(BF16)
    lo = (r1 - mid.astype(F32)).astype(BF16)
    return hi, mid, lo


def _dot_exact_lhs01(m01, x):
    hi, mid, lo = _split3(x)
    return (_dot(m01, hi) + _dot(m01, mid)) + _dot(m01, lo)


def _dot_exact_rhs01(x, m01):
    hi, mid, lo = _split3(x)
    return (_dot(hi, m01) + _dot(mid, m01)) + _dot(lo, m01)


def _sigmoid(x):
    return 1.0 / (1.0 + jnp.exp2(x * (-LOG2E)))


def _silu(x):
    return x * _sigmoid(x)


def _softplus(x):
    return jnp.maximum(x, 0.0) + jnp.log1p(jnp.exp(-jnp.abs(x)))


def _rmsnorm(x, w):
    return x * lax.rsqrt(jnp.mean(x * x, axis=-1, keepdims=True) + EPS) * w


def _store_chunk_unpermuted(val, stage_ref, dst_ref):
    rows, width = val.shape
    for i in range(width // LANES):
        stage_ref[i] = val[:, i * LANES:(i + 1) * LANES]
    for i in range(width // LANES):
        for c in range(rows // CHUNK):
            for j in range(SUBLANES):
                dst_ref[c * CHUNK + j * TILES:c * CHUNK + (j + 1) * TILES, i * LANES:(i + 1) * LANES] = (
                    stage_ref[i, pl.ds(c * CHUNK + j, TILES, stride=SUBLANES), :])


def _stage_scratch(rows, width):
    return pltpu.VMEM((width // LANES, rows, LANES), F32)


def _causal_mask():
    def pos(dim):
        r = lax.broadcasted_iota(jnp.int32, (CHUNK, CHUNK), dim)
        return (r % SUBLANES) * TILES + r // SUBLANES
    return pos(0) >= pos(1)


def _permute_rows_kernel(x_ref, o_ref):
    for c in range(x_ref.shape[0] // CHUNK):
        for k in range(TILES):
            o_ref[c * CHUNK + k * SUBLANES:c * CHUNK + (k + 1) * SUBLANES, :] = (
                x_ref[pl.ds(c * CHUNK + k, SUBLANES, stride=TILES), :])


def _permute_rows(x, rows):
    m, d = x.shape
    return pl.pallas_call(
        _permute_rows_kernel,
        grid=(m // rows, d // LANES),
        in_specs=[pl.BlockSpec((rows, LANES), lambda i, j: (i, j))],
        out_specs=pl.BlockSpec((rows, LANES), lambda i, j: (i, j)),
        out_shape=jax.ShapeDtypeStruct((m, d), x.dtype),
        compiler_params=_params("parallel", "parallel"),
        name="permute_rows",
    )(x)


def _prep_w_in_kernel(w_ref, wm_ref, wdt_ref):
    def put(dst, src, width):
        wm_ref[:, dst:dst + width] = w_ref[:, src:src + width].astype(BF16)

    put(COL_XBC, SRC_XBC, SSD_XBC)
    put(COL_Z, SRC_Z, SSD_WIDTH)
    rest = w_ref[:, SRC_DT:D_IN][:, SSD_HEADS:].astype(BF16)
    for dst, src in ((COL_LRU_GATE, SRC_LRU_GATE), (COL_LRU_X, SRC_LRU_X), (COL_ML_X, SRC_ML_X),
                     (COL_ML_O, SRC_ML_O)):
        off = src - SRC_LRU_GATE
        wm_ref[:, dst:dst + LRU_WIDTH] = rest[:, off:off + LRU_WIDTH]
    lane = lax.broadcasted_iota(jnp.int32, wdt_ref.shape, 1)
    wdt_ref[...] = jnp.where(lane < SSD_HEADS, w_ref[:, SRC_DT:SRC_DT + LANES], 0.0).astype(BF16)


def _prep_w_in(w_in, rows):
    depth, d, _ = w_in.shape
    return pl.pallas_call(
        _prep_w_in_kernel,
        grid=(depth, d // rows),
        in_specs=[pl.BlockSpec((None, rows, D_IN), lambda l, i: (l, i, 0))],
        out_specs=[pl.BlockSpec((None, rows, D_PROJ), lambda l, i: (l, i, 0)),
                   pl.BlockSpec((None, rows, LANES), lambda l, i: (l, i, 0))],
        out_shape=[jax.ShapeDtypeStruct((depth, d, D_PROJ), BF16),
                   jax.ShapeDtypeStruct((depth, d, LANES), BF16)],
        compiler_params=_params("parallel", "parallel"),
        name="prep_w_in",
    )(w_in)


def _norm_matmul_kernel(x_ref, nw_ref, w_ref, wdt_ref, o_ref, dt_ref, u_ref):
    @pl.when(pl.program_id(1) == 0)
    def _():
        u = _rmsnorm(x_ref[...], nw_ref[...]).astype(BF16)
        u_ref[...] = u
        dt_ref[...] = _dot(u, wdt_ref[...])

    o_ref[...] = _dot(u_ref[...], w_ref[...])


def _norm_matmul(x, nw, w, wdt, layer, tm, tn):
    m, k = x.shape
    n = w.shape[2]
    return pl.pallas_call(
        _norm_matmul_kernel,
        grid=(m // tm, n // tn),
        in_specs=[pl.BlockSpec((tm, k), lambda i, j: (i, 0)),
                  pl.BlockSpec((1, k), lambda i, j: (0, 0)),
                  pl.BlockSpec((None, k, tn), lambda i, j: (layer, 0, j)),
                  pl.BlockSpec((None, k, LANES), lambda i, j: (layer, 0, 0))],
        out_specs=[pl.BlockSpec((tm, tn), lambda i, j: (i, j)),
                   pl.BlockSpec((tm, LANES), lambda i, j: (i, 0))],
        out_shape=[jax.ShapeDtypeStruct((m, n), F32), jax.ShapeDtypeStruct((m, LANES), F32)],
        scratch_shapes=[pltpu.VMEM((tm, k), BF16)],
        compiler_params=_params("parallel", "arbitrary"),
        name="norm_in_proj",
    )(x, nw, w, wdt)


def _causal_conv(cur, hist_ref, w_ref, b_ref):
    width = cur.shape[1]
    sub = lax.broadcasted_iota(jnp.int32, (SUBLANES, width), 0)
    wrapped = []
    for i in range(CONV_K - 1):
        lo = CHUNK - HIST + i * SUBLANES
        from_cur = pltpu.roll(cur[lo:lo + SUBLANES, :], 1, axis=0)
        from_prev = pltpu.roll(hist_ref[i * SUBLANES:(i + 1) * SUBLANES, :], 1, axis=0)
        wrapped.append(jnp.where(sub == 0, from_prev, from_cur))
    acc = cur * w_ref[CONV_K - 1:CONV_K, :] + b_ref[...]
    for s in range(1, CONV_K):
        shifted = jnp.concatenate(wrapped[CONV_K - 1 - s:] + [cur[0:CHUNK - s * SUBLANES, :]], axis=0)
        acc = acc + shifted * w_ref[CONV_K - 1 - s:CONV_K - s, :]
    hist_ref[...] = cur[CHUNK - HIST:, :]
    return acc


def _ssd_body(xbc, z, dt_raw, causal, dtb_ref, alog_ref, dsk_ref, nw_ref, e_ref, state_ref, y_ref):
    xs = xbc[:, :SSD_WIDTH]
    bm = xbc[:, SSD_WIDTH:SSD_WIDTH + SSD_BC].astype(BF16)
    cm = xbc[:, SSD_WIDTH + SSD_BC:].astype(BF16)

    dt = _softplus(dt_raw + dtb_ref[...])
    da = dt * (-jnp.exp(alog_ref[...]))
    cs = _dot_exact_lhs01(causal.astype(BF16), da) * LOG2E
    cs_t = cs.T

    expand = e_ref[...]
    dt_e = _dot_exact_rhs01(dt, expand)
    cs_e = _dot_exact_rhs01(cs, expand)
    cs_last_e = cs_e[CHUNK - 1:CHUNK, :]
    xd = xs * dt_e
    xd_to_end = (xd * jnp.exp2(cs_last_e - cs_e)).astype(BF16)
    decay_from_start = jnp.exp2(cs_e)
    chunk_decay = jnp.exp2(cs_last_e)

    head_of_lane = lax.broadcasted_iota(jnp.int32, (CHUNK, SSD_GROUP_WIDTH), 1) // SSD_HEAD_DIM
    ys = []
    for g in range(SSD_GROUPS):
        gs = slice(g * SSD_GROUP_WIDTH, (g + 1) * SSD_GROUP_WIDTH)
        b_g = bm[:, g * SSD_STATE:(g + 1) * SSD_STATE]
        c_g = cm[:, g * SSD_STATE:(g + 1) * SSD_STATE]
        scores = _dot_nt(c_g, b_g)
        xd_g = xd[:, gs]
        lhs, rhs = [], []
        for j in range(SSD_HEADS_PER_GROUP):
            h = g * SSD_HEADS_PER_GROUP + j
            seg = cs[:, h:h + 1] - cs_t[h:h + 1, :]
            decay = jnp.exp2(jnp.where(causal, seg, -jnp.inf))
            lhs.append((scores * decay).astype(BF16))
            rhs.append(jnp.where(head_of_lane == j, xd_g, 0.0).astype(BF16))
        y_diag = _dot(jnp.concatenate(lhs, axis=1), jnp.concatenate(rhs, axis=0))
        state = state_ref[g]
        y_off = _dot(c_g, state.astype(BF16)) * decay_from_start[:, gs]
        state_ref[g] = state * chunk_decay[:, gs] + _dot_tn(b_g, xd_to_end[:, gs])
        ys.append(y_diag + y_off)

    y = jnp.concatenate(ys, axis=1) + dsk_ref[...] * xs
    y = y * _silu(z)
    for g in range(SSD_GROUPS):
        gs = slice(g * SSD_GROUP_WIDTH, (g + 1) * SSD_GROUP_WIDTH)
        yg = y[:, gs]
        yn = yg * lax.rsqrt(jnp.mean(yg * yg, axis=-1, keepdims=True) + EPS)
        y_ref[:, OUT_SSD + gs.start:OUT_SSD + gs.stop] = (yn * nw_ref[:, gs]).astype(BF16)


def _splat(x11, shape):
    return jnp.broadcast_to(x11, shape)


def _mlstm_body(mx, mc, o_pre_ref, causal, wq_ref, wk_ref, wv_ref, wif_ref, bif_ref, nw_ref,
                c_ref, n_ref, m_ref, y_ref):
    mcb = mc.astype(BF16)
    mxb = mx.astype(BF16)
    qs, ks, vs = [], [], []
    for h in range(ML_HEADS):
        hs = slice(h * ML_HEAD_DIM, (h + 1) * ML_HEAD_DIM)
        qs.append(_dot(mcb[:, hs], wq_ref[h]))
        ks.append(_dot(mcb[:, hs], wk_ref[h]))
        vs.append(_dot(mxb[:, hs], wv_ref[h]))

    gates = bif_ref[...]
    for part, vals in enumerate((qs, ks, vs)):
        for h in range(ML_HEADS):
            r0 = part * ML_WIDTH + h * ML_HEAD_DIM
            gates = gates + _dot(vals[h].astype(BF16), wif_ref[r0:r0 + ML_HEAD_DIM, :])
    i_pre = gates[:, :LANES] * LOG2E
    log_f = -_softplus(-gates[:, LANES:])

    bcum = _dot_exact_lhs01(causal.astype(BF16), log_f) * LOG2E
    bcum_t = bcum.T
    i_t = i_pre.T
    g_tot = bcum[CHUNK - 1:CHUNK, :]
    w_end = g_tot - bcum + i_pre

    for h in range(ML_HEADS):
        hs = slice(h * ML_HEAD_DIM, (h + 1) * ML_HEAD_DIM)
        q = qs[h]
        qb = q.astype(BF16)
        kf = ks[h] * ML_KSCALE
        kb = kf.astype(BF16)
        v = vs[h]
        c_in = c_ref[h]
        n_in = n_ref[h][0:1, :]
        m_in = m_ref[h][0:1, 0:1]

        bc = bcum[:, h:h + 1]
        dmat = jnp.where(causal, bc - bcum_t[h:h + 1, :] + i_t[h:h + 1, :], -jnp.inf)
        inter_log = bc + m_in
        m_t = jnp.maximum(jnp.max(dmat, axis=1, keepdims=True), inter_log)
        p_intra = jnp.exp2(dmat - m_t)
        s_inter = jnp.exp2(inter_log - m_t)
        qk = _dot_nt(qb, kb) * p_intra
        num = _dot(qk.astype(BF16), v.astype(BF16)) + s_inter * _dot(qb, c_in.astype(BF16))
        den = (jnp.sum(qk, axis=1, keepdims=True)
               + s_inter * jnp.sum(q * n_in, axis=1, keepdims=True))
        out = num / jnp.maximum(jnp.abs(den), jnp.exp2(-m_t))

        mu = jnp.mean(out, axis=1, keepdims=True)
        cen = out - mu
        var = jnp.mean(cen * cen, axis=1, keepdims=True)
        hn = cen * lax.rsqrt(var + EPS) * nw_ref[:, hs]
        y_ref[:, OUT_ML + hs.start:OUT_ML + hs.stop] = (_sigmoid(o_pre_ref[:, hs]) * hn).astype(BF16)

        we = w_end[:, h:h + 1]
        m_loc = jnp.max(we, axis=0, keepdims=True)
        p_end = jnp.exp2(we - m_loc)
        c_loc = _dot_tn(kb, (v * p_end).astype(BF16))
        n_loc = jnp.sum(kf * p_end, axis=0, keepdims=True)
        g_h = g_tot[:, h:h + 1]
        m_new = jnp.maximum(g_h + m_in, m_loc)
        s_old = jnp.exp2(g_h + m_in - m_new)
        s_loc = jnp.exp2(m_loc - m_new)
        c_ref[h] = s_old * c_in + s_loc * c_loc
        n_ref[h] = _splat(s_old * n_in + s_loc * n_loc, (SUBLANES, ML_HEAD_DIM))
        m_ref[h] = _splat(m_new, (SUBLANES, LANES))


def _lru_gates(xc, wa_ref, ba_ref, wx_ref, bx_ref, lam_ref):
    xcb = xc.astype(BF16)
    ra, ix = [], []
    for q in range(LRU_WIDTH // MXU_DIM):
        blk = xcb[:, q * MXU_DIM:(q + 1) * MXU_DIM]
        ra.append(_dot(blk, wa_ref[q]))
        ix.append(_dot(blk, wx_ref[q]))
    r = _sigmoid(jnp.concatenate(ra, axis=1) + ba_ref[...])
    i = _sigmoid(jnp.concatenate(ix, axis=1) + bx_ref[...])
    log_a = (-LRU_C * r) * _softplus(-lam_ref[...])
    a = jnp.exp(log_a)
    u = jnp.sqrt(1.0 - jnp.exp(2.0 * log_a)) * (i * xc)
    return a, u


def _lru_scan(a, u, h_ref):
    def tile(x, k):
        return x[k * SUBLANES:(k + 1) * SUBLANES, :]

    decay, local = [tile(a, 0)], [tile(u, 0)]
    for k in range(1, TILES):
        a_k = tile(a, k)
        local.append(a_k * local[-1] + tile(u, k))
        decay.append(a_k * decay[-1])
    p, q = decay[-1], local[-1]
    sub = lax.broadcasted_iota(jnp.int32, p.shape, 0)
    d = 1
    while d < SUBLANES:
        p_prev = jnp.where(sub >= d, pltpu.roll(p, d, axis=0), 1.0)
        q_prev = jnp.where(sub >= d, pltpu.roll(q, d, axis=0), 0.0)
        q = p * q_prev + q
        p = p * p_prev
        d *= 2
    h_prev = jnp.broadcast_to(h_ref[0:1, :], p.shape)
    run_end = q + p * h_prev
    run_in = jnp.where(sub == 0, h_prev, pltpu.roll(run_end, 1, axis=0))
    h_ref[0:1, :] = run_end[SUBLANES - 1:SUBLANES, :]
    return jnp.concatenate([local[k] + decay[k] * run_in for k in range(TILES)], axis=0)


def _mixer_kernel(proj_ref, dt_ref, cw_ref, cb_ref,
                  dtb_ref, alog_ref, dsk_ref, snw_ref, e_ref,
                  wa_ref, ba_ref, wx_ref, bx_ref, lam_ref,
                  wq_ref, wk_ref, wv_ref, wif_ref, bif_ref, mnw_ref,
                  y_ref,
                  hist_ref, state_ref, h_ref, c_ref, n_ref, m_ref):
    @pl.when(pl.program_id(1) == 0)
    def _():
        for ref in (hist_ref, state_ref, h_ref, c_ref, n_ref, m_ref):
            ref[...] = jnp.zeros(ref.shape, F32)

    causal = _causal_mask()
    conv_in = proj_ref[:, :CONV_WIDTH]
    conv = _causal_conv(conv_in, hist_ref, cw_ref, cb_ref)
    _ssd_body(_silu(conv[:, COL_XBC:COL_XBC + SSD_XBC]), proj_ref[:, COL_Z:COL_Z + SSD_WIDTH],
              dt_ref[...], causal, dtb_ref, alog_ref, dsk_ref, snw_ref, e_ref, state_ref, y_ref)
    _mlstm_body(conv_in[:, COL_ML_X:COL_ML_X + ML_WIDTH], _silu(conv[:, COL_ML_X:COL_ML_X + ML_WIDTH]),
                proj_ref.at[:, COL_ML_O:COL_ML_O + ML_WIDTH], causal, wq_ref, wk_ref, wv_ref, wif_ref,
                bif_ref, mnw_ref, c_ref, n_ref, m_ref, y_ref)
    a, u = _lru_gates(conv[:, COL_LRU_X:COL_LRU_X + LRU_WIDTH], wa_ref, ba_ref, wx_ref, bx_ref, lam_ref)
    hs = _lru_scan(a, u, h_ref)
    gate = proj_ref[:, COL_LRU_GATE:COL_LRU_GATE + LRU_WIDTH]
    y_ref[:, OUT_LRU:OUT_LRU + LRU_WIDTH] = (hs * jax.nn.gelu(gate, approximate=True)).astype(BF16)


def _mixers(proj, dt_raw, p):
    b, s, _ = proj.shape
    pad = LANES - SSD_HEADS
    head = jnp.arange(LANES)[:, None]
    expand = (head == (jnp.arange(SSD_WIDTH)[None, :] // SSD_HEAD_DIM)).astype(BF16)
    w_if, b_if = _pad_gate_weights(p["ml_w_if"], p["ml_b_if"])
    operands = [
        jnp.concatenate([p["ssd_conv_w"], p["lru_conv_w"], p["ml_conv_w"]], axis=1),
        jnp.concatenate([p["ssd_conv_b"], p["lru_conv_b"], p["ml_conv_b"]])[None, :],
        jnp.pad(p["ssd_dt_bias"], (0, pad))[None, :],
        jnp.pad(p["ssd_a_log"], (0, pad))[None, :],
        jnp.repeat(p["ssd_d"], SSD_HEAD_DIM)[None, :],
        p["ssd_norm_w"][None, :],
        expand,
        _block_diag_tiles(p["lru_w_a"], MXU_DIM), p["lru_b_a"][None, :],
        _block_diag_tiles(p["lru_w_x"], MXU_DIM), p["lru_b_x"][None, :],
        p["lru_lambda"][None, :],
        _block_diag_tiles(p["ml_w_q"], ML_HEAD_DIM),
        _block_diag_tiles(p["ml_w_k"], ML_HEAD_DIM),
        _block_diag_tiles(p["ml_w_v"], ML_HEAD_DIM),
        w_if.astype(BF16), b_if, p["ml_norm_w"][None, :],
    ]

    def const(a):
        return pl.BlockSpec(a.shape, lambda i, c, nd=a.ndim: (0,) * nd)

    return pl.pallas_call(
        _mixer_kernel,
        grid=(b, s // CHUNK),
        in_specs=[pl.BlockSpec((None, CHUNK, D_PROJ), lambda i, c: (i, c, 0)),
                  pl.BlockSpec((None, CHUNK, LANES), lambda i, c: (i, c, 0))]
                 + [const(a) for a in operands],
        out_specs=pl.BlockSpec((None, CHUNK, D_MIX), lambda i, c: (i, c, 0)),
        out_shape=jax.ShapeDtypeStruct((b, s, D_MIX), BF16),
        scratch_shapes=[pltpu.VMEM((HIST, CONV_WIDTH), F32),
                        pltpu.VMEM((SSD_GROUPS, SSD_STATE, SSD_GROUP_WIDTH), F32),
                        pltpu.VMEM((SUBLANES, LRU_WIDTH), F32),
                        pltpu.VMEM((ML_HEADS, ML_HEAD_DIM, ML_HEAD_DIM), F32),
                        pltpu.VMEM((ML_HEADS, SUBLANES, ML_HEAD_DIM), F32),
                        pltpu.VMEM((ML_HEADS, SUBLANES, LANES), F32)],
        compiler_params=_params("parallel", "arbitrary"),
        name="mixers",
    )(proj, dt_raw, *operands)


def _out_proj_kernel(h_ref, y_ref, w_ref, wgu32_ref, wd32_ref, o_ref, wgu_ref, wd_ref):
    o_ref[...] = h_ref[...] + _dot(y_ref[...], w_ref[...])

    wgu_ref[...] = wgu32_ref[...].astype(BF16)

    @pl.when(pl.program_id(0) == 0)
    def _():
        wd_ref[...] = wd32_ref[...].astype(BF16)


def _out_proj(h, y, w_out, w_gate_up, w_down, layer, tm, tn):
    m, n = h.shape
    k = y.shape[1]
    nj, ni = n // tn, m // tm
    _, d, f2 = w_gate_up.shape
    f = w_down.shape[1]
    bf16_rows = 2 * SUBLANES
    assert d % (bf16_rows * ni * nj) == 0 and f % (bf16_rows * ni) == 0, (d, f, ni, nj)

    def down_block(j, i):
        return jnp.where(j == 0, i, ni - 1)

    return pl.pallas_call(
        _out_proj_kernel,
        grid=(nj, ni),
        in_specs=[pl.BlockSpec((tm, tn), lambda j, i: (i, j)),
                  pl.BlockSpec((tm, k), lambda j, i: (i, 0)),
                  pl.BlockSpec((None, k, tn), lambda j, i: (layer, 0, j)),
                  pl.BlockSpec((None, d // (ni * nj), f2), lambda j, i: (layer, j * ni + i, 0)),
                  pl.BlockSpec((None, f // ni, d), lambda j, i: (layer, down_block(j, i), 0))],
        out_specs=[pl.BlockSpec((tm, tn), lambda j, i: (i, j)),
                   pl.BlockSpec((d // (ni * nj), f2), lambda j, i: (j * ni + i, 0)),
                   pl.BlockSpec((f // ni, d), lambda j, i: (down_block(j, i), 0))],
        out_shape=[jax.ShapeDtypeStruct((m, n), F32),
                   jax.ShapeDtypeStruct((d, f2), BF16),
                   jax.ShapeDtypeStruct((f, d), BF16)],
        compiler_params=_params("arbitrary", "arbitrary"),
        name="out_proj",
    )(h, y, w_out, w_gate_up, w_down)


def _ffn_kernel(h_ref, nw_ref, wg_ref, wu_ref, wd_ref, fw_ref, o_ref, u_ref, acc_ref, *stage, last_layer):
    j = pl.program_id(1)

    @pl.when(j == 0)
    def _():
        u_ref[...] = _rmsnorm(h_ref[...], nw_ref[...]).astype(BF16)
        acc_ref[...] = jnp.zeros(acc_ref.shape, F32)

    u = u_ref[...]
    gate = _dot(u, wg_ref[...])
    up = _dot(u, wu_ref[...])
    acc_ref[...] += _dot((_silu(gate) * up).astype(BF16), wd_ref[...])

    @pl.when(j == pl.num_programs(1) - 1)
    def _():
        out = h_ref[...] + acc_ref[...]
        if last_layer:
            _store_chunk_unpermuted(_rmsnorm(out, fw_ref[...]), *stage, o_ref)
        else:
            o_ref[...] = out


def _ffn(h, nw, w_gate_up, w_down, fw, tm, tf, last_layer):
    m, d = h.shape
    f = w_down.shape[0]
    nf = f // tf
    return pl.pallas_call(
        functools.partial(_ffn_kernel, last_layer=last_layer),
        grid=(m // tm, nf),
        in_specs=[pl.BlockSpec((tm, d), lambda i, j: (i, 0)),
                  pl.BlockSpec((1, d), lambda i, j: (0, 0)),
                  pl.BlockSpec((d, tf), lambda i, j: (0, j)),
                  pl.BlockSpec((d, tf), lambda i, j: (0, j + nf)),
                  pl.BlockSpec((tf, d), lambda i, j: (j, 0)),
                  pl.BlockSpec((1, d), lambda i, j: (0, 0))],
        out_specs=pl.BlockSpec((tm, d), lambda i, j: (i, 0)),
        out_shape=jax.ShapeDtypeStruct((m, d), F32),
        scratch_shapes=([pltpu.VMEM((tm, d), BF16), pltpu.VMEM((tm, d), F32)]
                        + ([_stage_scratch(tm, d)] if last_layer else [])),
        compiler_params=_params("parallel", "arbitrary"),
        name="ffn_last" if last_layer else "ffn",
    )(h, nw, w_gate_up, w_gate_up, w_down, fw)


def _block_diag_tiles(w, tile):
    nblk, c, _ = w.shape
    col = jnp.arange(tile)
    spread = (jnp.arange(c)[:, None] == (col % c)[None, :]).astype(w.dtype)
    same_block = ((col // c)[:, None] == (col // c)[None, :]).astype(w.dtype)
    rep = jnp.dot(w.reshape(nblk * c, c), spread, precision=lax.Precision.HIGHEST)
    return (rep.reshape(nblk * c // tile, tile, tile) * same_block).astype(BF16)


def _pad_gate_weights(w_if, b_if):
    zw = jnp.zeros((w_if.shape[0], LANES - ML_HEADS), w_if.dtype)
    w = jnp.concatenate([w_if[:, :ML_HEADS], zw, w_if[:, ML_HEADS:], zw], axis=1)
    zb = jnp.zeros((LANES - ML_HEADS,), b_if.dtype)
    b = jnp.concatenate([b_if[:ML_HEADS], zb, b_if[ML_HEADS:], zb])[None, :]
    return w, b


def _tile(total, want):
    return want if total % want == 0 else total


def kernel(x, norm1_w, w_in, ssd_conv_w, ssd_conv_b, ssd_dt_bias, ssd_a_log, ssd_d, ssd_norm_w, lru_conv_w, lru_conv_b, lru_w_a, lru_b_a, lru_w_x, lru_b_x, lru_lambda, ml_conv_w, ml_conv_b, ml_w_q, ml_w_k, ml_w_v, ml_w_if, ml_b_if, ml_norm_w, w_out, norm2_w, w_gate_up, w_down, norm_f_w):
    b, s, d = x.shape
    m = b * s
    depth = w_in.shape[0]
    mixer_params = dict(
        ssd_conv_w=ssd_conv_w, ssd_conv_b=ssd_conv_b, ssd_dt_bias=ssd_dt_bias, ssd_a_log=ssd_a_log,
        ssd_d=ssd_d, ssd_norm_w=ssd_norm_w, lru_conv_w=lru_conv_w, lru_conv_b=lru_conv_b,
        lru_w_a=lru_w_a, lru_b_a=lru_b_a, lru_w_x=lru_w_x, lru_b_x=lru_b_x, lru_lambda=lru_lambda,
        ml_conv_w=ml_conv_w, ml_conv_b=ml_conv_b, ml_w_q=ml_w_q, ml_w_k=ml_w_k, ml_w_v=ml_w_v,
        ml_w_if=ml_w_if, ml_b_if=ml_b_if, ml_norm_w=ml_norm_w)
    w_main, w_dt = _prep_w_in(w_in, 256)
    w_out_b = w_out.astype(BF16)
    tm = _tile(m, 512)
    h = _permute_rows(x.reshape(m, d), _tile(m, 4096))
    for l in range(depth):
        proj, dt_raw = _norm_matmul(h, norm1_w[l][None, :], w_main, w_dt, l, _tile(m, 1024), D_PROJ // 8)
        y = _mixers(proj.reshape(b, s, D_PROJ), dt_raw.reshape(b, s, LANES),
                    {k: v[l] for k, v in mixer_params.items()})
        h, w_gate_up_b, w_down_b = _out_proj(h, y.reshape(m, D_MIX), w_out_b, w_gate_up, w_down, l, tm, 1024)
        h = _ffn(h, norm2_w[l][None, :], w_gate_up_b, w_down_b, norm_f_w[None, :], tm, 512,
                 last_layer=(l == depth - 1))
    return h.reshape(b, s, d)
```

```python
import functools

import jax
import jax.numpy as jnp
from jax import lax
from jax.experimental import pallas as pl
from jax.experimental.pallas import tpu as pltpu

F32 = jnp.float32
BF16 = jnp.bfloat16

EPS = 1e-6
LOG2E = 1.4426950408889634
CONV_K = 4
CHUNK = 128
SUBLANES = 8
LANES = 128
MXU_DIM = 256
TILES = CHUNK // SUBLANES
HIST = (CONV_K - 1) * SUBLANES

SSD_WIDTH = 2048
SSD_HEAD_DIM = 64
SSD_HEADS = 32
SSD_GROUPS = 8
SSD_STATE = 128
SSD_BC = SSD_GROUPS * SSD_STATE
SSD_XBC = SSD_WIDTH + 2 * SSD_BC
SSD_GROUP_WIDTH = SSD_WIDTH // SSD_GROUPS
SSD_HEADS_PER_GROUP = SSD_HEADS // SSD_GROUPS
LRU_WIDTH = 1024
LRU_C = 8.0
ML_WIDTH = 1024
ML_HEADS = 4
ML_HEAD_DIM = 256
ML_KSCALE = ML_HEAD_DIM ** -0.5
D_MIX = SSD_WIDTH + LRU_WIDTH + ML_WIDTH

COL_XBC = 0
COL_LRU_X = COL_XBC + SSD_XBC
COL_ML_X = COL_LRU_X + LRU_WIDTH
CONV_WIDTH = COL_ML_X + ML_WIDTH
COL_Z = CONV_WIDTH
COL_LRU_GATE = COL_Z + SSD_WIDTH
COL_ML_O = COL_LRU_GATE + LRU_WIDTH
D_PROJ = COL_ML_O + ML_WIDTH
SRC_Z = 0
SRC_XBC = SRC_Z + SSD_WIDTH
SRC_DT = SRC_XBC + SSD_XBC
SRC_LRU_GATE = SRC_DT + SSD_HEADS
SRC_LRU_X = SRC_LRU_GATE + LRU_WIDTH
SRC_ML_X = SRC_LRU_X + LRU_WIDTH
SRC_ML_O = SRC_ML_X + ML_WIDTH
D_IN = SRC_ML_O + ML_WIDTH
OUT_SSD = 0
OUT_LRU = OUT_SSD + SSD_WIDTH
OUT_ML = OUT_LRU + LRU_WIDTH

VMEM_LIMIT_BYTES = 56 * 1024 * 1024


def _params(*semantics):
    return pltpu.CompilerParams(dimension_semantics=semantics, vmem_limit_bytes=VMEM_LIMIT_BYTES)


def _dot(a, b):
    return jnp.dot(a, b, preferred_element_type=F32)


def _dot_nt(a, b):
    return lax.dot_general(a, b, (((1,), (1,)), ((), ())), preferred_element_type=F32)


def _dot_tn(a, b):
    return lax.dot_general(a, b, (((0,), (0,)), ((), ())), preferred_element_type=F32)


def _split3(x):
    hi = x.astype(BF16)
    r1 = x - hi.astype(F32)
    mid = r1.astype(BF16)
    lo = (r1 - mid.astype(F32)).astype(BF16)
    return hi, mid, lo


def _dot_exact_lhs01(m01, x):
    hi, mid, lo = _split3(x)
    return (_dot(m01, hi) + _dot(m01, mid)) + _dot(m01, lo)


def _dot_exact_rhs01(x, m01):
    hi, mid, lo = _split3(x)
    return (_dot(hi, m01) + _dot(mid, m01)) + _dot(lo, m01)


def _sigmoid(x):
    return 1.0 / (1.0 + jnp.exp2(x * (-LOG2E)))


def _silu(x):
    return x * _sigmoid(x)


def _softplus(x):
    return jnp.maximum(x, 0.0) + jnp.log1p(jnp.exp(-jnp.abs(x)))


def _rmsnorm(x, w):
    return x * lax.rsqrt(jnp.mean(x * x, axis=-1, keepdims=True) + EPS) * w


def _store_chunk_unpermuted(val, stage_ref, dst_ref):
    rows, width = val.shape
    for i in range(width // LANES):
        stage_ref[i] = val[:, i * LANES:(i + 1) * LANES]
    for i in range(width // LANES):
        for c in range(rows // CHUNK):
            for j in range(SUBLANES):
                dst_ref[c * CHUNK + j * TILES:c * CHUNK + (j + 1) * TILES, i * LANES:(i + 1) * LANES] = (
                    stage_ref[i, pl.ds(c * CHUNK + j, TILES, stride=SUBLANES), :])


def _stage_scratch(rows, width):
    return pltpu.VMEM((width // LANES, rows, LANES), F32)


def _causal_mask():
    def pos(dim):
        r = lax.broadcasted_iota(jnp.int32, (CHUNK, CHUNK), dim)
        return (r % SUBLANES) * TILES + r // SUBLANES
    return pos(0) >= pos(1)


def _permute_rows_kernel(x_ref, o_ref):
    for c in range(x_ref.shape[0] // CHUNK):
        for k in range(TILES):
            o_ref[c * CHUNK + k * SUBLANES:c * CHUNK + (k + 1) * SUBLANES, :] = (
                x_ref[pl.ds(c * CHUNK + k, SUBLANES, stride=TILES), :])


def _permute_rows(x, rows):
    m, d = x.shape
    return pl.pallas_call(
        _permute_rows_kernel,
        grid=(m // rows, d // LANES),
        in_specs=[pl.BlockSpec((rows, LANES), lambda i, j: (i, j))],
        out_specs=pl.BlockSpec((rows, LANES), lambda i, j: (i, j)),
        out_shape=jax.ShapeDtypeStruct((m, d), x.dtype),
        compiler_params=_params("parallel", "parallel"),
        name="permute_rows",
    )(x)


def _prep_w_in_kernel(w_ref, wm_ref, wdt_ref):
    for dst, src, width in ((COL_XBC, SRC_XBC, SSD_XBC), (COL_LRU_X, SRC_LRU_X, LRU_WIDTH),
                            (COL_ML_X, SRC_ML_X, ML_WIDTH), (COL_Z, SRC_Z, SSD_WIDTH),
                            (COL_LRU_GATE, SRC_LRU_GATE, LRU_WIDTH), (COL_ML_O, SRC_ML_O, ML_WIDTH)):
        wm_ref[dst:dst + width, :] = w_ref[src:src + width, :].astype(BF16)
    wdt_ref[0:SSD_HEADS, :] = w_ref[SRC_DT:SRC_DT + SSD_HEADS, :].astype(BF16)
    wdt_ref[SSD_HEADS:, :] = jnp.zeros((LANES - SSD_HEADS, wdt_ref.shape[1]), BF16)


def _prep_w_in(w_in, cols):
    depth, d, _ = w_in.shape
    return pl.pallas_call(
        _prep_w_in_kernel,
        grid=(depth, d // cols),
        in_specs=[pl.BlockSpec((None, D_IN, cols), lambda l, i: (l, 0, i))],
        out_specs=[pl.BlockSpec((None, D_PROJ, cols), lambda l, i: (l, 0, i)),
                   pl.BlockSpec((None, LANES, cols), lambda l, i: (l, 0, i))],
        out_shape=[jax.ShapeDtypeStruct((depth, D_PROJ, d), BF16),
                   jax.ShapeDtypeStruct((depth, LANES, d), BF16)],
        compiler_params=_params("parallel", "parallel"),
        name="prep_w_in",
    )(jnp.swapaxes(w_in, 1, 2))


def _norm_matmul_kernel(x_ref, nw_ref, w_ref, wdt_ref, o_ref, dt_ref, u_ref):
    @pl.when(pl.program_id(1) == 0)
    def _():
        u = _rmsnorm(x_ref[...], nw_ref[...]).astype(BF16)
        u_ref[...] = u
        dt_ref[...] = _dot_nt(u, wdt_ref[...])

    o_ref[...] = _dot_nt(u_ref[...], w_ref[...])


def _norm_matmul(x, nw, w, wdt, layer, tm, tn):
    m, k = x.shape
    n = w.shape[1]
    return pl.pallas_call(
        _norm_matmul_kernel,
        grid=(m // tm, n // tn),
        in_specs=[pl.BlockSpec((tm, k), lambda i, j: (i, 0)),
                  pl.BlockSpec((1, k), lambda i, j: (0, 0)),
                  pl.BlockSpec((None, tn, k), lambda i, j: (layer, j, 0)),
                  pl.BlockSpec((None, LANES, k), lambda i, j: (layer, 0, 0))],
        out_specs=[pl.BlockSpec((tm, tn), lambda i, j: (i, j)),
                   pl.BlockSpec((tm, LANES), lambda i, j: (i, 0))],
        out_shape=[jax.ShapeDtypeStruct((m, n), F32), jax.ShapeDtypeStruct((m, LANES), F32)],
        scratch_shapes=[pltpu.VMEM((tm, k), BF16)],
        compiler_params=_params("parallel", "arbitrary"),
        name="norm_in_proj",
    )(x, nw, w, wdt)


def _causal_conv(cur, hist_ref, w_ref, b_ref):
    width = cur.shape[1]
    sub = lax.broadcasted_iota(jnp.int32, (SUBLANES, width), 0)
    wrapped = []
    for i in range(CONV_K - 1):
        lo = CHUNK - HIST + i * SUBLANES
        from_cur = pltpu.roll(cur[lo:lo + SUBLANES, :], 1, axis=0)
        from_prev = pltpu.roll(hist_ref[i * SUBLANES:(i + 1) * SUBLANES, :], 1, axis=0)
        wrapped.append(jnp.where(sub == 0, from_prev, from_cur))
    acc = cur * w_ref[CONV_K - 1:CONV_K, :] + b_ref[...]
    for s in range(1, CONV_K):
        shifted = jnp.concatenate(wrapped[CONV_K - 1 - s:] + [cur[0:CHUNK - s * SUBLANES, :]], axis=0)
        acc = acc + shifted * w_ref[CONV_K - 1 - s:CONV_K - s, :]
    hist_ref[...] = cur[CHUNK - HIST:, :]
    return acc


def _ssd_body(xbc, z, dt_raw, causal, dtb_ref, alog_ref, dsk_ref, nw_ref, e_ref, state_ref, y_ref):
    xs = xbc[:, :SSD_WIDTH]
    bm = xbc[:, SSD_WIDTH:SSD_WIDTH + SSD_BC].astype(BF16)
    cm = xbc[:, SSD_WIDTH + SSD_BC:].astype(BF16)

    dt = _softplus(dt_raw + dtb_ref[...])
    da = dt * (-jnp.exp(alog_ref[...]))
    cs = _dot_exact_lhs01(causal.astype(BF16), da) * LOG2E
    cs_t = cs.T

    expand = e_ref[...]
    dt_e = _dot_exact_rhs01(dt, expand)
    cs_e = _dot_exact_rhs01(cs, expand)
    cs_last_e = cs_e[CHUNK - 1:CHUNK, :]
    xd = xs * dt_e
    xd_to_end = (xd * jnp.exp2(cs_last_e - cs_e)).astype(BF16)
    decay_from_start = jnp.exp2(cs_e)
    chunk_decay = jnp.exp2(cs_last_e)

    head_of_lane = lax.broadcasted_iota(jnp.int32, (CHUNK, SSD_GROUP_WIDTH), 1) // SSD_HEAD_DIM
    ys = []
    for g in range(SSD_GROUPS):
        gs = slice(g * SSD_GROUP_WIDTH, (g + 1) * SSD_GROUP_WIDTH)
        b_g = bm[:, g * SSD_STATE:(g + 1) * SSD_STATE]
        c_g = cm[:, g * SSD_STATE:(g + 1) * SSD_STATE]
        scores = _dot_nt(c_g, b_g)
        xd_g = xd[:, gs]
        lhs, rhs = [], []
        for j in range(SSD_HEADS_PER_GROUP):
            h = g * SSD_HEADS_PER_GROUP + j
            seg = cs[:, h:h + 1] - cs_t[h:h + 1, :]
            decay = jnp.exp2(jnp.where(causal, seg, -jnp.inf))
            lhs.append((scores * decay).astype(BF16))
            rhs.append(jnp.where(head_of_lane == j, xd_g, 0.0).astype(BF16))
        y_diag = _dot(jnp.concatenate(lhs, axis=1), jnp.concatenate(rhs, axis=0))
        state = state_ref[g]
        y_off = _dot(c_g, state.astype(BF16)) * decay_from_start[:, gs]
        state_ref[g] = state * chunk_decay[:, gs] + _dot_tn(b_g, xd_to_end[:, gs])
        ys.append(y_diag + y_off)

    y = jnp.concatenate(ys, axis=1) + dsk_ref[...] * xs
    y = y * _silu(z)
    for g in range(SSD_GROUPS):
        gs = slice(g * SSD_GROUP_WIDTH, (g + 1) * SSD_GROUP_WIDTH)
        yg = y[:, gs]
        yn = yg * lax.rsqrt(jnp.mean(yg * yg, axis=-1, keepdims=True) + EPS)
        y_ref[:, OUT_SSD + gs.start:OUT_SSD + gs.stop] = (yn * nw_ref[:, gs]).astype(BF16)


def _splat(x11, shape):
    return jnp.broadcast_to(x11, shape)


def _mlstm_body(mx, mc, o_pre_ref, causal, wq_ref, wk_ref, wv_ref, wif_ref, bif_ref, nw_ref,
                c_ref, n_ref, m_ref, y_ref):
    mcb = mc.astype(BF16)
    mxb = mx.astype(BF16)
    qs, ks, vs = [], [], []
    for h in range(ML_HEADS):
        hs = slice(h * ML_HEAD_DIM, (h + 1) * ML_HEAD_DIM)
        qs.append(_dot(mcb[:, hs], wq_ref[h]))
        ks.append(_dot(mcb[:, hs], wk_ref[h]))
        vs.append(_dot(mxb[:, hs], wv_ref[h]))

    gates = bif_ref[...]
    for part, vals in enumerate((qs, ks, vs)):
        for h in range(ML_HEADS):
            r0 = part * ML_WIDTH + h * ML_HEAD_DIM
            gates = gates + _dot(vals[h].astype(BF16), wif_ref[r0:r0 + ML_HEAD_DIM, :])
    i_pre = gates[:, :LANES] * LOG2E
    log_f = -_softplus(-gates[:, LANES:])

    bcum = _dot_exact_lhs01(causal.astype(BF16), log_f) * LOG2E
    bcum_t = bcum.T
    i_t = i_pre.T
    g_tot = bcum[CHUNK - 1:CHUNK, :]
    w_end = g_tot - bcum + i_pre

    for h in range(ML_HEADS):
        hs = slice(h * ML_HEAD_DIM, (h + 1) * ML_HEAD_DIM)
        q = qs[h]
        qb = q.astype(BF16)
        kf = ks[h] * ML_KSCALE
        kb = kf.astype(BF16)
        v = vs[h]
        c_in = c_ref[h]
        n_in = n_ref[h][0:1, :]
        m_in = m_ref[h][0:1, 0:1]

        bc = bcum[:, h:h + 1]
        dmat = jnp.where(causal, bc - bcum_t[h:h + 1, :] + i_t[h:h + 1, :], -jnp.inf)
        inter_log = bc + m_in
        m_t = jnp.maximum(jnp.max(dmat, axis=1, keepdims=True), inter_log)
        p_intra = jnp.exp2(dmat - m_t)
        s_inter = jnp.exp2(inter_log - m_t)
        qk = _dot_nt(qb, kb) * p_intra
        num = _dot(qk.astype(BF16), v.astype(BF16)) + s_inter * _dot(qb, c_in.astype(BF16))
        den = (jnp.sum(qk, axis=1, keepdims=True)
               + s_inter * jnp.sum(q * n_in, axis=1, keepdims=True))
        out = num / jnp.maximum(jnp.abs(den), jnp.exp2(-m_t))

        mu = jnp.mean(out, axis=1, keepdims=True)
        cen = out - mu
        var = jnp.mean(cen * cen, axis=1, keepdims=True)
        hn = cen * lax.rsqrt(var + EPS) * nw_ref[:, hs]
        y_ref[:, OUT_ML + hs.start:OUT_ML + hs.stop] = (_sigmoid(o_pre_ref[:, hs]) * hn).astype(BF16)

        we = w_end[:, h:h + 1]
        m_loc = jnp.max(we, axis=0, keepdims=True)
        p_end = jnp.exp2(we - m_loc)
        c_loc = _dot_tn(kb, (v * p_end).astype(BF16))
        n_loc = jnp.sum(kf * p_end, axis=0, keepdims=True)
        g_h = g_tot[:, h:h + 1]
        m_new = jnp.maximum(g_h + m_in, m_loc)
        s_old = jnp.exp2(g_h + m_in - m_new)
        s_loc = jnp.exp2(m_loc - m_new)
        c_ref[h] = s_old * c_in + s_loc * c_loc
        n_ref[h] = _splat(s_old * n_in + s_loc * n_loc, (SUBLANES, ML_HEAD_DIM))
        m_ref[h] = _splat(m_new, (SUBLANES, LANES))


def _lru_gates(xc, wa_ref, ba_ref, wx_ref, bx_ref, lam_ref):
    xcb = xc.astype(BF16)
    ra, ix = [], []
    for q in range(LRU_WIDTH // MXU_DIM):
        blk = xcb[:, q * MXU_DIM:(q + 1) * MXU_DIM]
        ra.append(_dot(blk, wa_ref[q]))
        ix.append(_dot(blk, wx_ref[q]))
    r = _sigmoid(jnp.concatenate(ra, axis=1) + ba_ref[...])
    i = _sigmoid(jnp.concatenate(ix, axis=1) + bx_ref[...])
    log_a = (-LRU_C * r) * _softplus(-lam_ref[...])
    a = jnp.exp(log_a)
    u = jnp.sqrt(1.0 - jnp.exp(2.0 * log_a)) * (i * xc)
    return a, u


def _lru_scan(a, u, h_ref):
    def tile(x, k):
        return x[k * SUBLANES:(k + 1) * SUBLANES, :]

    decay, local = [tile(a, 0)], [tile(u, 0)]
    for k in range(1, TILES):
        a_k = tile(a, k)
        local.append(a_k * local[-1] + tile(u, k))
        decay.append(a_k * decay[-1])
    p, q = decay[-1], local[-1]
    sub = lax.broadcasted_iota(jnp.int32, p.shape, 0)
    d = 1
    while d < SUBLANES:
        p_prev = jnp.where(sub >= d, pltpu.roll(p, d, axis=0), 1.0)
        q_prev = jnp.where(sub >= d, pltpu.roll(q, d, axis=0), 0.0)
        q = p * q_prev + q
        p = p * p_prev
        d *= 2
    h_prev = jnp.broadcast_to(h_ref[0:1, :], p.shape)
    run_end = q + p * h_prev
    run_in = jnp.where(sub == 0, h_prev, pltpu.roll(run_end, 1, axis=0))
    h_ref[0:1, :] = run_end[SUBLANES - 1:SUBLANES, :]
    return jnp.concatenate([local[k] + decay[k] * run_in for k in range(TILES)], axis=0)


def _mixer_kernel(proj_ref, dt_ref, cw_ref, cb_ref,
                  dtb_ref, alog_ref, dsk_ref, snw_ref, e_ref,
                  wa_ref, ba_ref, wx_ref, bx_ref, lam_ref,
                  wq_ref, wk_ref, wv_ref, wif_ref, bif_ref, mnw_ref,
                  y_ref,
                  hist_ref, state_ref, h_ref, c_ref, n_ref, m_ref):
    @pl.when(pl.program_id(1) == 0)
    def _():
        for ref in (hist_ref, state_ref, h_ref, c_ref, n_ref, m_ref):
            ref[...] = jnp.zeros(ref.shape, F32)

    causal = _causal_mask()
    conv_in = proj_ref[:, :CONV_WIDTH]
    conv = _causal_conv(conv_in, hist_ref, cw_ref, cb_ref)
    _ssd_body(_silu(conv[:, COL_XBC:COL_XBC + SSD_XBC]), proj_ref[:, COL_Z:COL_Z + SSD_WIDTH],
              dt_ref[...], causal, dtb_ref, alog_ref, dsk_ref, snw_ref, e_ref, state_ref, y_ref)
    _mlstm_body(conv_in[:, COL_ML_X:COL_ML_X + ML_WIDTH], _silu(conv[:, COL_ML_X:COL_ML_X + ML_WIDTH]),
                proj_ref.at[:, COL_ML_O:COL_ML_O + ML_WIDTH], causal, wq_ref, wk_ref, wv_ref, wif_ref,
                bif_ref, mnw_ref, c_ref, n_ref, m_ref, y_ref)
    a, u = _lru_gates(conv[:, COL_LRU_X:COL_LRU_X + LRU_WIDTH], wa_ref, ba_ref, wx_ref, bx_ref, lam_ref)
    hs = _lru_scan(a, u, h_ref)
    gate = proj_ref[:, COL_LRU_GATE:COL_LRU_GATE + LRU_WIDTH]
    y_ref[:, OUT_LRU:OUT_LRU + LRU_WIDTH] = (hs * jax.nn.gelu(gate, approximate=True)).astype(BF16)


def _mixers(proj, dt_raw, p):
    b, s, _ = proj.shape
    pad = LANES - SSD_HEADS
    head = jnp.arange(LANES)[:, None]
    expand = (head == (jnp.arange(SSD_WIDTH)[None, :] // SSD_HEAD_DIM)).astype(BF16)
    w_if, b_if = _pad_gate_weights(p["ml_w_if"], p["ml_b_if"])
    operands = [
        jnp.concatenate([p["ssd_conv_w"], p["lru_conv_w"], p["ml_conv_w"]], axis=1),
        jnp.concatenate([p["ssd_conv_b"], p["lru_conv_b"], p["ml_conv_b"]])[None, :],
        jnp.pad(p["ssd_dt_bias"], (0, pad))[None, :],
        jnp.pad(p["ssd_a_log"], (0, pad))[None, :],
        jnp.repeat(p["ssd_d"], SSD_HEAD_DIM)[None, :],
        p["ssd_norm_w"][None, :],
        expand,
        _block_diag_tiles(p["lru_w_a"], MXU_DIM), p["lru_b_a"][None, :],
        _block_diag_tiles(p["lru_w_x"], MXU_DIM), p["lru_b_x"][None, :],
        p["lru_lambda"][None, :],
        _block_diag_tiles(p["ml_w_q"], ML_HEAD_DIM),
        _block_diag_tiles(p["ml_w_k"], ML_HEAD_DIM),
        _block_diag_tiles(p["ml_w_v"], ML_HEAD_DIM),
        w_if.astype(BF16), b_if, p["ml_norm_w"][None, :],
    ]

    def const(a):
        return pl.BlockSpec(a.shape, lambda i, c, nd=a.ndim: (0,) * nd)

    return pl.pallas_call(
        _mixer_kernel,
        grid=(b, s // CHUNK),
        in_specs=[pl.BlockSpec((None, CHUNK, D_PROJ), lambda i, c: (i, c, 0)),
                  pl.BlockSpec((None, CHUNK, LANES), lambda i, c: (i, c, 0))]
                 + [const(a) for a in operands],
        out_specs=pl.BlockSpec((None, CHUNK, D_MIX), lambda i, c: (i, c, 0)),
        out_shape=jax.ShapeDtypeStruct((b, s, D_MIX), BF16),
        scratch_shapes=[pltpu.VMEM((HIST, CONV_WIDTH), F32),
                        pltpu.VMEM((SSD_GROUPS, SSD_STATE, SSD_GROUP_WIDTH), F32),
                        pltpu.VMEM((SUBLANES, LRU_WIDTH), F32),
                        pltpu.VMEM((ML_HEADS, ML_HEAD_DIM, ML_HEAD_DIM), F32),
                        pltpu.VMEM((ML_HEADS, SUBLANES, ML_HEAD_DIM), F32),
                        pltpu.VMEM((ML_HEADS, SUBLANES, LANES), F32)],
        compiler_params=_params("parallel", "arbitrary"),
        name="mixers",
    )(proj, dt_raw, *operands)


def _out_proj_kernel(h_ref, y_ref, w_ref, wgu32_ref, wd32_ref, o_ref, wgu_ref, wd_ref):
    o_ref[...] = h_ref[...] + _dot(y_ref[...], w_ref[...])

    wgu_ref[...] = wgu32_ref[...].astype(BF16)

    @pl.when(pl.program_id(0) == 0)
    def _():
        wd_ref[...] = wd32_ref[...].astype(BF16)


def _out_proj(h, y, w_out, w_gate_up, w_down, layer, tm, tn):
    m, n = h.shape
    k = y.shape[1]
    nj, ni = n // tn, m // tm
    _, d, f2 = w_gate_up.shape
    f = w_down.shape[1]
    bf16_rows = 2 * SUBLANES
    assert d % (bf16_rows * ni * nj) == 0 and f % (bf16_rows * ni) == 0, (d, f, ni, nj)

    def down_block(j, i):
        return jnp.where(j == 0, i, ni - 1)

    return pl.pallas_call(
        _out_proj_kernel,
        grid=(nj, ni),
        in_specs=[pl.BlockSpec((tm, tn), lambda j, i: (i, j)),
                  pl.BlockSpec((tm, k), lambda j, i: (i, 0)),
                  pl.BlockSpec((None, k, tn), lambda j, i: (layer, 0, j)),
                  pl.BlockSpec((None, d // (ni * nj), f2), lambda j, i: (layer, j * ni + i, 0)),
                  pl.BlockSpec((None, f // ni, d), lambda j, i: (layer, down_block(j, i), 0))],
        out_specs=[pl.BlockSpec((tm, tn), lambda j, i: (i, j)),
                   pl.BlockSpec((d // (ni * nj), f2), lambda j, i: (j * ni + i, 0)),
                   pl.BlockSpec((f // ni, d), lambda j, i: (down_block(j, i), 0))],
        out_shape=[jax.ShapeDtypeStruct((m, n), F32),
                   jax.ShapeDtypeStruct((d, f2), BF16),
                   jax.ShapeDtypeStruct((f, d), BF16)],
        compiler_params=_params("arbitrary", "arbitrary"),
        name="out_proj",
    )(h, y, w_out, w_gate_up, w_down)


def _ffn_kernel(h_ref, nw_ref, wg_ref, wu_ref, wd_ref, fw_ref, o_ref, u_ref, acc_ref, *stage, last_layer):
    j = pl.program_id(1)

    @pl.when(j == 0)
    def _():
        u_ref[...] = _rmsnorm(h_ref[...], nw_ref[...]).astype(BF16)
        acc_ref[...] = jnp.zeros(acc_ref.shape, F32)

    u = u_ref[...]
    gate = _dot(u, wg_ref[...])
    up = _dot(u, wu_ref[...])
    acc_ref[...] += _dot((_silu(gate) * up).astype(BF16), wd_ref[...])

    @pl.when(j == pl.num_programs(1) - 1)
    def _():
        out = h_ref[...] + acc_ref[...]
        if last_layer:
            _store_chunk_unpermuted(_rmsnorm(out, fw_ref[...]), *stage, o_ref)
        else:
            o_ref[...] = out


def _ffn(h, nw, w_gate_up, w_down, fw, tm, tf, last_layer):
    m, d = h.shape
    f = w_down.shape[0]
    nf = f // tf
    return pl.pallas_call(
        functools.partial(_ffn_kernel, last_layer=last_layer),
        grid=(m // tm, nf),
        in_specs=[pl.BlockSpec((tm, d), lambda i, j: (i, 0)),
                  pl.BlockSpec((1, d), lambda i, j: (0, 0)),
                  pl.BlockSpec((d, tf), lambda i, j: (0, j)),
                  pl.BlockSpec((d, tf), lambda i, j: (0, j + nf)),
                  pl.BlockSpec((tf, d), lambda i, j: (j, 0)),
                  pl.BlockSpec((1, d), lambda i, j: (0, 0))],
        out_specs=pl.BlockSpec((tm, d), lambda i, j: (i, 0)),
        out_shape=jax.ShapeDtypeStruct((m, d), F32),
        scratch_shapes=([pltpu.VMEM((tm, d), BF16), pltpu.VMEM((tm, d), F32)]
                        + ([_stage_scratch(tm, d)] if last_layer else [])),
        compiler_params=_params("parallel", "arbitrary"),
        name="ffn_last" if last_layer else "ffn",
    )(h, nw, w_gate_up, w_gate_up, w_down, fw)


def _block_diag_tiles(w, tile):
    nblk, c, _ = w.shape
    col = jnp.arange(tile)
    spread = (jnp.arange(c)[:, None] == (col % c)[None, :]).astype(w.dtype)
    same_block = ((col // c)[:, None] == (col // c)[None, :]).astype(w.dtype)
    rep = jnp.dot(w.reshape(nblk * c, c), spread, precision=lax.Precision.HIGHEST)
    return (rep.reshape(nblk * c // tile, tile, tile) * same_block).astype(BF16)


def _pad_gate_weights(w_if, b_if):
    zw = jnp.zeros((w_if.shape[0], LANES - ML_HEADS), w_if.dtype)
    w = jnp.concatenate([w_if[:, :ML_HEADS], zw, w_if[:, ML_HEADS:], zw], axis=1)
    zb = jnp.zeros((LANES - ML_HEADS,), b_if.dtype)
    b = jnp.concatenate([b_if[:ML_HEADS], zb, b_if[ML_HEADS:], zb])[None, :]
    return w, b


def _tile(total, want):
    return want if total % want == 0 else total


def kernel(x, norm1_w, w_in, ssd_conv_w, ssd_conv_b, ssd_dt_bias, ssd_a_log, ssd_d, ssd_norm_w, lru_conv_w, lru_conv_b, lru_w_a, lru_b_a, lru_w_x, lru_b_x, lru_lambda, ml_conv_w, ml_conv_b, ml_w_q, ml_w_k, ml_w_v, ml_w_if, ml_b_if, ml_norm_w, w_out, norm2_w, w_gate_up, w_down, norm_f_w):
    b, s, d = x.shape
    m = b * s
    depth = w_in.shape[0]
    mixer_params = dict(
        ssd_conv_w=ssd_conv_w, ssd_conv_b=ssd_conv_b, ssd_dt_bias=ssd_dt_bias, ssd_a_log=ssd_a_log,
        ssd_d=ssd_d, ssd_norm_w=ssd_norm_w, lru_conv_w=lru_conv_w, lru_conv_b=lru_conv_b,
        lru_w_a=lru_w_a, lru_b_a=lru_b_a, lru_w_x=lru_w_x, lru_b_x=lru_b_x, lru_lambda=lru_lambda,
        ml_conv_w=ml_conv_w, ml_conv_b=ml_conv_b, ml_w_q=ml_w_q, ml_w_k=ml_w_k, ml_w_v=ml_w_v,
        ml_w_if=ml_w_if, ml_b_if=ml_b_if, ml_norm_w=ml_norm_w)
    w_main, w_dt = _prep_w_in(w_in, 256)
    w_out_b = w_out.astype(BF16)
    tm = _tile(m, 512)
    h = _permute_rows(x.reshape(m, d), _tile(m, 4096))
    for l in range(depth):
        proj, dt_raw = _norm_matmul(h, norm1_w[l][None, :], w_main, w_dt, l, _tile(m, 1024), D_PROJ // 8)
        y = _mixers(proj.reshape(b, s, D_PROJ), dt_raw.reshape(b, s, LANES),
                    {k: v[l] for k, v in mixer_params.items()})
        h, w_gate_up_b, w_down_b = _out_proj(h, y.reshape(m, D_MIX), w_out_b, w_gate_up, w_down, l, tm, 1024)
        h = _ffn(h, norm2_w[l][None, :], w_gate_up_b, w_down_b, norm_f_w[None, :], tm, 512,
                 last_layer=(l == depth - 1))
    return h.reshape(b, s, d)
```

```python
import functools

import jax
import jax.numpy as jnp
from jax import lax
from jax.experimental import pallas as pl
from jax.experimental.pallas import tpu as pltpu

F32 = jnp.float32
BF16 = jnp.bfloat16

EPS = 1e-6
LOG2E = 1.4426950408889634
CONV_K = 4
CHUNK = 128
SUBLANES = 8
LANES = 128
MXU_DIM = 256
TILES = CHUNK // SUBLANES
HIST = (CONV_K - 1) * SUBLANES

SSD_WIDTH = 2048
SSD_HEAD_DIM = 64
SSD_HEADS = 32
SSD_GROUPS = 8
SSD_STATE = 128
SSD_BC = SSD_GROUPS * SSD_STATE
SSD_XBC = SSD_WIDTH + 2 * SSD_BC
SSD_GROUP_WIDTH = SSD_WIDTH // SSD_GROUPS
SSD_HEADS_PER_GROUP = SSD_HEADS // SSD_GROUPS
LRU_WIDTH = 1024
LRU_C = 8.0
ML_WIDTH = 1024
ML_HEADS = 4
ML_HEAD_DIM = 256
ML_KSCALE = ML_HEAD_DIM ** -0.5
D_MIX = SSD_WIDTH + LRU_WIDTH + ML_WIDTH

COL_XBC = 0
COL_LRU_X = COL_XBC + SSD_XBC
COL_ML_X = COL_LRU_X + LRU_WIDTH
CONV_WIDTH = COL_ML_X + ML_WIDTH
COL_Z = CONV_WIDTH
COL_LRU_GATE = COL_Z + SSD_WIDTH
COL_ML_O = COL_LRU_GATE + LRU_WIDTH
D_PROJ = COL_ML_O + ML_WIDTH
SRC_Z = 0
SRC_XBC = SRC_Z + SSD_WIDTH
SRC_DT = SRC_XBC + SSD_XBC
SRC_LRU_GATE = SRC_DT + SSD_HEADS
SRC_LRU_X = SRC_LRU_GATE + LRU_WIDTH
SRC_ML_X = SRC_LRU_X + LRU_WIDTH
SRC_ML_O = SRC_ML_X + ML_WIDTH
D_IN = SRC_ML_O + ML_WIDTH
OUT_SSD = 0
OUT_LRU = OUT_SSD + SSD_WIDTH
OUT_ML = OUT_LRU + LRU_WIDTH

VMEM_LIMIT_BYTES = 56 * 1024 * 1024


def _params(*semantics):
    return pltpu.CompilerParams(dimension_semantics=semantics, vmem_limit_bytes=VMEM_LIMIT_BYTES)


def _dot(a, b):
    return jnp.dot(a, b, preferred_element_type=F32)


def _dot_nt(a, b):
    return lax.dot_general(a, b, (((1,), (1,)), ((), ())), preferred_element_type=F32)


def _dot_tn(a, b):
    return lax.dot_general(a, b, (((0,), (0,)), ((), ())), preferred_element_type=F32)


def _split3(x):
    hi = x.astype(BF16)
    r1 = x - hi.astype(F32)
    mid = r1.astype(BF16)
    lo = (r1 - mid.astype(F32)).astype(BF16)
    return hi, mid, lo


def _dot_exact_lhs01(m01, x):
    hi, mid, lo = _split3(x)
    return (_dot(m01, hi) + _dot(m01, mid)) + _dot(m01, lo)


def _dot_exact_rhs01(x, m01):
    hi, mid, lo = _split3(x)
    return (_dot(hi, m01) + _dot(mid, m01)) + _dot(lo, m01)


def _sigmoid(x):
    return 1.0 / (1.0 + jnp.exp2(x * (-LOG2E)))


def _silu(x):
    return x * _sigmoid(x)


def _softplus(x):
    return jnp.maximum(x, 0.0) + jnp.log1p(jnp.exp(-jnp.abs(x)))


def _rmsnorm(x, w):
    return x * lax.rsqrt(jnp.mean(x * x, axis=-1, keepdims=True) + EPS) * w


def _store_chunk_unpermuted(val, stage_ref, dst_ref):
    rows, width = val.shape
    for i in range(width // LANES):
        stage_ref[i] = val[:, i * LANES:(i + 1) * LANES]
    for i in range(width // LANES):
        for c in range(rows // CHUNK):
            for j in range(SUBLANES):
                dst_ref[c * CHUNK + j * TILES:c * CHUNK + (j + 1) * TILES, i * LANES:(i + 1) * LANES] = (
                    stage_ref[i, pl.ds(c * CHUNK + j, TILES, stride=SUBLANES), :])


def _stage_scratch(rows, width):
    return pltpu.VMEM((width // LANES, rows, LANES), F32)


def _causal_mask():
    def pos(dim):
        r = lax.broadcasted_iota(jnp.int32, (CHUNK, CHUNK), dim)
        return (r % SUBLANES) * TILES + r // SUBLANES
    return pos(0) >= pos(1)


def _permute_rows_kernel(x_ref, o_ref):
    for c in range(x_ref.shape[0] // CHUNK):
        for k in range(TILES):
            o_ref[c * CHUNK + k * SUBLANES:c * CHUNK + (k + 1) * SUBLANES, :] = (
                x_ref[pl.ds(c * CHUNK + k, SUBLANES, stride=TILES), :])


def _permute_rows(x, rows):
    m, d = x.shape
    return pl.pallas_call(
        _permute_rows_kernel,
        grid=(m // rows, d // LANES),
        in_specs=[pl.BlockSpec((rows, LANES), lambda i, j: (i, j))],
        out_specs=pl.BlockSpec((rows, LANES), lambda i, j: (i, j)),
        out_shape=jax.ShapeDtypeStruct((m, d), x.dtype),
        compiler_params=_params("parallel", "parallel"),
        name="permute_rows",
    )(x)


def _prep_w_in_kernel(w_ref, wm_ref, wdt_ref):
    for dst, src, width in ((COL_XBC, SRC_XBC, SSD_XBC), (COL_LRU_X, SRC_LRU_X, LRU_WIDTH),
                            (COL_ML_X, SRC_ML_X, ML_WIDTH), (COL_Z, SRC_Z, SSD_WIDTH),
                            (COL_LRU_GATE, SRC_LRU_GATE, LRU_WIDTH), (COL_ML_O, SRC_ML_O, ML_WIDTH)):
        wm_ref[dst:dst + width, :] = w_ref[src:src + width, :].astype(BF16)
    wdt_ref[0:SSD_HEADS, :] = w_ref[SRC_DT:SRC_DT + SSD_HEADS, :].astype(BF16)
    wdt_ref[SSD_HEADS:, :] = jnp.zeros((LANES - SSD_HEADS, wdt_ref.shape[1]), BF16)


def _prep_w_in(w_in, cols):
    depth, d, _ = w_in.shape
    return pl.pallas_call(
        _prep_w_in_kernel,
        grid=(depth, d // cols),
        in_specs=[pl.BlockSpec((None, D_IN, cols), lambda l, i: (l, 0, i))],
        out_specs=[pl.BlockSpec((None, D_PROJ, cols), lambda l, i: (l, 0, i)),
                   pl.BlockSpec((None, LANES, cols), lambda l, i: (l, 0, i))],
        out_shape=[jax.ShapeDtypeStruct((depth, D_PROJ, d), BF16),
                   jax.ShapeDtypeStruct((depth, LANES, d), BF16)],
        compiler_params=_params("parallel", "parallel"),
        name="prep_w_in",
    )(jnp.swapaxes(w_in, 1, 2))


def _norm_matmul_kernel(x_ref, nw_ref, w_ref, wdt_ref, o_ref, dt_ref, u_ref):
    @pl.when(pl.program_id(1) == 0)
    def _():
        u = _rmsnorm(x_ref[...], nw_ref[...]).astype(BF16)
        u_ref[...] = u
        dt_ref[...] = _dot_nt(u, wdt_ref[...])

    o_ref[...] = _dot_nt(u_ref[...], w_ref[...])


def _norm_matmul(x, nw, w, wdt, layer, tm, tn):
    m, k = x.shape
    n = w.shape[1]
    return pl.pallas_call(
        _norm_matmul_kernel,
        grid=(m // tm, n // tn),
        in_specs=[pl.BlockSpec((tm, k), lambda i, j: (i, 0)),
                  pl.BlockSpec((1, k), lambda i, j: (0, 0)),
                  pl.BlockSpec((None, tn, k), lambda i, j: (layer, j, 0)),
                  pl.BlockSpec((None, LANES, k), lambda i, j: (layer, 0, 0))],
        out_specs=[pl.BlockSpec((tm, tn), lambda i, j: (i, j)),
                   pl.BlockSpec((tm, LANES), lambda i, j: (i, 0))],
        out_shape=[jax.ShapeDtypeStruct((m, n), F32), jax.ShapeDtypeStruct((m, LANES), F32)],
        scratch_shapes=[pltpu.VMEM((tm, k), BF16)],
        compiler_params=_params("parallel", "arbitrary"),
        name="norm_in_proj",
    )(x, nw, w, wdt)


def _mlstm_body(mx, mc, o_pre_ref, causal, wq_ref, wk_ref, wv_ref, wif_ref, bif_ref, nw_ref,
                c_ins, n_ins, m_ins):
    mcb = mc.astype(BF16)
    mxb = mx.astype(BF16)
    qs, ks, vs = [], [], []
    for h in range(ML_HEADS):
        hs = slice(h * ML_HEAD_DIM, (h + 1) * ML_HEAD_DIM)
        qs.append(_dot(mcb[:, hs], wq_ref[h]))
        ks.append(_dot(mcb[:, hs], wk_ref[h]))
        vs.append(_dot(mxb[:, hs], wv_ref[h]))

    gates = bif_ref[...]
    for part, vals in enumerate((qs, ks, vs)):
        for h in range(ML_HEADS):
            r0 = part * ML_WIDTH + h * ML_HEAD_DIM
            gates = gates + _dot(vals[h].astype(BF16), wif_ref[r0:r0 + ML_HEAD_DIM, :])
    i_pre = gates[:, :LANES] * LOG2E
    log_f = -_softplus(-gates[:, LANES:])

    bcum = _dot_exact_lhs01(causal.astype(BF16), log_f) * LOG2E
    bcum_t = bcum.T
    i_t = i_pre.T
    g_tot = bcum[CHUNK - 1:CHUNK, :]
    w_end = g_tot - bcum + i_pre

    ys, c_new, n_new, m_news = [], [], [], []
    for h in range(ML_HEADS):
        hs = slice(h * ML_HEAD_DIM, (h + 1) * ML_HEAD_DIM)
        q = qs[h]
        qb = q.astype(BF16)
        kf = ks[h] * ML_KSCALE
        kb = kf.astype(BF16)
        v = vs[h]
        c_in, n_in, m_in = c_ins[h], n_ins[h], m_ins[h]

        bc = bcum[:, h:h + 1]
        dmat = jnp.where(causal, bc - bcum_t[h:h + 1, :] + i_t[h:h + 1, :], -jnp.inf)
        inter_log = bc + m_in
        m_t = jnp.maximum(jnp.max(dmat, axis=1, keepdims=True), inter_log)
        p_intra = jnp.exp2(dmat - m_t)
        s_inter = jnp.exp2(inter_log - m_t)
        qk = _dot_nt(qb, kb) * p_intra
        num = _dot(qk.astype(BF16), v.astype(BF16)) + s_inter * _dot(qb, c_in.astype(BF16))
        den = (jnp.sum(qk, axis=1, keepdims=True)
               + s_inter * jnp.sum(q * n_in, axis=1, keepdims=True))
        out = num / jnp.maximum(jnp.abs(den), jnp.exp2(-m_t))

        mu = jnp.mean(out, axis=1, keepdims=True)
        cen = out - mu
        var = jnp.mean(cen * cen, axis=1, keepdims=True)
        hn = cen * lax.rsqrt(var + EPS) * nw_ref[:, hs]
        ys.append((_sigmoid(o_pre_ref[:, hs]) * hn).astype(BF16))

        we = w_end[:, h:h + 1]
        m_loc = jnp.max(we, axis=0, keepdims=True)
        p_end = jnp.exp2(we - m_loc)
        c_loc = _dot_tn(kb, (v * p_end).astype(BF16))
        n_loc = jnp.sum(kf * p_end, axis=0, keepdims=True)
        g_h = g_tot[:, h:h + 1]
        m_new = jnp.maximum(g_h + m_in, m_loc)
        s_old = jnp.exp2(g_h + m_in - m_new)
        s_loc = jnp.exp2(m_loc - m_new)
        c_new.append(s_old * c_in + s_loc * c_loc)
        n_new.append(s_old * n_in + s_loc * n_loc)
        m_news.append(m_new)
    return ys, c_new, n_new, m_news


def _lru_gates(xc, wa_ref, ba_ref, wx_ref, bx_ref, lam_ref):
    xcb = xc.astype(BF16)
    ra, ix = [], []
    for q in range(LRU_WIDTH // MXU_DIM):
        blk = xcb[:, q * MXU_DIM:(q + 1) * MXU_DIM]
        ra.append(_dot(blk, wa_ref[q]))
        ix.append(_dot(blk, wx_ref[q]))
    r = _sigmoid(jnp.concatenate(ra, axis=1) + ba_ref[...])
    i = _sigmoid(jnp.concatenate(ix, axis=1) + bx_ref[...])
    log_a = (-LRU_C * r) * _softplus(-lam_ref[...])
    a = jnp.exp(log_a)
    u = jnp.sqrt(1.0 - jnp.exp(2.0 * log_a)) * (i * xc)
    return a, u


def _lru_scan(a, u, h_carry):
    def tile(x, k):
        return x[k * SUBLANES:(k + 1) * SUBLANES, :]

    decay, local = [tile(a, 0)], [tile(u, 0)]
    for k in range(1, TILES):
        a_k = tile(a, k)
        local.append(a_k * local[-1] + tile(u, k))
        decay.append(a_k * decay[-1])
    p, q = decay[-1], local[-1]
    sub = lax.broadcasted_iota(jnp.int32, p.shape, 0)
    d = 1
    while d < SUBLANES:
        p_prev = jnp.where(sub >= d, pltpu.roll(p, d, axis=0), 1.0)
        q_prev = jnp.where(sub >= d, pltpu.roll(q, d, axis=0), 0.0)
        q = p * q_prev + q
        p = p * p_prev
        d *= 2
    h_prev = jnp.broadcast_to(h_carry, p.shape)
    run_end = q + p * h_prev
    run_in = jnp.where(sub == 0, h_prev, pltpu.roll(run_end, 1, axis=0))
    hs = jnp.concatenate([local[k] + decay[k] * run_in for k in range(TILES)], axis=0)
    return hs, run_end[SUBLANES - 1:SUBLANES, :]


def _mixer_kernel(proj_ref, dt_ref, cw_ref, cb_ref,
                  dtb_ref, alog_ref, dsk_ref, snw_ref, e_ref,
                  wa_ref, ba_ref, wx_ref, bx_ref, lam_ref,
                  wq_ref, wk_ref, wv_ref, wif_ref, bif_ref, mnw_ref,
                  y_ref,
                  hist_ref, state_ref, h_ref, c_ref, n_ref, m_ref):
    @pl.when(pl.program_id(1) == 0)
    def _():
        for ref in (hist_ref, state_ref, h_ref, c_ref, n_ref, m_ref):
            ref[...] = jnp.zeros(ref.shape, F32)

    causal = _causal_mask()
    conv_in = proj_ref[:, :CONV_WIDTH]
    ssd_states = [state_ref[g] for g in range(SSD_GROUPS)]
    c_ins = [c_ref[h] for h in range(ML_HEADS)]
    n_ins = [n_ref[h][0:1, :] for h in range(ML_HEADS)]
    m_ins = [m_ref[h][0:1, 0:1] for h in range(ML_HEADS)]
    h_carry = h_ref[0:1, :]
    conv, new_hist = _causal_conv(conv_in, hist_ref[...], cw_ref, cb_ref)

    y_ml, c_new, n_new, m_new = _mlstm_body(
        conv_in[:, COL_ML_X:COL_ML_X + ML_WIDTH], _silu(conv[:, COL_ML_X:COL_ML_X + ML_WIDTH]),
        proj_ref.at[:, COL_ML_O:COL_ML_O + ML_WIDTH], causal, wq_ref, wk_ref, wv_ref, wif_ref, bif_ref,
        mnw_ref, c_ins, n_ins, m_ins)
    y_ssd, new_states = _ssd_body(
        _silu(conv[:, COL_XBC:COL_XBC + SSD_XBC]), proj_ref[:, COL_Z:COL_Z + SSD_WIDTH], dt_ref[...],
        causal, dtb_ref, alog_ref, dsk_ref, snw_ref, e_ref, ssd_states)
    a, u = _lru_gates(conv[:, COL_LRU_X:COL_LRU_X + LRU_WIDTH], wa_ref, ba_ref, wx_ref, bx_ref, lam_ref)
    hs, h_last = _lru_scan(a, u, h_carry)
    gate = proj_ref[:, COL_LRU_GATE:COL_LRU_GATE + LRU_WIDTH]
    y_lru = (hs * jax.nn.gelu(gate, approximate=True)).astype(BF16)

    y_ref[:, OUT_SSD:OUT_SSD + SSD_WIDTH] = y_ssd
    y_ref[:, OUT_LRU:OUT_LRU + LRU_WIDTH] = y_lru
    hist_ref[...] = new_hist
    h_ref[0:1, :] = h_last
    for g in range(SSD_GROUPS):
        state_ref[g] = new_states[g]
    for h in range(ML_HEADS):
        y_ref[:, OUT_ML + h * ML_HEAD_DIM:OUT_ML + (h + 1) * ML_HEAD_DIM] = y_ml[h]
        c_ref[h] = c_new[h]
        n_ref[h] = _splat(n_new[h], (SUBLANES, ML_HEAD_DIM))
        m_ref[h] = _splat(m_new[h], (SUBLANES, LANES))


def _conv_columns(proj_ref, hist_ref, w_ref, b_ref, lo, width):
    cur = proj_ref[:, lo:lo + width]
    sub = lax.broadcasted_iota(jnp.int32, (SUBLANES, width), 0)
    wrapped = []
    for i in range(CONV_K - 1):
        r0 = CHUNK - HIST + i * SUBLANES
        from_cur = pltpu.roll(cur[r0:r0 + SUBLANES, :], 1, axis=0)
        from_prev = pltpu.roll(hist_ref[i * SUBLANES:(i + 1) * SUBLANES, lo:lo + width], 1, axis=0)
        wrapped.append(jnp.where(sub == 0, from_prev, from_cur))
    acc = cur * w_ref[CONV_K - 1:CONV_K, lo:lo + width] + b_ref[:, lo:lo + width]
    for s in range(1, CONV_K):
        shifted = jnp.concatenate(wrapped[CONV_K - 1 - s:] + [cur[0:CHUNK - s * SUBLANES, :]], axis=0)
        acc = acc + shifted * w_ref[CONV_K - 1 - s:CONV_K - s, lo:lo + width]
    return acc, cur


def _interleave(tasks):
    tasks = list(tasks)
    while tasks:
        for t in list(tasks):
            try:
                next(t)
            except StopIteration:
                tasks.remove(t)


def _ssd_prepare(dt_raw, causal01, dtb_ref, alog_ref):
    dt = _softplus(dt_raw + dtb_ref[...])
    da = dt * (-jnp.exp(alog_ref[...]))
    cs = _dot_exact_lhs01(causal01, da) * LOG2E
    return _split3(dt), _split3(cs), cs, cs.T


def _ssd_group(g, conv, proj_ref, prep, causal, dsk_ref, nw_ref, e_ref, state, out):
    dt_parts, cs_parts, cs, cs_t = prep
    gs = slice(g * SSD_GROUP_WIDTH, (g + 1) * SSD_GROUP_WIDTH)
    xs = _silu(conv(COL_XBC + gs.start, SSD_GROUP_WIDTH)[0])
    b_g = _silu(conv(COL_XBC + SSD_WIDTH + g * SSD_STATE, SSD_STATE)[0]).astype(BF16)
    c_g = _silu(conv(COL_XBC + SSD_WIDTH + SSD_BC + g * SSD_STATE, SSD_STATE)[0]).astype(BF16)
    scores = _dot_nt(c_g, b_g)
    expand = e_ref[:, gs]
    dt_e = (_dot(dt_parts[0], expand) + _dot(dt_parts[1], expand)) + _dot(dt_parts[2], expand)
    cs_e = (_dot(cs_parts[0], expand) + _dot(cs_parts[1], expand)) + _dot(cs_parts[2], expand)
    yield
    cs_last_e = cs_e[CHUNK - 1:CHUNK, :]
    xd = xs * dt_e
    head_of_lane = lax.broadcasted_iota(jnp.int32, (CHUNK, SSD_GROUP_WIDTH), 1) // SSD_HEAD_DIM
    lhs, rhs = [], []
    for j in range(SSD_HEADS_PER_GROUP):
        h = g * SSD_HEADS_PER_GROUP + j
        seg = cs[:, h:h + 1] - cs_t[h:h + 1, :]
        decay = jnp.exp2(jnp.where(causal, seg, -jnp.inf))
        lhs.append((scores * decay).astype(BF16))
        rhs.append(jnp.where(head_of_lane == j, xd, 0.0).astype(BF16))
    y_diag = _dot(jnp.concatenate(lhs, axis=1), jnp.concatenate(rhs, axis=0))
    y_off = _dot(c_g, state.astype(BF16))
    xd_to_end = (xd * jnp.exp2(cs_last_e - cs_e)).astype(BF16)
    new_state = state * jnp.exp2(cs_last_e) + _dot_tn(b_g, xd_to_end)
    yield
    y = y_diag + y_off * jnp.exp2(cs_e) + dsk_ref[:, gs] * xs
    y = y * _silu(proj_ref[:, COL_Z + gs.start:COL_Z + gs.stop])
    yn = y * lax.rsqrt(jnp.mean(y * y, axis=-1, keepdims=True) + EPS)
    out[g] = ((yn * nw_ref[:, gs]).astype(BF16), new_state)


def _mlstm_prepare(conv, causal01, wq_ref, wk_ref, wv_ref, wif_ref, bif_ref):
    qs, ks, vs = [], [], []
    for h in range(ML_HEADS):
        mc, mx = conv(COL_ML_X + h * ML_HEAD_DIM, ML_HEAD_DIM)
        mcb = _silu(mc).astype(BF16)
        qs.append(_dot(mcb, wq_ref[h]))
        ks.append(_dot(mcb, wk_ref[h]))
        vs.append(_dot(mx.astype(BF16), wv_ref[h]))
    gates = bif_ref[...]
    for part, vals in enumerate((qs, ks, vs)):
        for h in range(ML_HEADS):
            r0 = part * ML_WIDTH + h * ML_HEAD_DIM
            gates = gates + _dot(vals[h].astype(BF16), wif_ref[r0:r0 + ML_HEAD_DIM, :])
    i_pre = gates[:, :LANES] * LOG2E
    log_f = -_softplus(-gates[:, LANES:])
    bcum = _dot_exact_lhs01(causal01, log_f) * LOG2E
    g_tot = bcum[CHUNK - 1:CHUNK, :]
    return qs, ks, vs, bcum, bcum.T, i_pre.T, g_tot, g_tot - bcum + i_pre


def _mlstm_head(h, prep, proj_ref, causal, nw_ref, c_in, n_in, m_in, out):
    qs, ks, vs, bcum, bcum_t, i_t, g_tot, w_end = prep
    hs = slice(h * ML_HEAD_DIM, (h + 1) * ML_HEAD_DIM)
    q, v = qs[h], vs[h]
    qb = q.astype(BF16)
    kf = ks[h] * ML_KSCALE
    kb = kf.astype(BF16)
    bc = bcum[:, h:h + 1]
    dmat = jnp.where(causal, bc - bcum_t[h:h + 1, :] + i_t[h:h + 1, :], -jnp.inf)
    inter_log = bc + m_in
    m_t = jnp.maximum(jnp.max(dmat, axis=1, keepdims=True), inter_log)
    qk_raw = _dot_nt(qb, kb)
    inter = _dot(qb, c_in.astype(BF16))
    we = w_end[:, h:h + 1]
    m_loc = jnp.max(we, axis=0, keepdims=True)
    yield
    p_end = jnp.exp2(we - m_loc)
    c_loc = _dot_tn(kb, (v * p_end).astype(BF16))
    n_loc = jnp.sum(kf * p_end, axis=0, keepdims=True)
    s_inter = jnp.exp2(inter_log - m_t)
    qk = qk_raw * jnp.exp2(dmat - m_t)
    num = _dot(qk.astype(BF16), v.astype(BF16)) + s_inter * inter
    den = (jnp.sum(qk, axis=1, keepdims=True)
           + s_inter * jnp.sum(q * n_in, axis=1, keepdims=True))
    yield
    o = num / jnp.maximum(jnp.abs(den), jnp.exp2(-m_t))
    mu = jnp.mean(o, axis=1, keepdims=True)
    cen = o - mu
    var = jnp.mean(cen * cen, axis=1, keepdims=True)
    yield
    hn = cen * lax.rsqrt(var + EPS) * nw_ref[:, hs]
    y = (_sigmoid(proj_ref[:, COL_ML_O + hs.start:COL_ML_O + hs.stop]) * hn).astype(BF16)
    g_h = g_tot[:, h:h + 1]
    m_new = jnp.maximum(g_h + m_in, m_loc)
    s_old = jnp.exp2(g_h + m_in - m_new)
    s_loc = jnp.exp2(m_loc - m_new)
    out[h] = (y, s_old * c_in + s_loc * c_loc, s_old * n_in + s_loc * n_loc, m_new)


def _lru_scan(a, u, h_carry):
    def tile(x, k):
        return x[k * SUBLANES:(k + 1) * SUBLANES, :]

    decay, local = [tile(a, 0)], [tile(u, 0)]
    for k in range(1, TILES):
        a_k = tile(a, k)
        local.append(a_k * local[-1] + tile(u, k))
        decay.append(a_k * decay[-1])
    p, q = decay[-1], local[-1]
    sub = lax.broadcasted_iota(jnp.int32, p.shape, 0)
    d = 1
    while d < SUBLANES:
        p_prev = jnp.where(sub >= d, pltpu.roll(p, d, axis=0), 1.0)
        q_prev = jnp.where(sub >= d, pltpu.roll(q, d, axis=0), 0.0)
        q = p * q_prev + q
        p = p * p_prev
        d *= 2
    h_prev = jnp.broadcast_to(h_carry, p.shape)
    run_end = q + p * h_prev
    run_in = jnp.where(sub == 0, h_prev, pltpu.roll(run_end, 1, axis=0))
    hs = jnp.concatenate([local[k] + decay[k] * run_in for k in range(TILES)], axis=0)
    return hs, run_end[SUBLANES - 1:SUBLANES, :]


def _lru_tile(t, conv, proj_ref, wa_ref, ba_ref, wx_ref, bx_ref, lam_ref, h_carry, out):
    ts = slice(t * MXU_DIM, (t + 1) * MXU_DIM)
    xc = conv(COL_LRU_X + ts.start, MXU_DIM)[0]
    xcb = xc.astype(BF16)
    ra = _dot(xcb, wa_ref[t])
    ix = _dot(xcb, wx_ref[t])
    yield
    r = _sigmoid(ra + ba_ref[:, ts])
    i = _sigmoid(ix + bx_ref[:, ts])
    log_a = (-LRU_C * r) * _softplus(-lam_ref[:, ts])
    a = jnp.exp(log_a)
    u = jnp.sqrt(1.0 - jnp.exp(2.0 * log_a)) * (i * xc)
    yield
    hs, h_last = _lru_scan(a, u, h_carry)
    gate = proj_ref[:, COL_LRU_GATE + ts.start:COL_LRU_GATE + ts.stop]
    out[t] = ((hs * jax.nn.gelu(gate, approximate=True)).astype(BF16), h_last)


def _splat(x11, shape):
    return jnp.broadcast_to(x11, shape)


def _mixer_kernel(proj_ref, dt_ref, cw_ref, cb_ref,
                  dtb_ref, alog_ref, dsk_ref, snw_ref, e_ref,
                  wa_ref, ba_ref, wx_ref, bx_ref, lam_ref,
                  wq_ref, wk_ref, wv_ref, wif_ref, bif_ref, mnw_ref,
                  y_ref,
                  hist_ref, state_ref, h_ref, c_ref, n_ref, m_ref):
    @pl.when(pl.program_id(1) == 0)
    def _():
        for ref in (hist_ref, state_ref, h_ref, c_ref, n_ref, m_ref):
            ref[...] = jnp.zeros(ref.shape, F32)

    causal = _causal_mask()
    causal01 = causal.astype(BF16)
    conv = functools.partial(_conv_columns, proj_ref, hist_ref, cw_ref, cb_ref)
    n_lru = LRU_WIDTH // MXU_DIM

    ml_out, ssd_out, lru_out = {}, {}, {}
    ml_states = [(c_ref[h], n_ref[h][0:1, :], m_ref[h][0:1, 0:1]) for h in range(ML_HEADS)]
    ssd_prep = _ssd_prepare(dt_ref[...], causal01, dtb_ref, alog_ref)
    others = [_ssd_group(g, conv, proj_ref, ssd_prep, causal, dsk_ref, snw_ref, e_ref, state_ref[g], ssd_out)
              for g in range(SSD_GROUPS)]
    others += [_lru_tile(t, conv, proj_ref, wa_ref, ba_ref, wx_ref, bx_ref, lam_ref,
                         h_ref[0:1, t * MXU_DIM:(t + 1) * MXU_DIM], lru_out) for t in range(n_lru)]
    ml_prep = _mlstm_prepare(conv, causal01, wq_ref, wk_ref, wv_ref, wif_ref, bif_ref)
    for t in others:
        next(t)
    heads = [_mlstm_head(h, ml_prep, proj_ref, causal, mnw_ref, *ml_states[h], ml_out)
             for h in range(ML_HEADS)]
    _interleave(heads + others)

    hist_ref[...] = proj_ref[CHUNK - HIST:, :CONV_WIDTH]
    for g in range(SSD_GROUPS):
        y, new_state = ssd_out[g]
        y_ref[:, OUT_SSD + g * SSD_GROUP_WIDTH:OUT_SSD + (g + 1) * SSD_GROUP_WIDTH] = y
        state_ref[g] = new_state
    for t in range(n_lru):
        y, h_last = lru_out[t]
        y_ref[:, OUT_LRU + t * MXU_DIM:OUT_LRU + (t + 1) * MXU_DIM] = y
        h_ref[0:1, t * MXU_DIM:(t + 1) * MXU_DIM] = h_last
    for h in range(ML_HEADS):
        y, c_new, n_new, m_new = ml_out[h]
        y_ref[:, OUT_ML + h * ML_HEAD_DIM:OUT_ML + (h + 1) * ML_HEAD_DIM] = y
        c_ref[h] = c_new
        n_ref[h] = _splat(n_new, (SUBLANES, ML_HEAD_DIM))
        m_ref[h] = _splat(m_new, (SUBLANES, LANES))


def _mixers(proj, dt_raw, p):
    b, s, _ = proj.shape
    pad = LANES - SSD_HEADS
    head = jnp.arange(LANES)[:, None]
    expand = (head == (jnp.arange(SSD_WIDTH)[None, :] // SSD_HEAD_DIM)).astype(BF16)
    w_if, b_if = _pad_gate_weights(p["ml_w_if"], p["ml_b_if"])
    operands = [
        jnp.concatenate([p["ssd_conv_w"], p["lru_conv_w"], p["ml_conv_w"]], axis=1),
        jnp.concatenate([p["ssd_conv_b"], p["lru_conv_b"], p["ml_conv_b"]])[None, :],
        jnp.pad(p["ssd_dt_bias"], (0, pad))[None, :],
        jnp.pad(p["ssd_a_log"], (0, pad))[None, :],
        jnp.repeat(p["ssd_d"], SSD_HEAD_DIM)[None, :],
        p["ssd_norm_w"][None, :],
        expand,
        _block_diag_tiles(p["lru_w_a"], MXU_DIM), p["lru_b_a"][None, :],
        _block_diag_tiles(p["lru_w_x"], MXU_DIM), p["lru_b_x"][None, :],
        p["lru_lambda"][None, :],
        _block_diag_tiles(p["ml_w_q"], ML_HEAD_DIM),
        _block_diag_tiles(p["ml_w_k"], ML_HEAD_DIM),
        _block_diag_tiles(p["ml_w_v"], ML_HEAD_DIM),
        w_if.astype(BF16), b_if, p["ml_norm_w"][None, :],
    ]

    def const(a):
        return pl.BlockSpec(a.shape, lambda i, c, nd=a.ndim: (0,) * nd)

    return pl.pallas_call(
        _mixer_kernel,
        grid=(b, s // CHUNK),
        in_specs=[pl.BlockSpec((None, CHUNK, D_PROJ), lambda i, c: (i, c, 0)),
                  pl.BlockSpec((None, CHUNK, LANES), lambda i, c: (i, c, 0))]
                 + [const(a) for a in operands],
        out_specs=pl.BlockSpec((None, CHUNK, D_MIX), lambda i, c: (i, c, 0)),
        out_shape=jax.ShapeDtypeStruct((b, s, D_MIX), BF16),
        scratch_shapes=[pltpu.VMEM((HIST, CONV_WIDTH), F32),
                        pltpu.VMEM((SSD_GROUPS, SSD_STATE, SSD_GROUP_WIDTH), F32),
                        pltpu.VMEM((SUBLANES, LRU_WIDTH), F32),
                        pltpu.VMEM((ML_HEADS, ML_HEAD_DIM, ML_HEAD_DIM), F32),
                        pltpu.VMEM((ML_HEADS, SUBLANES, ML_HEAD_DIM), F32),
                        pltpu.VMEM((ML_HEADS, SUBLANES, LANES), F32)],
        compiler_params=_params("parallel", "arbitrary"),
        name="mixers",
    )(proj, dt_raw, *operands)


def _out_proj_kernel(h_ref, y_ref, w_ref, wgu32_ref, wd32_ref, o_ref, wgu_ref, wd_ref):
    o_ref[...] = h_ref[...] + _dot(y_ref[...], w_ref[...])

    wgu_ref[...] = wgu32_ref[...].astype(BF16)

    @pl.when(pl.program_id(0) == 0)
    def _():
        wd_ref[...] = wd32_ref[...].astype(BF16)


def _out_proj(h, y, w_out, w_gate_up, w_down, layer, tm, tn):
    m, n = h.shape
    k = y.shape[1]
    nj, ni = n // tn, m // tm
    _, d, f2 = w_gate_up.shape
    f = w_down.shape[1]
    bf16_rows = 2 * SUBLANES
    assert d % (bf16_rows * ni * nj) == 0 and f % (bf16_rows * ni) == 0, (d, f, ni, nj)

    def down_block(j, i):
        return jnp.where(j == 0, i, ni - 1)

    return pl.pallas_call(
        _out_proj_kernel,
        grid=(nj, ni),
        in_specs=[pl.BlockSpec((tm, tn), lambda j, i: (i, j)),
                  pl.BlockSpec((tm, k), lambda j, i: (i, 0)),
                  pl.BlockSpec((None, k, tn), lambda j, i: (layer, 0, j)),
                  pl.BlockSpec((None, d // (ni * nj), f2), lambda j, i: (layer, j * ni + i, 0)),
                  pl.BlockSpec((None, f // ni, d), lambda j, i: (layer, down_block(j, i), 0))],
        out_specs=[pl.BlockSpec((tm, tn), lambda j, i: (i, j)),
                   pl.BlockSpec((d // (ni * nj), f2), lambda j, i: (j * ni + i, 0)),
                   pl.BlockSpec((f // ni, d), lambda j, i: (down_block(j, i), 0))],
        out_shape=[jax.ShapeDtypeStruct((m, n), F32),
                   jax.ShapeDtypeStruct((d, f2), BF16),
                   jax.ShapeDtypeStruct((f, d), BF16)],
        compiler_params=_params("arbitrary", "arbitrary"),
        name="out_proj",
    )(h, y, w_out, w_gate_up, w_down)


def _ffn_kernel(h_ref, nw_ref, wg_ref, wu_ref, wd_ref, fw_ref, o_ref, u_ref, acc_ref, *stage, last_layer):
    j = pl.program_id(1)

    @pl.when(j == 0)
    def _():
        u_ref[...] = _rmsnorm(h_ref[...], nw_ref[...]).astype(BF16)
        acc_ref[...] = jnp.zeros(acc_ref.shape, F32)

    u = u_ref[...]
    gate = _dot(u, wg_ref[...])
    up = _dot(u, wu_ref[...])
    acc_ref[...] += _dot((_silu(gate) * up).astype(BF16), wd_ref[...])

    @pl.when(j == pl.num_programs(1) - 1)
    def _():
        out = h_ref[...] + acc_ref[...]
        if last_layer:
            _store_chunk_unpermuted(_rmsnorm(out, fw_ref[...]), *stage, o_ref)
        else:
            o_ref[...] = out


def _ffn(h, nw, w_gate_up, w_down, fw, tm, tf, last_layer):
    m, d = h.shape
    f = w_down.shape[0]
    nf = f // tf
    return pl.pallas_call(
        functools.partial(_ffn_kernel, last_layer=last_layer),
        grid=(m // tm, nf),
        in_specs=[pl.BlockSpec((tm, d), lambda i, j: (i, 0)),
                  pl.BlockSpec((1, d), lambda i, j: (0, 0)),
                  pl.BlockSpec((d, tf), lambda i, j: (0, j)),
                  pl.BlockSpec((d, tf), lambda i, j: (0, j + nf)),
                  pl.BlockSpec((tf, d), lambda i, j: (j, 0)),
                  pl.BlockSpec((1, d), lambda i, j: (0, 0))],
        out_specs=pl.BlockSpec((tm, d), lambda i, j: (i, 0)),
        out_shape=jax.ShapeDtypeStruct((m, d), F32),
        scratch_shapes=([pltpu.VMEM((tm, d), BF16), pltpu.VMEM((tm, d), F32)]
                        + ([_stage_scratch(tm, d)] if last_layer else [])),
        compiler_params=_params("parallel", "arbitrary"),
        name="ffn_last" if last_layer else "ffn",
    )(h, nw, w_gate_up, w_gate_up, w_down, fw)


def _block_diag_tiles(w, tile):
    nblk, c, _ = w.shape
    col = jnp.arange(tile)
    spread = (jnp.arange(c)[:, None] == (col % c)[None, :]).astype(w.dtype)
    same_block = ((col // c)[:, None] == (col // c)[None, :]).astype(w.dtype)
    rep = jnp.dot(w.reshape(nblk * c, c), spread, precision=lax.Precision.HIGHEST)
    return (rep.reshape(nblk * c // tile, tile, tile) * same_block).astype(BF16)


def _pad_gate_weights(w_if, b_if):
    zw = jnp.zeros((w_if.shape[0], LANES - ML_HEADS), w_if.dtype)
    w = jnp.concatenate([w_if[:, :ML_HEADS], zw, w_if[:, ML_HEADS:], zw], axis=1)
    zb = jnp.zeros((LANES - ML_HEADS,), b_if.dtype)
    b = jnp.concatenate([b_if[:ML_HEADS], zb, b_if[ML_HEADS:], zb])[None, :]
    return w, b


def _tile(total, want):
    return want if total % want == 0 else total


def kernel(x, norm1_w, w_in, ssd_conv_w, ssd_conv_b, ssd_dt_bias, ssd_a_log, ssd_d, ssd_norm_w, lru_conv_w, lru_conv_b, lru_w_a, lru_b_a, lru_w_x, lru_b_x, lru_lambda, ml_conv_w, ml_conv_b, ml_w_q, ml_w_k, ml_w_v, ml_w_if, ml_b_if, ml_norm_w, w_out, norm2_w, w_gate_up, w_down, norm_f_w):
    b, s, d = x.shape
    m = b * s
    depth = w_in.shape[0]
    mixer_params = dict(
        ssd_conv_w=ssd_conv_w, ssd_conv_b=ssd_conv_b, ssd_dt_bias=ssd_dt_bias, ssd_a_log=ssd_a_log,
        ssd_d=ssd_d, ssd_norm_w=ssd_norm_w, lru_conv_w=lru_conv_w, lru_conv_b=lru_conv_b,
        lru_w_a=lru_w_a, lru_b_a=lru_b_a, lru_w_x=lru_w_x, lru_b_x=lru_b_x, lru_lambda=lru_lambda,
        ml_conv_w=ml_conv_w, ml_conv_b=ml_conv_b, ml_w_q=ml_w_q, ml_w_k=ml_w_k, ml_w_v=ml_w_v,
        ml_w_if=ml_w_if, ml_b_if=ml_b_if, ml_norm_w=ml_norm_w)
    w_main, w_dt = _prep_w_in(w_in, 256)
    w_out_b = w_out.astype(BF16)
    tm = _tile(m, 512)
    h = _permute_rows(x.reshape(m, d), _tile(m, 4096))
    for l in range(depth):
        proj, dt_raw = _norm_matmul(h, norm1_w[l][None, :], w_main, w_dt, l, _tile(m, 1024), D_PROJ // 8)
        y = _mixers(proj.reshape(b, s, D_PROJ), dt_raw.reshape(b, s, LANES),
                    {k: v[l] for k, v in mixer_params.items()})
        h, w_gate_up_b, w_down_b = _out_proj(h, y.reshape(m, D_MIX), w_out_b, w_gate_up, w_down, l, tm, 1024)
        h = _ffn(h, norm2_w[l][None, :], w_gate_up_b, w_down_b, norm_f_w[None, :], tm, 512,
                 last_layer=(l == depth - 1))
    return h.reshape(b, s, d)
```

```python
import functools

import jax
import jax.numpy as jnp
from jax import lax
from jax.experimental import pallas as pl
from jax.experimental.pallas import tpu as pltpu

F32 = jnp.float32
BF16 = jnp.bfloat16

EPS = 1e-6
LOG2E = 1.4426950408889634
CONV_K = 4
CHUNK = 128
SUBLANES = 8
LANES = 128
MXU_DIM = 256
TILES = CHUNK // SUBLANES
HIST = (CONV_K - 1) * SUBLANES

SSD_WIDTH = 2048
SSD_HEAD_DIM = 64
SSD_HEADS = 32
SSD_GROUPS = 8
SSD_STATE = 128
SSD_BC = SSD_GROUPS * SSD_STATE
SSD_XBC = SSD_WIDTH + 2 * SSD_BC
SSD_GROUP_WIDTH = SSD_WIDTH // SSD_GROUPS
SSD_HEADS_PER_GROUP = SSD_HEADS // SSD_GROUPS
LRU_WIDTH = 1024
LRU_C = 8.0
ML_WIDTH = 1024
ML_HEADS = 4
ML_HEAD_DIM = 256
ML_KSCALE = ML_HEAD_DIM ** -0.5
D_MIX = SSD_WIDTH + LRU_WIDTH + ML_WIDTH

COL_XBC = 0
COL_LRU_X = COL_XBC + SSD_XBC
COL_ML_X = COL_LRU_X + LRU_WIDTH
CONV_WIDTH = COL_ML_X + ML_WIDTH
COL_Z = CONV_WIDTH
COL_LRU_GATE = COL_Z + SSD_WIDTH
COL_ML_O = COL_LRU_GATE + LRU_WIDTH
D_PROJ = COL_ML_O + ML_WIDTH
SRC_Z = 0
SRC_XBC = SRC_Z + SSD_WIDTH
SRC_DT = SRC_XBC + SSD_XBC
SRC_LRU_GATE = SRC_DT + SSD_HEADS
SRC_LRU_X = SRC_LRU_GATE + LRU_WIDTH
SRC_ML_X = SRC_LRU_X + LRU_WIDTH
SRC_ML_O = SRC_ML_X + ML_WIDTH
D_IN = SRC_ML_O + ML_WIDTH
OUT_SSD = 0
OUT_LRU = OUT_SSD + SSD_WIDTH
OUT_ML = OUT_LRU + LRU_WIDTH

VMEM_LIMIT_BYTES = 56 * 1024 * 1024


def _params(*semantics):
    return pltpu.CompilerParams(dimension_semantics=semantics, vmem_limit_bytes=VMEM_LIMIT_BYTES)


def _dot(a, b):
    return jnp.dot(a, b, preferred_element_type=F32)


def _dot_nt(a, b):
    return lax.dot_general(a, b, (((1,), (1,)), ((), ())), preferred_element_type=F32)


def _dot_tn(a, b):
    return lax.dot_general(a, b, (((0,), (0,)), ((), ())), preferred_element_type=F32)


def _split3(x):
    hi = x.astype(BF16)
    r1 = x - hi.astype(F32)
    mid = r1.astype(BF16)
    lo = (r1 - mid.astype(F32)).astype(BF16)
    return hi, mid, lo


def _dot_exact_lhs01(m01, x):
    hi, mid, lo = _split3(x)
    return (_dot(m01, hi) + _dot(m01, mid)) + _dot(m01, lo)


def _sigmoid(x):
    return 1.0 / (1.0 + jnp.exp2(x * (-LOG2E)))


def _silu(x):
    return x * _sigmoid(x)


def _softplus(x):
    return jnp.maximum(x, 0.0) + jnp.log1p(jnp.exp(-jnp.abs(x)))


def _rmsnorm(x, w):
    return x * lax.rsqrt(jnp.mean(x * x, axis=-1, keepdims=True) + EPS) * w


def _store_chunk_unpermuted(val, stage_ref, dst_ref):
    rows, width = val.shape
    for i in range(width // LANES):
        stage_ref[i] = val[:, i * LANES:(i + 1) * LANES]
    for i in range(width // LANES):
        for c in range(rows // CHUNK):
            for j in range(SUBLANES):
                dst_ref[c * CHUNK + j * TILES:c * CHUNK + (j + 1) * TILES, i * LANES:(i + 1) * LANES] = (
                    stage_ref[i, pl.ds(c * CHUNK + j, TILES, stride=SUBLANES), :])


def _stage_scratch(rows, width):
    return pltpu.VMEM((width // LANES, rows, LANES), F32)


def _causal_mask():
    def pos(dim):
        r = lax.broadcasted_iota(jnp.int32, (CHUNK, CHUNK), dim)
        return (r % SUBLANES) * TILES + r // SUBLANES
    return pos(0) >= pos(1)


def _permute_rows_kernel(x_ref, o_ref):
    for c in range(x_ref.shape[0] // CHUNK):
        for k in range(TILES):
            o_ref[c * CHUNK + k * SUBLANES:c * CHUNK + (k + 1) * SUBLANES, :] = (
                x_ref[pl.ds(c * CHUNK + k, SUBLANES, stride=TILES), :])


def _permute_rows(x, rows):
    m, d = x.shape
    return pl.pallas_call(
        _permute_rows_kernel,
        grid=(m // rows, d // LANES),
        in_specs=[pl.BlockSpec((rows, LANES), lambda i, j: (i, j))],
        out_specs=pl.BlockSpec((rows, LANES), lambda i, j: (i, j)),
        out_shape=jax.ShapeDtypeStruct((m, d), x.dtype),
        compiler_params=_params("parallel", "parallel"),
        name="permute_rows",
    )(x)


def _prep_w_in_kernel(w_ref, wm_ref, wdt_ref):
    for dst, src, width in ((COL_XBC, SRC_XBC, SSD_XBC), (COL_LRU_X, SRC_LRU_X, LRU_WIDTH),
                            (COL_ML_X, SRC_ML_X, ML_WIDTH), (COL_Z, SRC_Z, SSD_WIDTH),
                            (COL_LRU_GATE, SRC_LRU_GATE, LRU_WIDTH), (COL_ML_O, SRC_ML_O, ML_WIDTH)):
        wm_ref[dst:dst + width, :] = w_ref[src:src + width, :].astype(BF16)
    wdt_ref[0:SSD_HEADS, :] = w_ref[SRC_DT:SRC_DT + SSD_HEADS, :].astype(BF16)
    wdt_ref[SSD_HEADS:, :] = jnp.zeros((LANES - SSD_HEADS, wdt_ref.shape[1]), BF16)


def _prep_w_in(w_in, cols):
    depth, d, _ = w_in.shape
    return pl.pallas_call(
        _prep_w_in_kernel,
        grid=(depth, d // cols),
        in_specs=[pl.BlockSpec((None, D_IN, cols), lambda l, i: (l, 0, i))],
        out_specs=[pl.BlockSpec((None, D_PROJ, cols), lambda l, i: (l, 0, i)),
                   pl.BlockSpec((None, LANES, cols), lambda l, i: (l, 0, i))],
        out_shape=[jax.ShapeDtypeStruct((depth, D_PROJ, d), BF16),
                   jax.ShapeDtypeStruct((depth, LANES, d), BF16)],
        compiler_params=_params("parallel", "parallel"),
        name="prep_w_in",
    )(jnp.swapaxes(w_in, 1, 2))


def _norm_matmul_kernel(x_ref, nw_ref, w_ref, wdt_ref, wout32_ref, o_ref, dt_ref, wout_ref, u_ref):
    @pl.when(pl.program_id(1) == 0)
    def _():
        u = _rmsnorm(x_ref[...], nw_ref[...]).astype(BF16)
        u_ref[...] = u
        dt_ref[...] = _dot_nt(u, wdt_ref[...])

    o_ref[...] = _dot_nt(u_ref[...], w_ref[...])
    wout_ref[...] = wout32_ref[...].astype(BF16)


def _norm_matmul(x, nw, w, wdt, w_out, layer, tm, tn):
    m, k = x.shape
    n = w.shape[1]
    ni, nj = m // tm, n // tn
    _, ko, no = w_out.shape
    rows = ko // (ni * nj)
    assert rows * ni * nj == ko and rows % (2 * SUBLANES) == 0, (ko, ni, nj)
    return pl.pallas_call(
        _norm_matmul_kernel,
        grid=(ni, nj),
        in_specs=[pl.BlockSpec((tm, k), lambda i, j: (i, 0)),
                  pl.BlockSpec((1, k), lambda i, j: (0, 0)),
                  pl.BlockSpec((None, tn, k), lambda i, j: (layer, j, 0)),
                  pl.BlockSpec((None, LANES, k), lambda i, j: (layer, 0, 0)),
                  pl.BlockSpec((None, rows, no), lambda i, j: (layer, i * nj + j, 0))],
        out_specs=[pl.BlockSpec((tm, tn), lambda i, j: (i, j)),
                   pl.BlockSpec((tm, LANES), lambda i, j: (i, 0)),
                   pl.BlockSpec((rows, no), lambda i, j: (i * nj + j, 0))],
        out_shape=[jax.ShapeDtypeStruct((m, n), F32), jax.ShapeDtypeStruct((m, LANES), F32),
                   jax.ShapeDtypeStruct((ko, no), BF16)],
        scratch_shapes=[pltpu.VMEM((tm, k), BF16)],
        compiler_params=_params("arbitrary", "arbitrary"),
        name="norm_in_proj",
    )(x, nw, w, wdt, w_out)


def _conv_columns(proj_ref, hist_ref, w_ref, b_ref, lo, width):
    cur = proj_ref[:, lo:lo + width]
    sub = lax.broadcasted_iota(jnp.int32, (SUBLANES, width), 0)
    wrapped = []
    for i in range(CONV_K - 1):
        r0 = CHUNK - HIST + i * SUBLANES
        from_cur = pltpu.roll(cur[r0:r0 + SUBLANES, :], 1, axis=0)
        from_prev = pltpu.roll(hist_ref[i * SUBLANES:(i + 1) * SUBLANES, lo:lo + width], 1, axis=0)
        wrapped.append(jnp.where(sub == 0, from_prev, from_cur))
    acc = cur * w_ref[CONV_K - 1:CONV_K, lo:lo + width] + b_ref[:, lo:lo + width]
    for s in range(1, CONV_K):
        shifted = jnp.concatenate(wrapped[CONV_K - 1 - s:] + [cur[0:CHUNK - s * SUBLANES, :]], axis=0)
        acc = acc + shifted * w_ref[CONV_K - 1 - s:CONV_K - s, lo:lo + width]
    return acc, cur


def _interleave(tasks):
    tasks = list(tasks)
    while tasks:
        for t in list(tasks):
            try:
                next(t)
            except StopIteration:
                tasks.remove(t)


def _ssd_prepare(dt_raw, causal01, dtb_ref, alog_ref):
    dt = _softplus(dt_raw + dtb_ref[...])
    da = dt * (-jnp.exp(alog_ref[...]))
    cs = _dot_exact_lhs01(causal01, da) * LOG2E
    return _split3(dt), _split3(cs), cs, cs.T


def _ssd_group(g, conv, proj_ref, prep, causal, dsk_ref, nw_ref, e_ref, state, out):
    dt_parts, cs_parts, cs, cs_t = prep
    gs = slice(g * SSD_GROUP_WIDTH, (g + 1) * SSD_GROUP_WIDTH)
    xs = _silu(conv(COL_XBC + gs.start, SSD_GROUP_WIDTH)[0])
    b_g = _silu(conv(COL_XBC + SSD_WIDTH + g * SSD_STATE, SSD_STATE)[0]).astype(BF16)
    c_g = _silu(conv(COL_XBC + SSD_WIDTH + SSD_BC + g * SSD_STATE, SSD_STATE)[0]).astype(BF16)
    scores = _dot_nt(c_g, b_g)
    expand = e_ref[:, gs]
    dt_e = (_dot(dt_parts[0], expand) + _dot(dt_parts[1], expand)) + _dot(dt_parts[2], expand)
    cs_e = (_dot(cs_parts[0], expand) + _dot(cs_parts[1], expand)) + _dot(cs_parts[2], expand)
    yield
    cs_last_e = cs_e[CHUNK - 1:CHUNK, :]
    xd = xs * dt_e
    head_of_lane = lax.broadcasted_iota(jnp.int32, (CHUNK, SSD_GROUP_WIDTH), 1) // SSD_HEAD_DIM
    lhs, rhs = [], []
    for j in range(SSD_HEADS_PER_GROUP):
        h = g * SSD_HEADS_PER_GROUP + j
        seg = cs[:, h:h + 1] - cs_t[h:h + 1, :]
        decay = jnp.exp2(jnp.where(causal, seg, -jnp.inf))
        lhs.append((scores * decay).astype(BF16))
        rhs.append(jnp.where(head_of_lane == j, xd, 0.0).astype(BF16))
    y_diag = _dot(jnp.concatenate(lhs, axis=1), jnp.concatenate(rhs, axis=0))
    y_off = _dot(c_g, state.astype(BF16))
    xd_to_end = (xd * jnp.exp2(cs_last_e - cs_e)).astype(BF16)
    new_state = state * jnp.exp2(cs_last_e) + _dot_tn(b_g, xd_to_end)
    yield
    y = y_diag + y_off * jnp.exp2(cs_e) + dsk_ref[:, gs] * xs
    y = y * _silu(proj_ref[:, COL_Z + gs.start:COL_Z + gs.stop])
    yn = y * lax.rsqrt(jnp.mean(y * y, axis=-1, keepdims=True) + EPS)
    out[g] = ((yn * nw_ref[:, gs]).astype(BF16), new_state)


def _mlstm_prepare(conv, causal01, wq_ref, wk_ref, wv_ref, wif_ref, bif_ref):
    qs, ks, vs = [], [], []
    for h in range(ML_HEADS):
        mc, mx = conv(COL_ML_X + h * ML_HEAD_DIM, ML_HEAD_DIM)
        mcb = _silu(mc).astype(BF16)
        qs.append(_dot(mcb, wq_ref[h]))
        ks.append(_dot(mcb, wk_ref[h]))
        vs.append(_dot(mx.astype(BF16), wv_ref[h]))
    gates = bif_ref[...]
    for part, vals in enumerate((qs, ks, vs)):
        for h in range(ML_HEADS):
            r0 = part * ML_WIDTH + h * ML_HEAD_DIM
            gates = gates + _dot(vals[h].astype(BF16), wif_ref[r0:r0 + ML_HEAD_DIM, :])
    i_pre = gates[:, :LANES] * LOG2E
    log_f = -_softplus(-gates[:, LANES:])
    bcum = _dot_exact_lhs01(causal01, log_f) * LOG2E
    g_tot = bcum[CHUNK - 1:CHUNK, :]
    return qs, ks, vs, bcum, bcum.T, i_pre.T, g_tot, g_tot - bcum + i_pre


def _mlstm_head(h, prep, proj_ref, causal, nw_ref, c_in, n_in, m_in, out):
    qs, ks, vs, bcum, bcum_t, i_t, g_tot, w_end = prep
    hs = slice(h * ML_HEAD_DIM, (h + 1) * ML_HEAD_DIM)
    q, v = qs[h], vs[h]
    qb = q.astype(BF16)
    kf = ks[h] * ML_KSCALE
    kb = kf.astype(BF16)
    bc = bcum[:, h:h + 1]
    dmat = jnp.where(causal, bc - bcum_t[h:h + 1, :] + i_t[h:h + 1, :], -jnp.inf)
    inter_log = bc + m_in
    m_t = jnp.maximum(jnp.max(dmat, axis=1, keepdims=True), inter_log)
    qk_raw = _dot_nt(qb, kb)
    inter = _dot(qb, c_in.astype(BF16))
    we = w_end[:, h:h + 1]
    m_loc = jnp.max(we, axis=0, keepdims=True)
    yield
    p_end = jnp.exp2(we - m_loc)
    c_loc = _dot_tn(kb, (v * p_end).astype(BF16))
    n_loc = jnp.sum(kf * p_end, axis=0, keepdims=True)
    s_inter = jnp.exp2(inter_log - m_t)
    qk = qk_raw * jnp.exp2(dmat - m_t)
    num = _dot(qk.astype(BF16), v.astype(BF16)) + s_inter * inter
    den = (jnp.sum(qk, axis=1, keepdims=True)
           + s_inter * jnp.sum(q * n_in, axis=1, keepdims=True))
    yield
    o = num / jnp.maximum(jnp.abs(den), jnp.exp2(-m_t))
    mu = jnp.mean(o, axis=1, keepdims=True)
    cen = o - mu
    var = jnp.mean(cen * cen, axis=1, keepdims=True)
    yield
    hn = cen * lax.rsqrt(var + EPS) * nw_ref[:, hs]
    y = (_sigmoid(proj_ref[:, COL_ML_O + hs.start:COL_ML_O + hs.stop]) * hn).astype(BF16)
    g_h = g_tot[:, h:h + 1]
    m_new = jnp.maximum(g_h + m_in, m_loc)
    s_old = jnp.exp2(g_h + m_in - m_new)
    s_loc = jnp.exp2(m_loc - m_new)
    out[h] = (y, s_old * c_in + s_loc * c_loc, s_old * n_in + s_loc * n_loc, m_new)


def _lru_scan(a, u, h_carry):
    def tile(x, k):
        return x[k * SUBLANES:(k + 1) * SUBLANES, :]

    decay, local = [tile(a, 0)], [tile(u, 0)]
    for k in range(1, TILES):
        a_k = tile(a, k)
        local.append(a_k * local[-1] + tile(u, k))
        decay.append(a_k * decay[-1])
    p, q = decay[-1], local[-1]
    sub = lax.broadcasted_iota(jnp.int32, p.shape, 0)
    d = 1
    while d < SUBLANES:
        p_prev = jnp.where(sub >= d, pltpu.roll(p, d, axis=0), 1.0)
        q_prev = jnp.where(sub >= d, pltpu.roll(q, d, axis=0), 0.0)
        q = p * q_prev + q
        p = p * p_prev
        d *= 2
    h_prev = jnp.broadcast_to(h_carry, p.shape)
    run_end = q + p * h_prev
    run_in = jnp.where(sub == 0, h_prev, pltpu.roll(run_end, 1, axis=0))
    hs = jnp.concatenate([local[k] + decay[k] * run_in for k in range(TILES)], axis=0)
    return hs, run_end[SUBLANES - 1:SUBLANES, :]


def _lru_tile(t, conv, proj_ref, wa_ref, ba_ref, wx_ref, bx_ref, lam_ref, h_carry, out):
    ts = slice(t * MXU_DIM, (t + 1) * MXU_DIM)
    xc = conv(COL_LRU_X + ts.start, MXU_DIM)[0]
    xcb = xc.astype(BF16)
    ra = _dot(xcb, wa_ref[t])
    ix = _dot(xcb, wx_ref[t])
    yield
    r = _sigmoid(ra + ba_ref[:, ts])
    i = _sigmoid(ix + bx_ref[:, ts])
    log_a = (-LRU_C * r) * _softplus(-lam_ref[:, ts])
    a = jnp.exp(log_a)
    u = jnp.sqrt(1.0 - jnp.exp(2.0 * log_a)) * (i * xc)
    yield
    hs, h_last = _lru_scan(a, u, h_carry)
    gate = proj_ref[:, COL_LRU_GATE + ts.start:COL_LRU_GATE + ts.stop]
    out[t] = ((hs * jax.nn.gelu(gate, approximate=True)).astype(BF16), h_last)


def _splat(x11, shape):
    return jnp.broadcast_to(x11, shape)


def _mixer_kernel(proj_ref, dt_ref, cw_ref, cb_ref,
                  dtb_ref, alog_ref, dsk_ref, snw_ref, e_ref,
                  wa_ref, ba_ref, wx_ref, bx_ref, lam_ref,
                  wq_ref, wk_ref, wv_ref, wif_ref, bif_ref, mnw_ref,
                  y_ref,
                  hist_ref, state_ref, h_ref, c_ref, n_ref, m_ref):
    @pl.when(pl.program_id(1) == 0)
    def _():
        for ref in (hist_ref, state_ref, h_ref, c_ref, n_ref, m_ref):
            ref[...] = jnp.zeros(ref.shape, F32)

    causal = _causal_mask()
    causal01 = causal.astype(BF16)
    conv = functools.partial(_conv_columns, proj_ref, hist_ref, cw_ref, cb_ref)
    n_lru = LRU_WIDTH // MXU_DIM

    ml_out, ssd_out, lru_out = {}, {}, {}
    ml_states = [(c_ref[h], n_ref[h][0:1, :], m_ref[h][0:1, 0:1]) for h in range(ML_HEADS)]
    ssd_prep = _ssd_prepare(dt_ref[...], causal01, dtb_ref, alog_ref)
    ssd = [_ssd_group(g, conv, proj_ref, ssd_prep, causal, dsk_ref, snw_ref, e_ref, state_ref[g], ssd_out)
           for g in range(SSD_GROUPS)]
    lru = [_lru_tile(t, conv, proj_ref, wa_ref, ba_ref, wx_ref, bx_ref, lam_ref,
                     h_ref[0:1, t * MXU_DIM:(t + 1) * MXU_DIM], lru_out) for t in range(n_lru)]
    ml_prep = _mlstm_prepare(conv, causal01, wq_ref, wk_ref, wv_ref, wif_ref, bif_ref)
    for t in ssd + lru:
        next(t)
    heads = [_mlstm_head(h, ml_prep, proj_ref, causal, mnw_ref, *ml_states[h], ml_out)
             for h in range(ML_HEADS)]
    per_head = SSD_GROUPS // ML_HEADS
    order = []
    for h in range(ML_HEADS):
        order += [heads[h]] + ssd[h * per_head:(h + 1) * per_head] + lru[h:h + 1]
    _interleave(order)

    hist_ref[...] = proj_ref[CHUNK - HIST:, :CONV_WIDTH]
    for g in range(SSD_GROUPS):
        y, new_state = ssd_out[g]
        y_ref[:, OUT_SSD + g * SSD_GROUP_WIDTH:OUT_SSD + (g + 1) * SSD_GROUP_WIDTH] = y
        state_ref[g] = new_state
    for t in range(n_lru):
        y, h_last = lru_out[t]
        y_ref[:, OUT_LRU + t * MXU_DIM:OUT_LRU + (t + 1) * MXU_DIM] = y
        h_ref[0:1, t * MXU_DIM:(t + 1) * MXU_DIM] = h_last
    for h in range(ML_HEADS):
        y, c_new, n_new, m_new = ml_out[h]
        y_ref[:, OUT_ML + h * ML_HEAD_DIM:OUT_ML + (h + 1) * ML_HEAD_DIM] = y
        c_ref[h] = c_new
        n_ref[h] = _splat(n_new, (SUBLANES, ML_HEAD_DIM))
        m_ref[h] = _splat(m_new, (SUBLANES, LANES))


def _mixers(proj, dt_raw, p):
    b, s, _ = proj.shape
    pad = LANES - SSD_HEADS
    head = jnp.arange(LANES)[:, None]
    expand = (head == (jnp.arange(SSD_WIDTH)[None, :] // SSD_HEAD_DIM)).astype(BF16)
    w_if, b_if = _pad_gate_weights(p["ml_w_if"], p["ml_b_if"])
    operands = [
        jnp.concatenate([p["ssd_conv_w"], p["lru_conv_w"], p["ml_conv_w"]], axis=1),
        jnp.concatenate([p["ssd_conv_b"], p["lru_conv_b"], p["ml_conv_b"]])[None, :],
        jnp.pad(p["ssd_dt_bias"], (0, pad))[None, :],
        jnp.pad(p["ssd_a_log"], (0, pad))[None, :],
        jnp.repeat(p["ssd_d"], SSD_HEAD_DIM)[None, :],
        p["ssd_norm_w"][None, :],
        expand,
        _block_diag_tiles(p["lru_w_a"], MXU_DIM), p["lru_b_a"][None, :],
        _block_diag_tiles(p["lru_w_x"], MXU_DIM), p["lru_b_x"][None, :],
        p["lru_lambda"][None, :],
        _block_diag_tiles(p["ml_w_q"], ML_HEAD_DIM),
        _block_diag_tiles(p["ml_w_k"], ML_HEAD_DIM),
        _block_diag_tiles(p["ml_w_v"], ML_HEAD_DIM),
        w_if.astype(BF16), b_if, p["ml_norm_w"][None, :],
    ]

    def const(a):
        return pl.BlockSpec(a.shape, lambda i, c, nd=a.ndim: (0,) * nd)

    return pl.pallas_call(
        _mixer_kernel,
        grid=(b, s // CHUNK),
        in_specs=[pl.BlockSpec((None, CHUNK, D_PROJ), lambda i, c: (i, c, 0)),
                  pl.BlockSpec((None, CHUNK, LANES), lambda i, c: (i, c, 0))]
                 + [const(a) for a in operands],
        out_specs=pl.BlockSpec((None, CHUNK, D_MIX), lambda i, c: (i, c, 0)),
        out_shape=jax.ShapeDtypeStruct((b, s, D_MIX), BF16),
        scratch_shapes=[pltpu.VMEM((HIST, CONV_WIDTH), F32),
                        pltpu.VMEM((SSD_GROUPS, SSD_STATE, SSD_GROUP_WIDTH), F32),
                        pltpu.VMEM((SUBLANES, LRU_WIDTH), F32),
                        pltpu.VMEM((ML_HEADS, ML_HEAD_DIM, ML_HEAD_DIM), F32),
                        pltpu.VMEM((ML_HEADS, SUBLANES, ML_HEAD_DIM), F32),
                        pltpu.VMEM((ML_HEADS, SUBLANES, LANES), F32)],
        compiler_params=_params("parallel", "arbitrary"),
        name="mixers",
    )(proj, dt_raw, *operands)


def _out_proj_kernel(h_ref, y_ref, w_ref, wgu32_ref, wd32_ref, o_ref, wgu_ref, wd_ref):
    o_ref[...] = h_ref[...] + _dot(y_ref[...], w_ref[...])

    wgu_ref[...] = wgu32_ref[...].astype(BF16)

    @pl.when(pl.program_id(0) == 0)
    def _():
        wd_ref[...] = wd32_ref[...].astype(BF16)


def _out_proj(h, y, w_out, w_gate_up, w_down, layer, tm, tn):
    m, n = h.shape
    k = y.shape[1]
    nj, ni = n // tn, m // tm
    _, d, f2 = w_gate_up.shape
    f = w_down.shape[1]
    bf16_rows = 2 * SUBLANES
    assert d % (bf16_rows * ni * nj) == 0 and f % (bf16_rows * ni) == 0, (d, f, ni, nj)

    def down_block(j, i):
        return jnp.where(j == 0, i, ni - 1)

    return pl.pallas_call(
        _out_proj_kernel,
        grid=(nj, ni),
        in_specs=[pl.BlockSpec((tm, tn), lambda j, i: (i, j)),
                  pl.BlockSpec((tm, k), lambda j, i: (i, 0)),
                  pl.BlockSpec((k, tn), lambda j, i: (0, j)),
                  pl.BlockSpec((None, d // (ni * nj), f2), lambda j, i: (layer, j * ni + i, 0)),
                  pl.BlockSpec((None, f // ni, d), lambda j, i: (layer, down_block(j, i), 0))],
        out_specs=[pl.BlockSpec((tm, tn), lambda j, i: (i, j)),
                   pl.BlockSpec((d // (ni * nj), f2), lambda j, i: (j * ni + i, 0)),
                   pl.BlockSpec((f // ni, d), lambda j, i: (down_block(j, i), 0))],
        out_shape=[jax.ShapeDtypeStruct((m, n), F32),
                   jax.ShapeDtypeStruct((d, f2), BF16),
                   jax.ShapeDtypeStruct((f, d), BF16)],
        compiler_params=_params("arbitrary", "arbitrary"),
        name="out_proj",
    )(h, y, w_out, w_gate_up, w_down)


def _ffn_kernel(h_ref, nw_ref, wg_ref, wu_ref, wd_ref, fw_ref, o_ref, u_ref, acc_ref, *stage, last_layer):
    j = pl.program_id(1)

    @pl.when(j == 0)
    def _():
        u_ref[...] = _rmsnorm(h_ref[...], nw_ref[...]).astype(BF16)
        acc_ref[...] = jnp.zeros(acc_ref.shape, F32)

    u = u_ref[...]
    gate = _dot(u, wg_ref[...])
    up = _dot(u, wu_ref[...])
    acc_ref[...] += _dot((_silu(gate) * up).astype(BF16), wd_ref[...])

    @pl.when(j == pl.num_programs(1) - 1)
    def _():
        out = h_ref[...] + acc_ref[...]
        if last_layer:
            _store_chunk_unpermuted(_rmsnorm(out, fw_ref[...]), *stage, o_ref)
        else:
            o_ref[...] = out


def _ffn(h, nw, w_gate_up, w_down, fw, tm, tf, last_layer):
    m, d = h.shape
    f = w_down.shape[0]
    nf = f // tf
    return pl.pallas_call(
        functools.partial(_ffn_kernel, last_layer=last_layer),
        grid=(m // tm, nf),
        in_specs=[pl.BlockSpec((tm, d), lambda i, j: (i, 0)),
                  pl.BlockSpec((1, d), lambda i, j: (0, 0)),
                  pl.BlockSpec((d, tf), lambda i, j: (0, j)),
                  pl.BlockSpec((d, tf), lambda i, j: (0, j + nf)),
                  pl.BlockSpec((tf, d), lambda i, j: (j, 0)),
                  pl.BlockSpec((1, d), lambda i, j: (0, 0))],
        out_specs=pl.BlockSpec((tm, d), lambda i, j: (i, 0)),
        out_shape=jax.ShapeDtypeStruct((m, d), F32),
        scratch_shapes=([pltpu.VMEM((tm, d), BF16), pltpu.VMEM((tm, d), F32)]
                        + ([_stage_scratch(tm, d)] if last_layer else [])),
        compiler_params=_params("parallel", "arbitrary"),
        name="ffn_last" if last_layer else "ffn",
    )(h, nw, w_gate_up, w_gate_up, w_down, fw)


def _block_diag_tiles(w, tile):
    nblk, c, _ = w.shape
    col = jnp.arange(tile)
    spread = (jnp.arange(c)[:, None] == (col % c)[None, :]).astype(w.dtype)
    same_block = ((col // c)[:, None] == (col // c)[None, :]).astype(w.dtype)
    rep = jnp.dot(w.reshape(nblk * c, c), spread, precision=lax.Precision.HIGHEST)
    return (rep.reshape(nblk * c // tile, tile, tile) * same_block).astype(BF16)


def _pad_gate_weights(w_if, b_if):
    zw = jnp.zeros((w_if.shape[0], LANES - ML_HEADS), w_if.dtype)
    w = jnp.concatenate([w_if[:, :ML_HEADS], zw, w_if[:, ML_HEADS:], zw], axis=1)
    zb = jnp.zeros((LANES - ML_HEADS,), b_if.dtype)
    b = jnp.concatenate([b_if[:ML_HEADS], zb, b_if[ML_HEADS:], zb])[None, :]
    return w, b


IN_PROJ_ROWS = 1024
IN_PROJ_COLS = D_PROJ // 8
ROW_TILE = 512
OUT_PROJ_COLS = 1024
FFN_COLS = 512
PREP_COLS = 256
PERMUTE_ROWS = 4096


def _tile(total, want):
    return want if total % want == 0 else total


def kernel(x, norm1_w, w_in, ssd_conv_w, ssd_conv_b, ssd_dt_bias, ssd_a_log, ssd_d, ssd_norm_w, lru_conv_w, lru_conv_b, lru_w_a, lru_b_a, lru_w_x, lru_b_x, lru_lambda, ml_conv_w, ml_conv_b, ml_w_q, ml_w_k, ml_w_v, ml_w_if, ml_b_if, ml_norm_w, w_out, norm2_w, w_gate_up, w_down, norm_f_w):
    b, s, d = x.shape
    m = b * s
    depth = w_in.shape[0]
    mixer_params = dict(
        ssd_conv_w=ssd_conv_w, ssd_conv_b=ssd_conv_b, ssd_dt_bias=ssd_dt_bias, ssd_a_log=ssd_a_log,
        ssd_d=ssd_d, ssd_norm_w=ssd_norm_w, lru_conv_w=lru_conv_w, lru_conv_b=lru_conv_b,
        lru_w_a=lru_w_a, lru_b_a=lru_b_a, lru_w_x=lru_w_x, lru_b_x=lru_b_x, lru_lambda=lru_lambda,
        ml_conv_w=ml_conv_w, ml_conv_b=ml_conv_b, ml_w_q=ml_w_q, ml_w_k=ml_w_k, ml_w_v=ml_w_v,
        ml_w_if=ml_w_if, ml_b_if=ml_b_if, ml_norm_w=ml_norm_w)
    w_main, w_dt = _prep_w_in(w_in, PREP_COLS)
    tm = _tile(m, ROW_TILE)
    h = _permute_rows(x.reshape(m, d), _tile(m, PERMUTE_ROWS))
    for l in range(depth):
        proj, dt_raw, w_out_b = _norm_matmul(h, norm1_w[l][None, :], w_main, w_dt, w_out, l,
                                             _tile(m, IN_PROJ_ROWS), IN_PROJ_COLS)
        y = _mixers(proj.reshape(b, s, D_PROJ), dt_raw.reshape(b, s, LANES),
                    {k: v[l] for k, v in mixer_params.items()})
        h, w_gate_up_b, w_down_b = _out_proj(h, y.reshape(m, D_MIX), w_out_b, w_gate_up, w_down, l, tm,
                                             OUT_PROJ_COLS)
        h = _ffn(h, norm2_w[l][None, :], w_gate_up_b, w_down_b, norm_f_w[None, :], tm, FFN_COLS,
                 last_layer=(l == depth - 1))
    return h.reshape(b, s, d)
```

```python
import functools

import jax
import jax.numpy as jnp
from jax import lax
from jax.experimental import pallas as pl
from jax.experimental.pallas import tpu as pltpu

F32 = jnp.float32
BF16 = jnp.bfloat16

EPS = 1e-6
LOG2E = 1.4426950408889634
CONV_K = 4
CHUNK = 128
SUBLANES = 8
LANES = 128
MXU_DIM = 256
TILES = CHUNK // SUBLANES
HIST = (CONV_K - 1) * SUBLANES

SSD_WIDTH = 2048
SSD_HEAD_DIM = 64
SSD_HEADS = 32
SSD_GROUPS = 8
SSD_STATE = 128
SSD_BC = SSD_GROUPS * SSD_STATE
SSD_XBC = SSD_WIDTH + 2 * SSD_BC
SSD_GROUP_WIDTH = SSD_WIDTH // SSD_GROUPS
SSD_HEADS_PER_GROUP = SSD_HEADS // SSD_GROUPS
LRU_WIDTH = 1024
LRU_C = 8.0
ML_WIDTH = 1024
ML_HEADS = 4
ML_HEAD_DIM = 256
ML_KSCALE = ML_HEAD_DIM ** -0.5
D_MIX = SSD_WIDTH + LRU_WIDTH + ML_WIDTH

COL_XBC = 0
COL_LRU_X = COL_XBC + SSD_XBC
COL_ML_X = COL_LRU_X + LRU_WIDTH
CONV_WIDTH = COL_ML_X + ML_WIDTH
COL_Z = CONV_WIDTH
COL_LRU_GATE = COL_Z + SSD_WIDTH
COL_ML_O = COL_LRU_GATE + LRU_WIDTH
D_PROJ = COL_ML_O + ML_WIDTH
SRC_Z = 0
SRC_XBC = SRC_Z + SSD_WIDTH
SRC_DT = SRC_XBC + SSD_XBC
SRC_LRU_GATE = SRC_DT + SSD_HEADS
SRC_LRU_X = SRC_LRU_GATE + LRU_WIDTH
SRC_ML_X = SRC_LRU_X + LRU_WIDTH
SRC_ML_O = SRC_ML_X + ML_WIDTH
D_IN = SRC_ML_O + ML_WIDTH
OUT_SSD = 0
OUT_LRU = OUT_SSD + SSD_WIDTH
OUT_ML = OUT_LRU + LRU_WIDTH

VMEM_LIMIT_BYTES = 56 * 1024 * 1024


def _params(*semantics):
    return pltpu.CompilerParams(dimension_semantics=semantics, vmem_limit_bytes=VMEM_LIMIT_BYTES)


def _dot(a, b):
    return jnp.dot(a, b, preferred_element_type=F32)


def _dot_nt(a, b):
    return lax.dot_general(a, b, (((1,), (1,)), ((), ())), preferred_element_type=F32)


def _dot_tn(a, b):
    return lax.dot_general(a, b, (((0,), (0,)), ((), ())), preferred_element_type=F32)


def _split3(x):
    hi = x.astype(BF16)
    r1 = x - hi.astype(F32)
    mid = r1.astype(BF16)
    lo = (r1 - mid.astype(F32)).astype(BF16)
    return hi, mid, lo


def _dot_exact_lhs01(m01, x):
    hi, mid, lo = _split3(x)
    return (_dot(m01, hi) + _dot(m01, mid)) + _dot(m01, lo)


def _sigmoid(x):
    return 0.5 * jnp.tanh(0.5 * x) + 0.5


def _silu(x):
    half = 0.5 * x
    return half * jnp.tanh(half) + half


def _softplus(x):
    return jnp.maximum(x, 0.0) + jnp.log1p(jnp.exp(-jnp.abs(x)))


def _rmsnorm(x, w):
    return x * lax.rsqrt(jnp.mean(x * x, axis=-1, keepdims=True) + EPS) * w


def _store_chunk_unpermuted(val, stage_ref, dst_ref):
    rows, width = val.shape
    for i in range(width // LANES):
        stage_ref[i] = val[:, i * LANES:(i + 1) * LANES]
    for i in range(width // LANES):
        for c in range(rows // CHUNK):
            for j in range(SUBLANES):
                dst_ref[c * CHUNK + j * TILES:c * CHUNK + (j + 1) * TILES, i * LANES:(i + 1) * LANES] = (
                    stage_ref[i, pl.ds(c * CHUNK + j, TILES, stride=SUBLANES), :])


def _stage_scratch(rows, width):
    return pltpu.VMEM((width // LANES, rows, LANES), F32)


def _causal_mask():
    def pos(dim):
        r = lax.broadcasted_iota(jnp.int32, (CHUNK, CHUNK), dim)
        return (r % SUBLANES) * TILES + r // SUBLANES
    return pos(0) >= pos(1)


def _permute_rows_kernel(x_ref, o_ref):
    for c in range(x_ref.shape[0] // CHUNK):
        for k in range(TILES):
            o_ref[c * CHUNK + k * SUBLANES:c * CHUNK + (k + 1) * SUBLANES, :] = (
                x_ref[pl.ds(c * CHUNK + k, SUBLANES, stride=TILES), :])


def _permute_rows(x, rows):
    m, d = x.shape
    return pl.pallas_call(
        _permute_rows_kernel,
        grid=(m // rows, d // LANES),
        in_specs=[pl.BlockSpec((rows, LANES), lambda i, j: (i, j))],
        out_specs=pl.BlockSpec((rows, LANES), lambda i, j: (i, j)),
        out_shape=jax.ShapeDtypeStruct((m, d), x.dtype),
        compiler_params=_params("parallel", "parallel"),
        name="permute_rows",
    )(x)


def _prep_w_in_kernel(w_ref, wm_ref, wdt_ref):
    for dst, src, width in ((COL_XBC, SRC_XBC, SSD_XBC), (COL_LRU_X, SRC_LRU_X, LRU_WIDTH),
                            (COL_ML_X, SRC_ML_X, ML_WIDTH), (COL_Z, SRC_Z, SSD_WIDTH),
                            (COL_LRU_GATE, SRC_LRU_GATE, LRU_WIDTH), (COL_ML_O, SRC_ML_O, ML_WIDTH)):
        wm_ref[dst:dst + width, :] = w_ref[src:src + width, :].astype(BF16)
    wdt_ref[0:SSD_HEADS, :] = w_ref[SRC_DT:SRC_DT + SSD_HEADS, :].astype(BF16)
    wdt_ref[SSD_HEADS:, :] = jnp.zeros((LANES - SSD_HEADS, wdt_ref.shape[1]), BF16)


def _prep_w_in(w_in, cols):
    depth, d, _ = w_in.shape
    return pl.pallas_call(
        _prep_w_in_kernel,
        grid=(depth, d // cols),
        in_specs=[pl.BlockSpec((None, D_IN, cols), lambda l, i: (l, 0, i))],
        out_specs=[pl.BlockSpec((None, D_PROJ, cols), lambda l, i: (l, 0, i)),
                   pl.BlockSpec((None, LANES, cols), lambda l, i: (l, 0, i))],
        out_shape=[jax.ShapeDtypeStruct((depth, D_PROJ, d), BF16),
                   jax.ShapeDtypeStruct((depth, LANES, d), BF16)],
        compiler_params=_params("parallel", "parallel"),
        name="prep_w_in",
    )(jnp.swapaxes(w_in, 1, 2))


def _norm_matmul_kernel(x_ref, nw_ref, w_ref, wdt_ref, wout32_ref, o_ref, dt_ref, wout_ref, u_ref):
    @pl.when(pl.program_id(1) == 0)
    def _():
        u = _rmsnorm(x_ref[...], nw_ref[...]).astype(BF16)
        u_ref[...] = u
        dt_ref[...] = _dot_nt(u, wdt_ref[...])

    o_ref[...] = _dot_nt(u_ref[...], w_ref[...])
    wout_ref[...] = wout32_ref[...].astype(BF16)


def _norm_matmul(x, nw, w, wdt, w_out, layer, tm, tn):
    m, k = x.shape
    n = w.shape[1]
    ni, nj = m // tm, n // tn
    _, ko, no = w_out.shape
    rows = ko // (ni * nj)
    assert rows * ni * nj == ko and rows % (2 * SUBLANES) == 0, (ko, ni, nj)
    return pl.pallas_call(
        _norm_matmul_kernel,
        grid=(ni, nj),
        in_specs=[pl.BlockSpec((tm, k), lambda i, j: (i, 0)),
                  pl.BlockSpec((1, k), lambda i, j: (0, 0)),
                  pl.BlockSpec((None, tn, k), lambda i, j: (layer, j, 0)),
                  pl.BlockSpec((None, LANES, k), lambda i, j: (layer, 0, 0)),
                  pl.BlockSpec((None, rows, no), lambda i, j: (layer, i * nj + j, 0))],
        out_specs=[pl.BlockSpec((tm, tn), lambda i, j: (i, j)),
                   pl.BlockSpec((tm, LANES), lambda i, j: (i, 0)),
                   pl.BlockSpec((rows, no), lambda i, j: (i * nj + j, 0))],
        out_shape=[jax.ShapeDtypeStruct((m, n), F32), jax.ShapeDtypeStruct((m, LANES), F32),
                   jax.ShapeDtypeStruct((ko, no), BF16)],
        scratch_shapes=[pltpu.VMEM((tm, k), BF16)],
        compiler_params=_params("arbitrary", "arbitrary"),
        name="norm_in_proj",
    )(x, nw, w, wdt, w_out)


def _conv_columns(proj_ref, hist_ref, w_ref, b_ref, lo, width):
    cur = proj_ref[:, lo:lo + width]
    sub = lax.broadcasted_iota(jnp.int32, (SUBLANES, width), 0)
    wrapped = []
    for i in range(CONV_K - 1):
        r0 = CHUNK - HIST + i * SUBLANES
        from_cur = pltpu.roll(cur[r0:r0 + SUBLANES, :], 1, axis=0)
        from_prev = pltpu.roll(hist_ref[i * SUBLANES:(i + 1) * SUBLANES, lo:lo + width], 1, axis=0)
        wrapped.append(jnp.where(sub == 0, from_prev, from_cur))
    acc = cur * w_ref[CONV_K - 1:CONV_K, lo:lo + width] + b_ref[:, lo:lo + width]
    for s in range(1, CONV_K):
        shifted = jnp.concatenate(wrapped[CONV_K - 1 - s:] + [cur[0:CHUNK - s * SUBLANES, :]], axis=0)
        acc = acc + shifted * w_ref[CONV_K - 1 - s:CONV_K - s, lo:lo + width]
    return acc, cur


def _interleave(tasks):
    tasks = list(tasks)
    while tasks:
        for t in list(tasks):
            try:
                next(t)
            except StopIteration:
                tasks.remove(t)


def _ssd_prepare(dt_raw, causal01, dtb_ref, alog_ref):
    dt = _softplus(dt_raw + dtb_ref[...])
    da = dt * (-jnp.exp(alog_ref[...]))
    cs = _dot_exact_lhs01(causal01, da) * LOG2E
    return _split3(dt), _split3(cs), cs, cs.T


def _ssd_group(g, conv, proj_ref, prep, causal, dsk_ref, nw_ref, e_ref, state, out):
    dt_parts, cs_parts, cs, cs_t = prep
    gs = slice(g * SSD_GROUP_WIDTH, (g + 1) * SSD_GROUP_WIDTH)
    xs = _silu(conv(COL_XBC + gs.start, SSD_GROUP_WIDTH)[0])
    b_g = _silu(conv(COL_XBC + SSD_WIDTH + g * SSD_STATE, SSD_STATE)[0]).astype(BF16)
    c_g = _silu(conv(COL_XBC + SSD_WIDTH + SSD_BC + g * SSD_STATE, SSD_STATE)[0]).astype(BF16)
    scores = _dot_nt(c_g, b_g)
    expand = e_ref[:, gs]
    dt_e = (_dot(dt_parts[0], expand) + _dot(dt_parts[1], expand)) + _dot(dt_parts[2], expand)
    cs_e = (_dot(cs_parts[0], expand) + _dot(cs_parts[1], expand)) + _dot(cs_parts[2], expand)
    yield
    cs_last_e = cs_e[CHUNK - 1:CHUNK, :]
    xd = xs * dt_e
    head_of_lane = lax.broadcasted_iota(jnp.int32, (CHUNK, SSD_GROUP_WIDTH), 1) // SSD_HEAD_DIM
    lhs, rhs = [], []
    for j in range(SSD_HEADS_PER_GROUP):
        h = g * SSD_HEADS_PER_GROUP + j
        seg = cs[:, h:h + 1] - cs_t[h:h + 1, :]
        decay = jnp.exp2(jnp.where(causal, seg, -jnp.inf))
        lhs.append((scores * decay).astype(BF16))
        rhs.append(jnp.where(head_of_lane == j, xd, 0.0).astype(BF16))
    y_diag = _dot(jnp.concatenate(lhs, axis=1), jnp.concatenate(rhs, axis=0))
    y_off = _dot(c_g, state.astype(BF16))
    xd_to_end = (xd * jnp.exp2(cs_last_e - cs_e)).astype(BF16)
    new_state = state * jnp.exp2(cs_last_e) + _dot_tn(b_g, xd_to_end)
    yield
    y = y_diag + y_off * jnp.exp2(cs_e) + dsk_ref[:, gs] * xs
    y = y * _silu(proj_ref[:, COL_Z + gs.start:COL_Z + gs.stop])
    yn = y * lax.rsqrt(jnp.mean(y * y, axis=-1, keepdims=True) + EPS)
    out[g] = ((yn * nw_ref[:, gs]).astype(BF16), new_state)


def _mlstm_prepare(conv, causal01, wq_ref, wk_ref, wv_ref, wif_ref, bif_ref):
    qs, ks, vs = [], [], []
    for h in range(ML_HEADS):
        mc, mx = conv(COL_ML_X + h * ML_HEAD_DIM, ML_HEAD_DIM)
        mcb = _silu(mc).astype(BF16)
        qs.append(_dot(mcb, wq_ref[h]))
        ks.append(_dot(mcb, wk_ref[h]))
        vs.append(_dot(mx.astype(BF16), wv_ref[h]))
    gates = bif_ref[...]
    for part, vals in enumerate((qs, ks, vs)):
        for h in range(ML_HEADS):
            r0 = part * ML_WIDTH + h * ML_HEAD_DIM
            gates = gates + _dot(vals[h].astype(BF16), wif_ref[r0:r0 + ML_HEAD_DIM, :])
    i_pre = gates[:, :LANES] * LOG2E
    log_f = -_softplus(-gates[:, LANES:])
    bcum = _dot_exact_lhs01(causal01, log_f) * LOG2E
    g_tot = bcum[CHUNK - 1:CHUNK, :]
    return qs, ks, vs, bcum, bcum.T, i_pre.T, g_tot, g_tot - bcum + i_pre


def _mlstm_head(h, prep, proj_ref, causal, nw_ref, c_in, n_in, m_in, out):
    qs, ks, vs, bcum, bcum_t, i_t, g_tot, w_end = prep
    hs = slice(h * ML_HEAD_DIM, (h + 1) * ML_HEAD_DIM)
    q, v = qs[h], vs[h]
    qb = q.astype(BF16)
    kf = ks[h] * ML_KSCALE
    kb = kf.astype(BF16)
    bc = bcum[:, h:h + 1]
    dmat = jnp.where(causal, bc - bcum_t[h:h + 1, :] + i_t[h:h + 1, :], -jnp.inf)
    inter_log = bc + m_in
    m_t = jnp.maximum(jnp.max(dmat, axis=1, keepdims=True), inter_log)
    qk_raw = _dot_nt(qb, kb)
    inter = _dot(qb, c_in.astype(BF16))
    we = w_end[:, h:h + 1]
    m_loc = jnp.max(we, axis=0, keepdims=True)
    yield
    p_end = jnp.exp2(we - m_loc)
    c_loc = _dot_tn(kb, (v * p_end).astype(BF16))
    n_loc = jnp.sum(kf * p_end, axis=0, keepdims=True)
    s_inter = jnp.exp2(inter_log - m_t)
    qk = qk_raw * jnp.exp2(dmat - m_t)
    num = _dot(qk.astype(BF16), v.astype(BF16)) + s_inter * inter
    den = (jnp.sum(qk, axis=1, keepdims=True)
           + s_inter * jnp.sum(q * n_in, axis=1, keepdims=True))
    yield
    o = num / jnp.maximum(jnp.abs(den), jnp.exp2(-m_t))
    mu = jnp.mean(o, axis=1, keepdims=True)
    cen = o - mu
    var = jnp.mean(cen * cen, axis=1, keepdims=True)
    yield
    hn = cen * lax.rsqrt(var + EPS) * nw_ref[:, hs]
    y = (_sigmoid(proj_ref[:, COL_ML_O + hs.start:COL_ML_O + hs.stop]) * hn).astype(BF16)
    g_h = g_tot[:, h:h + 1]
    m_new = jnp.maximum(g_h + m_in, m_loc)
    s_old = jnp.exp2(g_h + m_in - m_new)
    s_loc = jnp.exp2(m_loc - m_new)
    out[h] = (y, s_old * c_in + s_loc * c_loc, s_old * n_in + s_loc * n_loc, m_new)


def _lru_scan(a, u, h_carry):
    def tile(x, k):
        return x[k * SUBLANES:(k + 1) * SUBLANES, :]

    decay, local = [tile(a, 0)], [tile(u, 0)]
    for k in range(1, TILES):
        a_k = tile(a, k)
        local.append(a_k * local[-1] + tile(u, k))
        decay.append(a_k * decay[-1])
    p, q = decay[-1], local[-1]
    sub = lax.broadcasted_iota(jnp.int32, p.shape, 0)
    d = 1
    while d < SUBLANES:
        p_prev = jnp.where(sub >= d, pltpu.roll(p, d, axis=0), 1.0)
        q_prev = jnp.where(sub >= d, pltpu.roll(q, d, axis=0), 0.0)
        q = p * q_prev + q
        p = p * p_prev
        d *= 2
    h_prev = jnp.broadcast_to(h_carry, p.shape)
    run_end = q + p * h_prev
    run_in = jnp.where(sub == 0, h_prev, pltpu.roll(run_end, 1, axis=0))
    hs = jnp.concatenate([local[k] + decay[k] * run_in for k in range(TILES)], axis=0)
    return hs, run_end[SUBLANES - 1:SUBLANES, :]


def _lru_tile(t, conv, proj_ref, wa_ref, ba_ref, wx_ref, bx_ref, lam_ref, h_carry, out):
    ts = slice(t * MXU_DIM, (t + 1) * MXU_DIM)
    xc = conv(COL_LRU_X + ts.start, MXU_DIM)[0]
    xcb = xc.astype(BF16)
    ra = _dot(xcb, wa_ref[t])
    ix = _dot(xcb, wx_ref[t])
    yield
    r = _sigmoid(ra + ba_ref[:, ts])
    i = _sigmoid(ix + bx_ref[:, ts])
    log_a = (-LRU_C * r) * _softplus(-lam_ref[:, ts])
    a = jnp.exp(log_a)
    u = jnp.sqrt(1.0 - jnp.exp(2.0 * log_a)) * (i * xc)
    yield
    hs, h_last = _lru_scan(a, u, h_carry)
    gate = proj_ref[:, COL_LRU_GATE + ts.start:COL_LRU_GATE + ts.stop]
    out[t] = ((hs * jax.nn.gelu(gate, approximate=True)).astype(BF16), h_last)


def _splat(x11, shape):
    return jnp.broadcast_to(x11, shape)


def _mixer_kernel(proj_ref, dt_ref, cw_ref, cb_ref,
                  dtb_ref, alog_ref, dsk_ref, snw_ref, e_ref,
                  wa_ref, ba_ref, wx_ref, bx_ref, lam_ref,
                  wq_ref, wk_ref, wv_ref, wif_ref, bif_ref, mnw_ref,
                  y_ref,
                  hist_ref, state_ref, h_ref, c_ref, n_ref, m_ref):
    @pl.when(pl.program_id(1) == 0)
    def _():
        for ref in (hist_ref, state_ref, h_ref, c_ref, n_ref, m_ref):
            ref[...] = jnp.zeros(ref.shape, F32)

    causal = _causal_mask()
    causal01 = causal.astype(BF16)
    conv = functools.partial(_conv_columns, proj_ref, hist_ref, cw_ref, cb_ref)
    n_lru = LRU_WIDTH // MXU_DIM

    ml_out, ssd_out, lru_out = {}, {}, {}
    ml_states = [(c_ref[h], n_ref[h][0:1, :], m_ref[h][0:1, 0:1]) for h in range(ML_HEADS)]
    ssd_prep = _ssd_prepare(dt_ref[...], causal01, dtb_ref, alog_ref)
    ssd = [_ssd_group(g, conv, proj_ref, ssd_prep, causal, dsk_ref, snw_ref, e_ref, state_ref[g], ssd_out)
           for g in range(SSD_GROUPS)]
    lru = [_lru_tile(t, conv, proj_ref, wa_ref, ba_ref, wx_ref, bx_ref, lam_ref,
                     h_ref[0:1, t * MXU_DIM:(t + 1) * MXU_DIM], lru_out) for t in range(n_lru)]
    ml_prep = _mlstm_prepare(conv, causal01, wq_ref, wk_ref, wv_ref, wif_ref, bif_ref)
    for t in ssd + lru:
        next(t)
    heads = [_mlstm_head(h, ml_prep, proj_ref, causal, mnw_ref, *ml_states[h], ml_out)
             for h in range(ML_HEADS)]
    per_head = SSD_GROUPS // ML_HEADS
    order = []
    for h in range(ML_HEADS):
        order += [heads[h]] + ssd[h * per_head:(h + 1) * per_head] + lru[h:h + 1]
    _interleave(order)

    hist_ref[...] = proj_ref[CHUNK - HIST:, :CONV_WIDTH]
    for g in range(SSD_GROUPS):
        y, new_state = ssd_out[g]
        y_ref[:, OUT_SSD + g * SSD_GROUP_WIDTH:OUT_SSD + (g + 1) * SSD_GROUP_WIDTH] = y
        state_ref[g] = new_state
    for t in range(n_lru):
        y, h_last = lru_out[t]
        y_ref[:, OUT_LRU + t * MXU_DIM:OUT_LRU + (t + 1) * MXU_DIM] = y
        h_ref[0:1, t * MXU_DIM:(t + 1) * MXU_DIM] = h_last
    for h in range(ML_HEADS):
        y, c_new, n_new, m_new = ml_out[h]
        y_ref[:, OUT_ML + h * ML_HEAD_DIM:OUT_ML + (h + 1) * ML_HEAD_DIM] = y
        c_ref[h] = c_new
        n_ref[h] = _splat(n_new, (SUBLANES, ML_HEAD_DIM))
        m_ref[h] = _splat(m_new, (SUBLANES, LANES))


def _mixers(proj, dt_raw, p):
    b, s, _ = proj.shape
    pad = LANES - SSD_HEADS
    head = jnp.arange(LANES)[:, None]
    expand = (head == (jnp.arange(SSD_WIDTH)[None, :] // SSD_HEAD_DIM)).astype(BF16)
    w_if, b_if = _pad_gate_weights(p["ml_w_if"], p["ml_b_if"])
    operands = [
        jnp.concatenate([p["ssd_conv_w"], p["lru_conv_w"], p["ml_conv_w"]], axis=1),
        jnp.concatenate([p["ssd_conv_b"], p["lru_conv_b"], p["ml_conv_b"]])[None, :],
        jnp.pad(p["ssd_dt_bias"], (0, pad))[None, :],
        jnp.pad(p["ssd_a_log"], (0, pad))[None, :],
        jnp.repeat(p["ssd_d"], SSD_HEAD_DIM)[None, :],
        p["ssd_norm_w"][None, :],
        expand,
        _block_diag_tiles(p["lru_w_a"], MXU_DIM), p["lru_b_a"][None, :],
        _block_diag_tiles(p["lru_w_x"], MXU_DIM), p["lru_b_x"][None, :],
        p["lru_lambda"][None, :],
        _block_diag_tiles(p["ml_w_q"], ML_HEAD_DIM),
        _block_diag_tiles(p["ml_w_k"], ML_HEAD_DIM),
        _block_diag_tiles(p["ml_w_v"], ML_HEAD_DIM),
        w_if.astype(BF16), b_if, p["ml_norm_w"][None, :],
    ]

    def const(a):
        return pl.BlockSpec(a.shape, lambda i, c, nd=a.ndim: (0,) * nd)

    return pl.pallas_call(
        _mixer_kernel,
        grid=(b, s // CHUNK),
        in_specs=[pl.BlockSpec((None, CHUNK, D_PROJ), lambda i, c: (i, c, 0)),
                  pl.BlockSpec((None, CHUNK, LANES), lambda i, c: (i, c, 0))]
                 + [const(a) for a in operands],
        out_specs=pl.BlockSpec((None, CHUNK, D_MIX), lambda i, c: (i, c, 0)),
        out_shape=jax.ShapeDtypeStruct((b, s, D_MIX), BF16),
        scratch_shapes=[pltpu.VMEM((HIST, CONV_WIDTH), F32),
                        pltpu.VMEM((SSD_GROUPS, SSD_STATE, SSD_GROUP_WIDTH), F32),
                        pltpu.VMEM((SUBLANES, LRU_WIDTH), F32),
                        pltpu.VMEM((ML_HEADS, ML_HEAD_DIM, ML_HEAD_DIM), F32),
                        pltpu.VMEM((ML_HEADS, SUBLANES, ML_HEAD_DIM), F32),
                        pltpu.VMEM((ML_HEADS, SUBLANES, LANES), F32)],
        compiler_params=_params("parallel", "arbitrary"),
        name="mixers",
    )(proj, dt_raw, *operands)


def _out_proj_kernel(h_ref, y_ref, w_ref, wgu32_ref, wd32_ref, o_ref, wgu_ref, wd_ref):
    o_ref[...] = h_ref[...] + _dot(y_ref[...], w_ref[...])

    wgu_ref[...] = wgu32_ref[...].astype(BF16)

    @pl.when(pl.program_id(0) == 0)
    def _():
        wd_ref[...] = wd32_ref[...].astype(BF16)


def _out_proj(h, y, w_out, w_gate_up, w_down, layer, tm, tn):
    m, n = h.shape
    k = y.shape[1]
    nj, ni = n // tn, m // tm
    _, d, f2 = w_gate_up.shape
    f = w_down.shape[1]
    bf16_rows = 2 * SUBLANES
    assert d % (bf16_rows * ni * nj) == 0 and f % (bf16_rows * ni) == 0, (d, f, ni, nj)

    def down_block(j, i):
        return jnp.where(j == 0, i, ni - 1)

    return pl.pallas_call(
        _out_proj_kernel,
        grid=(nj, ni),
        in_specs=[pl.BlockSpec((tm, tn), lambda j, i: (i, j)),
                  pl.BlockSpec((tm, k), lambda j, i: (i, 0)),
                  pl.BlockSpec((k, tn), lambda j, i: (0, j)),
                  pl.BlockSpec((None, d // (ni * nj), f2), lambda j, i: (layer, j * ni + i, 0)),
                  pl.BlockSpec((None, f // ni, d), lambda j, i: (layer, down_block(j, i), 0))],
        out_specs=[pl.BlockSpec((tm, tn), lambda j, i: (i, j)),
                   pl.BlockSpec((d // (ni * nj), f2), lambda j, i: (j * ni + i, 0)),
                   pl.BlockSpec((f // ni, d), lambda j, i: (down_block(j, i), 0))],
        out_shape=[jax.ShapeDtypeStruct((m, n), F32),
                   jax.ShapeDtypeStruct((d, f2), BF16),
                   jax.ShapeDtypeStruct((f, d), BF16)],
        compiler_params=_params("arbitrary", "arbitrary"),
        name="out_proj",
    )(h, y, w_out, w_gate_up, w_down)


def _ffn_kernel(h_ref, nw_ref, wg_ref, wu_ref, wd_ref, fw_ref, o_ref, u_ref, *stage, last_layer):
    j = pl.program_id(1)

    @pl.when(j == 0)
    def _():
        h = h_ref[...]
        u_ref[...] = _rmsnorm(h, nw_ref[...]).astype(BF16)
        o_ref[...] = h

    u = u_ref[...]
    gate = _dot(u, wg_ref[...])
    up = _dot(u, wu_ref[...])
    o_ref[...] += _dot((_silu(gate) * up).astype(BF16), wd_ref[...])

    if last_layer:
        @pl.when(j == pl.num_programs(1) - 1)
        def _():
            _store_chunk_unpermuted(_rmsnorm(o_ref[...], fw_ref[...]), *stage, o_ref)


def _ffn(h, nw, w_gate_up, w_down, fw, tm, tf, last_layer):
    m, d = h.shape
    f = w_down.shape[0]
    nf = f // tf
    return pl.pallas_call(
        functools.partial(_ffn_kernel, last_layer=last_layer),
        grid=(m // tm, nf),
        in_specs=[pl.BlockSpec((tm, d), lambda i, j: (i, 0)),
                  pl.BlockSpec((1, d), lambda i, j: (0, 0)),
                  pl.BlockSpec((d, tf), lambda i, j: (0, j)),
                  pl.BlockSpec((d, tf), lambda i, j: (0, j + nf)),
                  pl.BlockSpec((tf, d), lambda i, j: (j, 0)),
                  pl.BlockSpec((1, d), lambda i, j: (0, 0))],
        out_specs=pl.BlockSpec((tm, d), lambda i, j: (i, 0)),
        out_shape=jax.ShapeDtypeStruct((m, d), F32),
        scratch_shapes=[pltpu.VMEM((tm, d), BF16)] + ([_stage_scratch(tm, d)] if last_layer else []),
        compiler_params=_params("parallel", "arbitrary"),
        name="ffn_last" if last_layer else "ffn",
    )(h, nw, w_gate_up, w_gate_up, w_down, fw)


def _block_diag_tiles(w, tile):
    nblk, c, _ = w.shape
    col = jnp.arange(tile)
    spread = (jnp.arange(c)[:, None] == (col % c)[None, :]).astype(w.dtype)
    same_block = ((col // c)[:, None] == (col // c)[None, :]).astype(w.dtype)
    rep = jnp.dot(w.reshape(nblk * c, c), spread, precision=lax.Precision.HIGHEST)
    return (rep.reshape(nblk * c // tile, tile, tile) * same_block).astype(BF16)


def _pad_gate_weights(w_if, b_if):
    zw = jnp.zeros((w_if.shape[0], LANES - ML_HEADS), w_if.dtype)
    w = jnp.concatenate([w_if[:, :ML_HEADS], zw, w_if[:, ML_HEADS:], zw], axis=1)
    zb = jnp.zeros((LANES - ML_HEADS,), b_if.dtype)
    b = jnp.concatenate([b_if[:ML_HEADS], zb, b_if[ML_HEADS:], zb])[None, :]
    return w, b


IN_PROJ_ROWS = 1024
IN_PROJ_COLS = D_PROJ // 8
ROW_TILE = 512
OUT_PROJ_COLS = 1024
FFN_COLS = 512
PREP_COLS = 256
PERMUTE_ROWS = 4096


def _tile(total, want):
    return want if total % want == 0 else total


def kernel(x, norm1_w, w_in, ssd_conv_w, ssd_conv_b, ssd_dt_bias, ssd_a_log, ssd_d, ssd_norm_w, lru_conv_w, lru_conv_b, lru_w_a, lru_b_a, lru_w_x, lru_b_x, lru_lambda, ml_conv_w, ml_conv_b, ml_w_q, ml_w_k, ml_w_v, ml_w_if, ml_b_if, ml_norm_w, w_out, norm2_w, w_gate_up, w_down, norm_f_w):
    b, s, d = x.shape
    m = b * s
    depth = w_in.shape[0]
    mixer_params = dict(
        ssd_conv_w=ssd_conv_w, ssd_conv_b=ssd_conv_b, ssd_dt_bias=ssd_dt_bias, ssd_a_log=ssd_a_log,
        ssd_d=ssd_d, ssd_norm_w=ssd_norm_w, lru_conv_w=lru_conv_w, lru_conv_b=lru_conv_b,
        lru_w_a=lru_w_a, lru_b_a=lru_b_a, lru_w_x=lru_w_x, lru_b_x=lru_b_x, lru_lambda=lru_lambda,
        ml_conv_w=ml_conv_w, ml_conv_b=ml_conv_b, ml_w_q=ml_w_q, ml_w_k=ml_w_k, ml_w_v=ml_w_v,
        ml_w_if=ml_w_if, ml_b_if=ml_b_if, ml_norm_w=ml_norm_w)
    w_main, w_dt = _prep_w_in(w_in, PREP_COLS)
    tm = _tile(m, ROW_TILE)
    h = _permute_rows(x.reshape(m, d), _tile(m, PERMUTE_ROWS))
    for l in range(depth):
        proj, dt_raw, w_out_b = _norm_matmul(h, norm1_w[l][None, :], w_main, w_dt, w_out, l,
                                             _tile(m, IN_PROJ_ROWS), IN_PROJ_COLS)
        y = _mixers(proj.reshape(b, s, D_PROJ), dt_raw.reshape(b, s, LANES),
                    {k: v[l] for k, v in mixer_params.items()})
        h, w_gate_up_b, w_down_b = _out_proj(h, y.reshape(m, D_MIX), w_out_b, w_gate_up, w_down, l, tm,
                                             OUT_PROJ_COLS)
        h = _ffn(h, norm2_w[l][None, :], w_gate_up_b, w_down_b, norm_f_w[None, :], tm, FFN_COLS,
                 last_layer=(l == depth - 1))
    return h.reshape(b, s, d)
```

```python
import functools

import jax
import jax.numpy as jnp
from jax import lax
from jax.experimental import pallas as pl
from jax.experimental.pallas import tpu as pltpu

F32 = jnp.float32
BF16 = jnp.bfloat16

EPS = 1e-6
LOG2E = 1.4426950408889634
CONV_K = 4
CHUNK = 128
SUBLANES = 8
LANES = 128
MXU_DIM = 256
TILES = CHUNK // SUBLANES
HIST = (CONV_K - 1) * SUBLANES

SSD_WIDTH = 2048
SSD_HEAD_DIM = 64
SSD_HEADS = 32
SSD_GROUPS = 8
SSD_STATE = 128
SSD_BC = SSD_GROUPS * SSD_STATE
SSD_XBC = SSD_WIDTH + 2 * SSD_BC
SSD_GROUP_WIDTH = SSD_WIDTH // SSD_GROUPS
SSD_HEADS_PER_GROUP = SSD_HEADS // SSD_GROUPS
LRU_WIDTH = 1024
LRU_C = 8.0
ML_WIDTH = 1024
ML_HEADS = 4
ML_HEAD_DIM = 256
ML_KSCALE = ML_HEAD_DIM ** -0.5
D_MIX = SSD_WIDTH + LRU_WIDTH + ML_WIDTH

COL_XBC = 0
COL_LRU_X = COL_XBC + SSD_XBC
COL_ML_X = COL_LRU_X + LRU_WIDTH
CONV_WIDTH = COL_ML_X + ML_WIDTH
COL_Z = CONV_WIDTH
COL_LRU_GATE = COL_Z + SSD_WIDTH
COL_ML_O = COL_LRU_GATE + LRU_WIDTH
D_PROJ = COL_ML_O + ML_WIDTH
SRC_Z = 0
SRC_XBC = SRC_Z + SSD_WIDTH
SRC_DT = SRC_XBC + SSD_XBC
SRC_LRU_GATE = SRC_DT + SSD_HEADS
SRC_LRU_X = SRC_LRU_GATE + LRU_WIDTH
SRC_ML_X = SRC_LRU_X + LRU_WIDTH
SRC_ML_O = SRC_ML_X + ML_WIDTH
D_IN = SRC_ML_O + ML_WIDTH
OUT_SSD = 0
OUT_LRU = OUT_SSD + SSD_WIDTH
OUT_ML = OUT_LRU + LRU_WIDTH

VMEM_LIMIT_BYTES = 56 * 1024 * 1024


def _params(*semantics):
    return pltpu.CompilerParams(dimension_semantics=semantics, vmem_limit_bytes=VMEM_LIMIT_BYTES)


def _dot(a, b):
    return jnp.dot(a, b, preferred_element_type=F32)


def _dot_nt(a, b):
    return lax.dot_general(a, b, (((1,), (1,)), ((), ())), preferred_element_type=F32)


def _dot_tn(a, b):
    return lax.dot_general(a, b, (((0,), (0,)), ((), ())), preferred_element_type=F32)


def _split3(x):
    hi = x.astype(BF16)
    r1 = x - hi.astype(F32)
    mid = r1.astype(BF16)
    lo = (r1 - mid.astype(F32)).astype(BF16)
    return hi, mid, lo


def _dot_exact_lhs01(m01, x):
    hi, mid, lo = _split3(x)
    return (_dot(m01, hi) + _dot(m01, mid)) + _dot(m01, lo)


def _sigmoid(x):
    return 0.5 * jnp.tanh(0.5 * x) + 0.5


def _silu(x):
    half = 0.5 * x
    return half * jnp.tanh(half) + half


def _softplus(x):
    return jnp.maximum(x, 0.0) + jnp.log1p(jnp.exp(-jnp.abs(x)))


def _rmsnorm(x, w):
    return x * lax.rsqrt(jnp.mean(x * x, axis=-1, keepdims=True) + EPS) * w


def _store_chunk_unpermuted(val, stage_ref, dst_ref):
    rows, width = val.shape
    for i in range(width // LANES):
        stage_ref[i] = val[:, i * LANES:(i + 1) * LANES]
    for i in range(width // LANES):
        for c in range(rows // CHUNK):
            for j in range(SUBLANES):
                dst_ref[c * CHUNK + j * TILES:c * CHUNK + (j + 1) * TILES, i * LANES:(i + 1) * LANES] = (
                    stage_ref[i, pl.ds(c * CHUNK + j, TILES, stride=SUBLANES), :])


def _stage_scratch(rows, width):
    return pltpu.VMEM((width // LANES, rows, LANES), F32)


def _causal_mask():
    def pos(dim):
        r = lax.broadcasted_iota(jnp.int32, (CHUNK, CHUNK), dim)
        return (r % SUBLANES) * TILES + r // SUBLANES
    return pos(0) >= pos(1)


def _permute_rows_kernel(x_ref, o_ref):
    for c in range(x_ref.shape[0] // CHUNK):
        for k in range(TILES):
            o_ref[c * CHUNK + k * SUBLANES:c * CHUNK + (k + 1) * SUBLANES, :] = (
                x_ref[pl.ds(c * CHUNK + k, SUBLANES, stride=TILES), :])


def _permute_rows(x, rows):
    m, d = x.shape
    return pl.pallas_call(
        _permute_rows_kernel,
        grid=(m // rows, d // LANES),
        in_specs=[pl.BlockSpec((rows, LANES), lambda i, j: (i, j))],
        out_specs=pl.BlockSpec((rows, LANES), lambda i, j: (i, j)),
        out_shape=jax.ShapeDtypeStruct((m, d), x.dtype),
        compiler_params=_params("parallel", "parallel"),
        name="permute_rows",
    )(x)


def _prep_w_in_kernel(w_ref, wm_ref, wdt_ref):
    for dst, src, width in ((COL_XBC, SRC_XBC, SSD_XBC), (COL_LRU_X, SRC_LRU_X, LRU_WIDTH),
                            (COL_ML_X, SRC_ML_X, ML_WIDTH), (COL_Z, SRC_Z, SSD_WIDTH),
                            (COL_LRU_GATE, SRC_LRU_GATE, LRU_WIDTH), (COL_ML_O, SRC_ML_O, ML_WIDTH)):
        wm_ref[dst:dst + width, :] = w_ref[src:src + width, :].astype(BF16)
    wdt_ref[0:SSD_HEADS, :] = w_ref[SRC_DT:SRC_DT + SSD_HEADS, :].astype(BF16)
    wdt_ref[SSD_HEADS:, :] = jnp.zeros((LANES - SSD_HEADS, wdt_ref.shape[1]), BF16)


def _prep_w_in(w_in, cols):
    depth, d, _ = w_in.shape
    return pl.pallas_call(
        _prep_w_in_kernel,
        grid=(depth, d // cols),
        in_specs=[pl.BlockSpec((None, D_IN, cols), lambda l, i: (l, 0, i))],
        out_specs=[pl.BlockSpec((None, D_PROJ, cols), lambda l, i: (l, 0, i)),
                   pl.BlockSpec((None, LANES, cols), lambda l, i: (l, 0, i))],
        out_shape=[jax.ShapeDtypeStruct((depth, D_PROJ, d), BF16),
                   jax.ShapeDtypeStruct((depth, LANES, d), BF16)],
        compiler_params=_params("parallel", "parallel"),
        name="prep_w_in",
    )(jnp.swapaxes(w_in, 1, 2))


def _norm_matmul_kernel(x_ref, nw_ref, w_ref, wdt_ref, wout32_ref, o_ref, dt_ref, wout_ref, u_ref):
    @pl.when(pl.program_id(1) == 0)
    def _():
        u = _rmsnorm(x_ref[...], nw_ref[...]).astype(BF16)
        u_ref[...] = u
        dt_ref[...] = _dot_nt(u, wdt_ref[...])

    o_ref[...] = _dot_nt(u_ref[...], w_ref[...])
    wout_ref[...] = wout32_ref[...].astype(BF16)


def _norm_matmul(x, nw, w, wdt, w_out, layer, tm, tn):
    m, k = x.shape
    n = w.shape[1]
    ni, nj = m // tm, n // tn
    _, ko, no = w_out.shape
    rows = ko // (ni * nj)
    assert rows * ni * nj == ko and rows % (2 * SUBLANES) == 0, (ko, ni, nj)
    return pl.pallas_call(
        _norm_matmul_kernel,
        grid=(ni, nj),
        in_specs=[pl.BlockSpec((tm, k), lambda i, j: (i, 0)),
                  pl.BlockSpec((1, k), lambda i, j: (0, 0)),
                  pl.BlockSpec((None, tn, k), lambda i, j: (layer, j, 0)),
                  pl.BlockSpec((None, LANES, k), lambda i, j: (layer, 0, 0)),
                  pl.BlockSpec((None, rows, no), lambda i, j: (layer, i * nj + j, 0))],
        out_specs=[pl.BlockSpec((tm, tn), lambda i, j: (i, j)),
                   pl.BlockSpec((tm, LANES), lambda i, j: (i, 0)),
                   pl.BlockSpec((rows, no), lambda i, j: (i * nj + j, 0))],
        out_shape=[jax.ShapeDtypeStruct((m, n), F32), jax.ShapeDtypeStruct((m, LANES), F32),
                   jax.ShapeDtypeStruct((ko, no), BF16)],
        scratch_shapes=[pltpu.VMEM((tm, k), BF16)],
        compiler_params=_params("arbitrary", "arbitrary"),
        name="norm_in_proj",
    )(x, nw, w, wdt, w_out)


def _conv_columns(proj_ref, hist_ref, w_ref, b_ref, lo, width):
    cur = proj_ref[:, lo:lo + width]
    sub = lax.broadcasted_iota(jnp.int32, (SUBLANES, width), 0)
    wrapped = []
    for i in range(CONV_K - 1):
        r0 = CHUNK - HIST + i * SUBLANES
        from_cur = pltpu.roll(cur[r0:r0 + SUBLANES, :], 1, axis=0)
        from_prev = pltpu.roll(hist_ref[i * SUBLANES:(i + 1) * SUBLANES, lo:lo + width], 1, axis=0)
        wrapped.append(jnp.where(sub == 0, from_prev, from_cur))
    acc = cur * w_ref[CONV_K - 1:CONV_K, lo:lo + width] + b_ref[:, lo:lo + width]
    for s in range(1, CONV_K):
        shifted = jnp.concatenate(wrapped[CONV_K - 1 - s:] + [cur[0:CHUNK - s * SUBLANES, :]], axis=0)
        acc = acc + shifted * w_ref[CONV_K - 1 - s:CONV_K - s, lo:lo + width]
    return acc, cur


def _interleave(tasks):
    tasks = list(tasks)
    while tasks:
        for t in list(tasks):
            try:
                next(t)
            except StopIteration:
                tasks.remove(t)


def _ssd_prepare(dt_raw, causal01, dtb_ref, alog_ref):
    dt = _softplus(dt_raw + dtb_ref[...])
    da = dt * (-jnp.exp(alog_ref[...]))
    cs = _dot_exact_lhs01(causal01, da) * LOG2E
    return _split3(dt), _split3(cs), cs, cs.T


def _ssd_group(g, conv, proj_ref, prep, causal, dsk_ref, nw_ref, e_ref, state, out):
    dt_parts, cs_parts, cs, cs_t = prep
    gs = slice(g * SSD_GROUP_WIDTH, (g + 1) * SSD_GROUP_WIDTH)
    xs = _silu(conv(COL_XBC + gs.start, SSD_GROUP_WIDTH)[0])
    b_g = _silu(conv(COL_XBC + SSD_WIDTH + g * SSD_STATE, SSD_STATE)[0]).astype(BF16)
    c_g = _silu(conv(COL_XBC + SSD_WIDTH + SSD_BC + g * SSD_STATE, SSD_STATE)[0]).astype(BF16)
    scores = _dot_nt(c_g, b_g)
    expand = e_ref[:, gs]
    dt_e = (_dot(dt_parts[0], expand) + _dot(dt_parts[1], expand)) + _dot(dt_parts[2], expand)
    cs_e = (_dot(cs_parts[0], expand) + _dot(cs_parts[1], expand)) + _dot(cs_parts[2], expand)
    yield
    cs_last_e = cs_e[CHUNK - 1:CHUNK, :]
    xd = xs * dt_e
    head_of_lane = lax.broadcasted_iota(jnp.int32, (CHUNK, SSD_GROUP_WIDTH), 1) // SSD_HEAD_DIM
    lhs, rhs = [], []
    for j in range(SSD_HEADS_PER_GROUP):
        h = g * SSD_HEADS_PER_GROUP + j
        seg = cs[:, h:h + 1] - cs_t[h:h + 1, :]
        decay = jnp.exp2(jnp.where(causal, seg, -jnp.inf))
        lhs.append((scores * decay).astype(BF16))
        rhs.append(jnp.where(head_of_lane == j, xd, 0.0).astype(BF16))
    y_diag = _dot(jnp.concatenate(lhs, axis=1), jnp.concatenate(rhs, axis=0))
    y_off = _dot(c_g, state.astype(BF16))
    xd_to_end = (xd * jnp.exp2(cs_last_e - cs_e)).astype(BF16)
    new_state = state * jnp.exp2(cs_last_e) + _dot_tn(b_g, xd_to_end)
    yield
    y = y_diag + y_off * jnp.exp2(cs_e) + dsk_ref[:, gs] * xs
    y = y * _silu(proj_ref[:, COL_Z + gs.start:COL_Z + gs.stop])
    yn = y * lax.rsqrt(jnp.mean(y * y, axis=-1, keepdims=True) + EPS)
    out[g] = ((yn * nw_ref[:, gs]).astype(BF16), new_state)


def _mlstm_prepare(conv, causal01, wq_ref, wk_ref, wv_ref, wif_ref, bif_ref):
    qs, ks, vs = [], [], []
    for h in range(ML_HEADS):
        mc, mx = conv(COL_ML_X + h * ML_HEAD_DIM, ML_HEAD_DIM)
        mcb = _silu(mc).astype(BF16)
        qs.append(_dot(mcb, wq_ref[h]))
        ks.append(_dot(mcb, wk_ref[h]))
        vs.append(_dot(mx.astype(BF16), wv_ref[h]))
    gates = bif_ref[...]
    for part, vals in enumerate((qs, ks, vs)):
        for h in range(ML_HEADS):
            r0 = part * ML_WIDTH + h * ML_HEAD_DIM
            gates = gates + _dot(vals[h].astype(BF16), wif_ref[r0:r0 + ML_HEAD_DIM, :])
    i_pre = gates[:, :LANES] * LOG2E
    log_f = -_softplus(-gates[:, LANES:])
    bcum = _dot_exact_lhs01(causal01, log_f) * LOG2E
    g_tot = bcum[CHUNK - 1:CHUNK, :]
    return qs, ks, vs, bcum, bcum.T, i_pre.T, g_tot, g_tot - bcum + i_pre


def _mlstm_head(h, prep, proj_ref, causal, nw_ref, c_in, n_in, m_in, out):
    qs, ks, vs, bcum, bcum_t, i_t, g_tot, w_end = prep
    hs = slice(h * ML_HEAD_DIM, (h + 1) * ML_HEAD_DIM)
    q, v = qs[h], vs[h]
    qb = q.astype(BF16)
    kf = ks[h] * ML_KSCALE
    kb = kf.astype(BF16)
    bc = bcum[:, h:h + 1]
    dmat = jnp.where(causal, bc - bcum_t[h:h + 1, :] + i_t[h:h + 1, :], -jnp.inf)
    inter_log = bc + m_in
    m_t = jnp.maximum(jnp.max(dmat, axis=1, keepdims=True), inter_log)
    qk_raw = _dot_nt(qb, kb)
    inter = _dot(qb, c_in.astype(BF16))
    we = w_end[:, h:h + 1]
    m_loc = jnp.max(we, axis=0, keepdims=True)
    yield
    p_end = jnp.exp2(we - m_loc)
    c_loc = _dot_tn(kb, (v * p_end).astype(BF16))
    n_loc = jnp.sum(kf * p_end, axis=0, keepdims=True)
    s_inter = jnp.exp2(inter_log - m_t)
    qk = qk_raw * jnp.exp2(dmat - m_t)
    num = _dot(qk.astype(BF16), v.astype(BF16)) + s_inter * inter
    den = (jnp.sum(qk, axis=1, keepdims=True)
           + s_inter * jnp.sum(q * n_in, axis=1, keepdims=True))
    yield
    o = num / jnp.maximum(jnp.abs(den), jnp.exp2(-m_t))
    mu = jnp.mean(o, axis=1, keepdims=True)
    cen = o - mu
    var = jnp.mean(cen * cen, axis=1, keepdims=True)
    yield
    hn = cen * lax.rsqrt(var + EPS) * nw_ref[:, hs]
    y = (_sigmoid(proj_ref[:, COL_ML_O + hs.start:COL_ML_O + hs.stop]) * hn).astype(BF16)
    g_h = g_tot[:, h:h + 1]
    m_new = jnp.maximum(g_h + m_in, m_loc)
    s_old = jnp.exp2(g_h + m_in - m_new)
    s_loc = jnp.exp2(m_loc - m_new)
    out[h] = (y, s_old * c_in + s_loc * c_loc, s_old * n_in + s_loc * n_loc, m_new)


def _lru_scan(a, u, h_carry):
    def tile(x, k):
        return x[k * SUBLANES:(k + 1) * SUBLANES, :]

    decay, local = [tile(a, 0)], [tile(u, 0)]
    for k in range(1, TILES):
        a_k = tile(a, k)
        local.append(a_k * local[-1] + tile(u, k))
        decay.append(a_k * decay[-1])
    p, q = decay[-1], local[-1]
    sub = lax.broadcasted_iota(jnp.int32, p.shape, 0)
    d = 1
    while d < SUBLANES:
        p_prev = jnp.where(sub >= d, pltpu.roll(p, d, axis=0), 1.0)
        q_prev = jnp.where(sub >= d, pltpu.roll(q, d, axis=0), 0.0)
        q = p * q_prev + q
        p = p * p_prev
        d *= 2
    h_prev = jnp.broadcast_to(h_carry, p.shape)
    run_end = q + p * h_prev
    run_in = jnp.where(sub == 0, h_prev, pltpu.roll(run_end, 1, axis=0))
    hs = jnp.concatenate([local[k] + decay[k] * run_in for k in range(TILES)], axis=0)
    return hs, run_end[SUBLANES - 1:SUBLANES, :]


def _lru_tile(t, conv, proj_ref, wa_ref, ba_ref, wx_ref, bx_ref, lam_ref, h_carry, out):
    ts = slice(t * MXU_DIM, (t + 1) * MXU_DIM)
    xc = conv(COL_LRU_X + ts.start, MXU_DIM)[0]
    xcb = xc.astype(BF16)
    ra = _dot(xcb, wa_ref[t])
    ix = _dot(xcb, wx_ref[t])
    yield
    r = _sigmoid(ra + ba_ref[:, ts])
    i = _sigmoid(ix + bx_ref[:, ts])
    log_a = (-LRU_C * r) * _softplus(-lam_ref[:, ts])
    a = jnp.exp(log_a)
    u = jnp.sqrt(1.0 - jnp.exp(2.0 * log_a)) * (i * xc)
    yield
    hs, h_last = _lru_scan(a, u, h_carry)
    gate = proj_ref[:, COL_LRU_GATE + ts.start:COL_LRU_GATE + ts.stop]
    out[t] = ((hs * jax.nn.gelu(gate, approximate=True)).astype(BF16), h_last)


def _splat(x11, shape):
    return jnp.broadcast_to(x11, shape)


def _mixer_kernel(proj_ref, dt_ref, cw_ref, cb_ref,
                  dtb_ref, alog_ref, dsk_ref, snw_ref, e_ref,
                  wa_ref, ba_ref, wx_ref, bx_ref, lam_ref,
                  wq_ref, wk_ref, wv_ref, wif_ref, bif_ref, mnw_ref,
                  y_ref,
                  hist_ref, state_ref, h_ref, c_ref, n_ref, m_ref):
    @pl.when(pl.program_id(1) == 0)
    def _():
        for ref in (hist_ref, state_ref, h_ref, c_ref, n_ref, m_ref):
            ref[...] = jnp.zeros(ref.shape, F32)

    causal = _causal_mask()
    causal01 = causal.astype(BF16)
    conv = functools.partial(_conv_columns, proj_ref, hist_ref, cw_ref, cb_ref)
    n_lru = LRU_WIDTH // MXU_DIM

    ml_out, ssd_out, lru_out = {}, {}, {}
    ml_states = [(c_ref[h], n_ref[h][0:1, :], m_ref[h][0:1, 0:1]) for h in range(ML_HEADS)]
    ssd_prep = _ssd_prepare(dt_ref[...], causal01, dtb_ref, alog_ref)
    ssd = [_ssd_group(g, conv, proj_ref, ssd_prep, causal, dsk_ref, snw_ref, e_ref, state_ref[g], ssd_out)
           for g in range(SSD_GROUPS)]
    lru = [_lru_tile(t, conv, proj_ref, wa_ref, ba_ref, wx_ref, bx_ref, lam_ref,
                     h_ref[0:1, t * MXU_DIM:(t + 1) * MXU_DIM], lru_out) for t in range(n_lru)]
    ml_prep = _mlstm_prepare(conv, causal01, wq_ref, wk_ref, wv_ref, wif_ref, bif_ref)
    for t in ssd + lru:
        next(t)
    heads = [_mlstm_head(h, ml_prep, proj_ref, causal, mnw_ref, *ml_states[h], ml_out)
             for h in range(ML_HEADS)]
    per_head = SSD_GROUPS // ML_HEADS
    order = []
    for h in range(ML_HEADS):
        order += [heads[h]] + ssd[h * per_head:(h + 1) * per_head] + lru[h:h + 1]
    _interleave(order)

    hist_ref[...] = proj_ref[CHUNK - HIST:, :CONV_WIDTH]
    for g in range(SSD_GROUPS):
        y, new_state = ssd_out[g]
        y_ref[:, OUT_SSD + g * SSD_GROUP_WIDTH:OUT_SSD + (g + 1) * SSD_GROUP_WIDTH] = y
        state_ref[g] = new_state
    for t in range(n_lru):
        y, h_last = lru_out[t]
        y_ref[:, OUT_LRU + t * MXU_DIM:OUT_LRU + (t + 1) * MXU_DIM] = y
        h_ref[0:1, t * MXU_DIM:(t + 1) * MXU_DIM] = h_last
    for h in range(ML_HEADS):
        y, c_new, n_new, m_new = ml_out[h]
        y_ref[:, OUT_ML + h * ML_HEAD_DIM:OUT_ML + (h + 1) * ML_HEAD_DIM] = y
        c_ref[h] = c_new
        n_ref[h] = _splat(n_new, (SUBLANES, ML_HEAD_DIM))
        m_ref[h] = _splat(m_new, (SUBLANES, LANES))


def _mixers(proj, dt_raw, p):
    b, s, _ = proj.shape
    pad = LANES - SSD_HEADS
    head = jnp.arange(LANES)[:, None]
    expand = (head == (jnp.arange(SSD_WIDTH)[None, :] // SSD_HEAD_DIM)).astype(BF16)
    w_if, b_if = _pad_gate_weights(p["ml_w_if"], p["ml_b_if"])
    operands = [
        jnp.concatenate([p["ssd_conv_w"], p["lru_conv_w"], p["ml_conv_w"]], axis=1),
        jnp.concatenate([p["ssd_conv_b"], p["lru_conv_b"], p["ml_conv_b"]])[None, :],
        jnp.pad(p["ssd_dt_bias"], (0, pad))[None, :],
        jnp.pad(p["ssd_a_log"], (0, pad))[None, :],
        jnp.repeat(p["ssd_d"], SSD_HEAD_DIM)[None, :],
        p["ssd_norm_w"][None, :],
        expand,
        _block_diag_tiles(p["lru_w_a"], MXU_DIM), p["lru_b_a"][None, :],
        _block_diag_tiles(p["lru_w_x"], MXU_DIM), p["lru_b_x"][None, :],
        p["lru_lambda"][None, :],
        _block_diag_tiles(p["ml_w_q"], ML_HEAD_DIM),
        _block_diag_tiles(p["ml_w_k"], ML_HEAD_DIM),
        _block_diag_tiles(p["ml_w_v"], ML_HEAD_DIM),
        w_if.astype(BF16), b_if, p["ml_norm_w"][None, :],
    ]

    def const(a):
        return pl.BlockSpec(a.shape, lambda i, c, nd=a.ndim: (0,) * nd)

    return pl.pallas_call(
        _mixer_kernel,
        grid=(b, s // CHUNK),
        in_specs=[pl.BlockSpec((None, CHUNK, D_PROJ), lambda i, c: (i, c, 0)),
                  pl.BlockSpec((None, CHUNK, LANES), lambda i, c: (i, c, 0))]
                 + [const(a) for a in operands],
        out_specs=pl.BlockSpec((None, CHUNK, D_MIX), lambda i, c: (i, c, 0)),
        out_shape=jax.ShapeDtypeStruct((b, s, D_MIX), BF16),
        scratch_shapes=[pltpu.VMEM((HIST, CONV_WIDTH), F32),
                        pltpu.VMEM((SSD_GROUPS, SSD_STATE, SSD_GROUP_WIDTH), F32),
                        pltpu.VMEM((SUBLANES, LRU_WIDTH), F32),
                        pltpu.VMEM((ML_HEADS, ML_HEAD_DIM, ML_HEAD_DIM), F32),
                        pltpu.VMEM((ML_HEADS, SUBLANES, ML_HEAD_DIM), F32),
                        pltpu.VMEM((ML_HEADS, SUBLANES, LANES), F32)],
        compiler_params=_params("parallel", "arbitrary"),
        name="mixers",
    )(proj, dt_raw, *operands)


def _out_proj_kernel(h_ref, y_ref, w_ref, wgu32_ref, wd32_ref, o_ref, wgu_ref, wd_ref):
    o_ref[...] = h_ref[...] + _dot(y_ref[...], w_ref[...])

    wgu_ref[...] = wgu32_ref[...].astype(BF16)

    @pl.when(pl.program_id(0) == 0)
    def _():
        wd_ref[...] = wd32_ref[...].astype(BF16)


def _out_proj(h, y, w_out, w_gate_up, w_down, layer, tm, tn):
    m, n = h.shape
    k = y.shape[1]
    nj, ni = n // tn, m // tm
    _, d, f2 = w_gate_up.shape
    f = w_down.shape[1]
    bf16_rows = 2 * SUBLANES
    assert d % (bf16_rows * ni * nj) == 0 and f % (bf16_rows * ni) == 0, (d, f, ni, nj)

    def down_block(j, i):
        return jnp.where(j == 0, i, ni - 1)

    return pl.pallas_call(
        _out_proj_kernel,
        grid=(nj, ni),
        in_specs=[pl.BlockSpec((tm, tn), lambda j, i: (i, j)),
                  pl.BlockSpec((tm, k), lambda j, i: (i, 0)),
                  pl.BlockSpec((k, tn), lambda j, i: (0, j)),
                  pl.BlockSpec((None, d // (ni * nj), f2), lambda j, i: (layer, j * ni + i, 0)),
                  pl.BlockSpec((None, f // ni, d), lambda j, i: (layer, down_block(j, i), 0))],
        out_specs=[pl.BlockSpec((tm, tn), lambda j, i: (i, j)),
                   pl.BlockSpec((d // (ni * nj), f2), lambda j, i: (j * ni + i, 0)),
                   pl.BlockSpec((f // ni, d), lambda j, i: (down_block(j, i), 0))],
        out_shape=[jax.ShapeDtypeStruct((m, n), F32),
                   jax.ShapeDtypeStruct((d, f2), BF16),
                   jax.ShapeDtypeStruct((f, d), BF16)],
        compiler_params=_params("arbitrary", "arbitrary"),
        name="out_proj",
    )(h, y, w_out, w_gate_up, w_down)


def _ffn_kernel(h_ref, nw_ref, wg_ref, wu_ref, wd_ref, fw_ref, o_ref, u_ref, *stage, last_layer):
    j = pl.program_id(1)

    @pl.when(j == 0)
    def _():
        h = h_ref[...]
        u_ref[...] = _rmsnorm(h, nw_ref[...]).astype(BF16)
        o_ref[...] = h

    u = u_ref[...]
    gate = _dot(u, wg_ref[...])
    up = _dot(u, wu_ref[...])
    o_ref[...] += _dot((_silu(gate) * up).astype(BF16), wd_ref[...])

    if last_layer:
        @pl.when(j == pl.num_programs(1) - 1)
        def _():
            _store_chunk_unpermuted(_rmsnorm(o_ref[...], fw_ref[...]), *stage, o_ref)


def _ffn(h, nw, w_gate_up, w_down, fw, tm, tf, last_layer):
    m, d = h.shape
    f = w_down.shape[0]
    nf = f // tf
    return pl.pallas_call(
        functools.partial(_ffn_kernel, last_layer=last_layer),
        grid=(m // tm, nf),
        in_specs=[pl.BlockSpec((tm, d), lambda i, j: (i, 0)),
                  pl.BlockSpec((1, d), lambda i, j: (0, 0)),
                  pl.BlockSpec((d, tf), lambda i, j: (0, j)),
                  pl.BlockSpec((d, tf), lambda i, j: (0, j + nf)),
                  pl.BlockSpec((tf, d), lambda i, j: (j, 0)),
                  pl.BlockSpec((1, d), lambda i, j: (0, 0))],
        out_specs=pl.BlockSpec((tm, d), lambda i, j: (i, 0)),
        out_shape=jax.ShapeDtypeStruct((m, d), F32),
        scratch_shapes=[pltpu.VMEM((tm, d), BF16)] + ([_stage_scratch(tm, d)] if last_layer else []),
        compiler_params=_params("parallel", "arbitrary"),
        name="ffn_last" if last_layer else "ffn",
    )(h, nw, w_gate_up, w_gate_up, w_down, fw)


def _block_diag_tiles(w, tile):
    nblk, c, _ = w.shape
    col = jnp.arange(tile)
    spread = (jnp.arange(c)[:, None] == (col % c)[None, :]).astype(w.dtype)
    same_block = ((col // c)[:, None] == (col // c)[None, :]).astype(w.dtype)
    rep = jnp.dot(w.reshape(nblk * c, c), spread, precision=lax.Precision.HIGHEST)
    return (rep.reshape(nblk * c // tile, tile, tile) * same_block).astype(BF16)


def _pad_gate_weights(w_if, b_if):
    zw = jnp.zeros((w_if.shape[0], LANES - ML_HEADS), w_if.dtype)
    w = jnp.concatenate([w_if[:, :ML_HEADS], zw, w_if[:, ML_HEADS:], zw], axis=1)
    zb = jnp.zeros((LANES - ML_HEADS,), b_if.dtype)
    b = jnp.concatenate([b_if[:ML_HEADS], zb, b_if[ML_HEADS:], zb])[None, :]
    return w, b


IN_PROJ_ROWS = 1024
IN_PROJ_COLS = D_PROJ // 8
ROW_TILE = 512
FFN_ROWS = 1024
OUT_PROJ_COLS = 1024
FFN_COLS = 512
PREP_COLS = 256
PERMUTE_ROWS = 4096


def _tile(total, want):
    return want if total % want == 0 else total


def kernel(x, norm1_w, w_in, ssd_conv_w, ssd_conv_b, ssd_dt_bias, ssd_a_log, ssd_d, ssd_norm_w, lru_conv_w, lru_conv_b, lru_w_a, lru_b_a, lru_w_x, lru_b_x, lru_lambda, ml_conv_w, ml_conv_b, ml_w_q, ml_w_k, ml_w_v, ml_w_if, ml_b_if, ml_norm_w, w_out, norm2_w, w_gate_up, w_down, norm_f_w):
    b, s, d = x.shape
    m = b * s
    depth = w_in.shape[0]
    mixer_params = dict(
        ssd_conv_w=ssd_conv_w, ssd_conv_b=ssd_conv_b, ssd_dt_bias=ssd_dt_bias, ssd_a_log=ssd_a_log,
        ssd_d=ssd_d, ssd_norm_w=ssd_norm_w, lru_conv_w=lru_conv_w, lru_conv_b=lru_conv_b,
        lru_w_a=lru_w_a, lru_b_a=lru_b_a, lru_w_x=lru_w_x, lru_b_x=lru_b_x, lru_lambda=lru_lambda,
        ml_conv_w=ml_conv_w, ml_conv_b=ml_conv_b, ml_w_q=ml_w_q, ml_w_k=ml_w_k, ml_w_v=ml_w_v,
        ml_w_if=ml_w_if, ml_b_if=ml_b_if, ml_norm_w=ml_norm_w)
    w_main, w_dt = _prep_w_in(w_in, PREP_COLS)
    tm = _tile(m, ROW_TILE)
    h = _permute_rows(x.reshape(m, d), _tile(m, PERMUTE_ROWS))
    for l in range(depth):
        proj, dt_raw, w_out_b = _norm_matmul(h, norm1_w[l][None, :], w_main, w_dt, w_out, l,
                                             _tile(m, IN_PROJ_ROWS), IN_PROJ_COLS)
        y = _mixers(proj.reshape(b, s, D_PROJ), dt_raw.reshape(b, s, LANES),
                    {k: v[l] for k, v in mixer_params.items()})
        h, w_gate_up_b, w_down_b = _out_proj(h, y.reshape(m, D_MIX), w_out_b, w_gate_up, w_down, l, tm,
                                             OUT_PROJ_COLS)
        last = l == depth - 1
        h = _ffn(h, norm2_w[l][None, :], w_gate_up_b, w_down_b, norm_f_w[None, :],
                 tm if last else _tile(m, FFN_ROWS), FFN_COLS, last_layer=last)
    return h.reshape(b, s, d)
```

```python
import functools

import jax
import jax.numpy as jnp
from jax import lax
from jax.experimental import pallas as pl
from jax.experimental.pallas import tpu as pltpu

F32 = jnp.float32
BF16 = jnp.bfloat16

EPS = 1e-6
LOG2E = 1.4426950408889634
CONV_K = 4
CHUNK = 128
SUBLANES = 8
LANES = 128
MXU_DIM = 256
TILES = CHUNK // SUBLANES
HIST = (CONV_K - 1) * SUBLANES

SSD_WIDTH = 2048
SSD_HEAD_DIM = 64
SSD_HEADS = 32
SSD_GROUPS = 8
SSD_STATE = 128
SSD_BC = SSD_GROUPS * SSD_STATE
SSD_XBC = SSD_WIDTH + 2 * SSD_BC
SSD_GROUP_WIDTH = SSD_WIDTH // SSD_GROUPS
SSD_HEADS_PER_GROUP = SSD_HEADS // SSD_GROUPS
LRU_WIDTH = 1024
LRU_C = 8.0
ML_WIDTH = 1024
ML_HEADS = 4
ML_HEAD_DIM = 256
ML_KSCALE = ML_HEAD_DIM ** -0.5
D_MIX = SSD_WIDTH + LRU_WIDTH + ML_WIDTH

COL_XBC = 0
COL_LRU_X = COL_XBC + SSD_XBC
COL_ML_X = COL_LRU_X + LRU_WIDTH
CONV_WIDTH = COL_ML_X + ML_WIDTH
COL_Z = CONV_WIDTH
COL_LRU_GATE = COL_Z + SSD_WIDTH
COL_ML_O = COL_LRU_GATE + LRU_WIDTH
D_PROJ = COL_ML_O + ML_WIDTH
SRC_Z = 0
SRC_XBC = SRC_Z + SSD_WIDTH
SRC_DT = SRC_XBC + SSD_XBC
SRC_LRU_GATE = SRC_DT + SSD_HEADS
SRC_LRU_X = SRC_LRU_GATE + LRU_WIDTH
SRC_ML_X = SRC_LRU_X + LRU_WIDTH
SRC_ML_O = SRC_ML_X + ML_WIDTH
D_IN = SRC_ML_O + ML_WIDTH
OUT_SSD = 0
OUT_LRU = OUT_SSD + SSD_WIDTH
OUT_ML = OUT_LRU + LRU_WIDTH

VMEM_LIMIT_BYTES = 60 * 1024 * 1024


def _params(*semantics):
    return pltpu.CompilerParams(dimension_semantics=semantics, vmem_limit_bytes=VMEM_LIMIT_BYTES)


def _dot(a, b):
    return jnp.dot(a, b, preferred_element_type=F32)


def _dot_nt(a, b):
    return lax.dot_general(a, b, (((1,), (1,)), ((), ())), preferred_element_type=F32)


def _dot_tn(a, b):
    return lax.dot_general(a, b, (((0,), (0,)), ((), ())), preferred_element_type=F32)


def _split3(x):
    hi = x.astype(BF16)
    r1 = x - hi.astype(F32)
    mid = r1.astype(BF16)
    lo = (r1 - mid.astype(F32)).astype(BF16)
    return hi, mid, lo


def _dot_exact_lhs01(m01, x):
    hi, mid, lo = _split3(x)
    return (_dot(m01, hi) + _dot(m01, mid)) + _dot(m01, lo)


def _sigmoid(x):
    return 0.5 * jnp.tanh(0.5 * x) + 0.5


def _silu(x):
    half = 0.5 * x
    return half * jnp.tanh(half) + half


def _softplus(x):
    return jnp.maximum(x, 0.0) + jnp.log1p(jnp.exp(-jnp.abs(x)))


def _rmsnorm(x, w):
    return x * lax.rsqrt(jnp.mean(x * x, axis=-1, keepdims=True) + EPS) * w


def _store_chunk_unpermuted(val, stage_ref, dst_ref):
    rows, width = val.shape
    for i in range(width // LANES):
        slot = i % stage_ref.shape[0]
        stage_ref[slot] = val[:, i * LANES:(i + 1) * LANES]
        for c in range(rows // CHUNK):
            for j in range(SUBLANES):
                dst_ref[c * CHUNK + j * TILES:c * CHUNK + (j + 1) * TILES, i * LANES:(i + 1) * LANES] = (
                    stage_ref[slot, pl.ds(c * CHUNK + j, TILES, stride=SUBLANES), :])


def _stage_scratch(rows):
    return pltpu.VMEM((2, rows, LANES), F32)


def _causal_mask():
    def pos(dim):
        r = lax.broadcasted_iota(jnp.int32, (CHUNK, CHUNK), dim)
        return (r % SUBLANES) * TILES + r // SUBLANES
    return pos(0) >= pos(1)


def _permute_rows_kernel(x_ref, o_ref):
    for c in range(x_ref.shape[0] // CHUNK):
        for k in range(TILES):
            o_ref[c * CHUNK + k * SUBLANES:c * CHUNK + (k + 1) * SUBLANES, :] = (
                x_ref[pl.ds(c * CHUNK + k, SUBLANES, stride=TILES), :])


def _permute_rows(x, rows):
    m, d = x.shape
    return pl.pallas_call(
        _permute_rows_kernel,
        grid=(m // rows, d // LANES),
        in_specs=[pl.BlockSpec((rows, LANES), lambda i, j: (i, j))],
        out_specs=pl.BlockSpec((rows, LANES), lambda i, j: (i, j)),
        out_shape=jax.ShapeDtypeStruct((m, d), x.dtype),
        compiler_params=_params("parallel", "parallel"),
        name="permute_rows",
    )(x)


def _prep_w_in_kernel(w_ref, wm_ref, wdt_ref):
    for dst, src, width in ((COL_XBC, SRC_XBC, SSD_XBC), (COL_LRU_X, SRC_LRU_X, LRU_WIDTH),
                            (COL_ML_X, SRC_ML_X, ML_WIDTH), (COL_Z, SRC_Z, SSD_WIDTH),
                            (COL_LRU_GATE, SRC_LRU_GATE, LRU_WIDTH), (COL_ML_O, SRC_ML_O, ML_WIDTH)):
        wm_ref[dst:dst + width, :] = w_ref[src:src + width, :].astype(BF16)
    wdt_ref[0:SSD_HEADS, :] = w_ref[SRC_DT:SRC_DT + SSD_HEADS, :].astype(BF16)
    wdt_ref[SSD_HEADS:, :] = jnp.zeros((LANES - SSD_HEADS, wdt_ref.shape[1]), BF16)


def _prep_w_in(w_in, cols):
    depth, d, _ = w_in.shape
    return pl.pallas_call(
        _prep_w_in_kernel,
        grid=(depth, d // cols),
        in_specs=[pl.BlockSpec((None, D_IN, cols), lambda l, i: (l, 0, i))],
        out_specs=[pl.BlockSpec((None, D_PROJ, cols), lambda l, i: (l, 0, i)),
                   pl.BlockSpec((None, LANES, cols), lambda l, i: (l, 0, i))],
        out_shape=[jax.ShapeDtypeStruct((depth, D_PROJ, d), BF16),
                   jax.ShapeDtypeStruct((depth, LANES, d), BF16)],
        compiler_params=_params("parallel", "parallel"),
        name="prep_w_in",
    )(jnp.swapaxes(w_in, 1, 2))


def _norm_matmul_kernel(x_ref, nw_ref, w_ref, wdt_ref, wout32_ref, o_ref, dt_ref, wout_ref, u_ref):
    @pl.when(pl.program_id(1) == 0)
    def _():
        u = _rmsnorm(x_ref[...], nw_ref[...]).astype(BF16)
        u_ref[...] = u
        dt_ref[...] = _dot_nt(u, wdt_ref[...])

    o_ref[...] = _dot_nt(u_ref[...], w_ref[...])
    wout_ref[...] = wout32_ref[...].astype(BF16)


def _norm_matmul(x, nw, w, wdt, w_out, layer, tm, tn):
    m, k = x.shape
    n = w.shape[1]
    ni, nj = m // tm, n // tn
    _, ko, no = w_out.shape
    rows = ko // (ni * nj)
    assert rows * ni * nj == ko and rows % (2 * SUBLANES) == 0, (ko, ni, nj)
    return pl.pallas_call(
        _norm_matmul_kernel,
        grid=(ni, nj),
        in_specs=[pl.BlockSpec((tm, k), lambda i, j: (i, 0)),
                  pl.BlockSpec((1, k), lambda i, j: (0, 0)),
                  pl.BlockSpec((None, tn, k), lambda i, j: (layer, j, 0)),
                  pl.BlockSpec((None, LANES, k), lambda i, j: (layer, 0, 0)),
                  pl.BlockSpec((None, rows, no), lambda i, j: (layer, i * nj + j, 0))],
        out_specs=[pl.BlockSpec((tm, tn), lambda i, j: (i, j)),
                   pl.BlockSpec((tm, LANES), lambda i, j: (i, 0)),
                   pl.BlockSpec((rows, no), lambda i, j: (i * nj + j, 0))],
        out_shape=[jax.ShapeDtypeStruct((m, n), F32), jax.ShapeDtypeStruct((m, LANES), F32),
                   jax.ShapeDtypeStruct((ko, no), BF16)],
        scratch_shapes=[pltpu.VMEM((tm, k), BF16)],
        compiler_params=_params("arbitrary", "arbitrary"),
        name="norm_in_proj",
    )(x, nw, w, wdt, w_out)


def _conv_columns(proj_ref, hist_ref, w_ref, b_ref, lo, width):
    cur = proj_ref[:, lo:lo + width]
    sub = lax.broadcasted_iota(jnp.int32, (SUBLANES, width), 0)
    wrapped = []
    for i in range(CONV_K - 1):
        r0 = CHUNK - HIST + i * SUBLANES
        from_cur = pltpu.roll(cur[r0:r0 + SUBLANES, :], 1, axis=0)
        from_prev = pltpu.roll(hist_ref[i * SUBLANES:(i + 1) * SUBLANES, lo:lo + width], 1, axis=0)
        wrapped.append(jnp.where(sub == 0, from_prev, from_cur))
    acc = cur * w_ref[CONV_K - 1:CONV_K, lo:lo + width] + b_ref[:, lo:lo + width]
    for s in range(1, CONV_K):
        shifted = jnp.concatenate(wrapped[CONV_K - 1 - s:] + [cur[0:CHUNK - s * SUBLANES, :]], axis=0)
        acc = acc + shifted * w_ref[CONV_K - 1 - s:CONV_K - s, lo:lo + width]
    return acc, cur


def _interleave(tasks):
    tasks = list(tasks)
    while tasks:
        for t in list(tasks):
            try:
                next(t)
            except StopIteration:
                tasks.remove(t)


def _ssd_prepare(dt_raw, causal01, dtb_ref, alog_ref):
    dt = _softplus(dt_raw + dtb_ref[...])
    da = dt * (-jnp.exp(alog_ref[...]))
    cs = _dot_exact_lhs01(causal01, da) * LOG2E
    return _split3(dt), _split3(cs), cs, cs.T


def _ssd_group(g, conv, proj_ref, prep, causal, dsk_ref, nw_ref, e_ref, state, out):
    dt_parts, cs_parts, cs, cs_t = prep
    gs = slice(g * SSD_GROUP_WIDTH, (g + 1) * SSD_GROUP_WIDTH)
    xs = _silu(conv(COL_XBC + gs.start, SSD_GROUP_WIDTH)[0])
    b_g = _silu(conv(COL_XBC + SSD_WIDTH + g * SSD_STATE, SSD_STATE)[0]).astype(BF16)
    c_g = _silu(conv(COL_XBC + SSD_WIDTH + SSD_BC + g * SSD_STATE, SSD_STATE)[0]).astype(BF16)
    scores = _dot_nt(c_g, b_g)
    expand = e_ref[:, gs]
    dt_e = (_dot(dt_parts[0], expand) + _dot(dt_parts[1], expand)) + _dot(dt_parts[2], expand)
    cs_e = (_dot(cs_parts[0], expand) + _dot(cs_parts[1], expand)) + _dot(cs_parts[2], expand)
    yield
    cs_last_e = cs_e[CHUNK - 1:CHUNK, :]
    xd = xs * dt_e
    head_of_lane = lax.broadcasted_iota(jnp.int32, (CHUNK, SSD_GROUP_WIDTH), 1) // SSD_HEAD_DIM
    lhs, rhs = [], []
    for j in range(SSD_HEADS_PER_GROUP):
        h = g * SSD_HEADS_PER_GROUP + j
        seg = cs[:, h:h + 1] - cs_t[h:h + 1, :]
        decay = jnp.exp2(jnp.where(causal, seg, -jnp.inf))
        lhs.append((scores * decay).astype(BF16))
        rhs.append(jnp.where(head_of_lane == j, xd, 0.0).astype(BF16))
    y_diag = _dot(jnp.concatenate(lhs, axis=1), jnp.concatenate(rhs, axis=0))
    y_off = _dot(c_g, state.astype(BF16))
    xd_to_end = (xd * jnp.exp2(cs_last_e - cs_e)).astype(BF16)
    new_state = state * jnp.exp2(cs_last_e) + _dot_tn(b_g, xd_to_end)
    yield
    y = y_diag + y_off * jnp.exp2(cs_e) + dsk_ref[:, gs] * xs
    y = y * _silu(proj_ref[:, COL_Z + gs.start:COL_Z + gs.stop])
    yn = y * lax.rsqrt(jnp.mean(y * y, axis=-1, keepdims=True) + EPS)
    out[g] = ((yn * nw_ref[:, gs]).astype(BF16), new_state)


def _mlstm_prepare(conv, causal01, wq_ref, wk_ref, wv_ref, wif_ref, bif_ref):
    qs, ks, vs = [], [], []
    for h in range(ML_HEADS):
        mc, mx = conv(COL_ML_X + h * ML_HEAD_DIM, ML_HEAD_DIM)
        mcb = _silu(mc).astype(BF16)
        qs.append(_dot(mcb, wq_ref[h]))
        ks.append(_dot(mcb, wk_ref[h]))
        vs.append(_dot(mx.astype(BF16), wv_ref[h]))
    gates = bif_ref[...]
    for part, vals in enumerate((qs, ks, vs)):
        for h in range(ML_HEADS):
            r0 = part * ML_WIDTH + h * ML_HEAD_DIM
            gates = gates + _dot(vals[h].astype(BF16), wif_ref[r0:r0 + ML_HEAD_DIM, :])
    i_pre = gates[:, :LANES] * LOG2E
    log_f = -_softplus(-gates[:, LANES:])
    bcum = _dot_exact_lhs01(causal01, log_f) * LOG2E
    g_tot = bcum[CHUNK - 1:CHUNK, :]
    return qs, ks, vs, bcum, bcum.T, i_pre.T, g_tot, g_tot - bcum + i_pre


def _mlstm_head(h, prep, proj_ref, causal, nw_ref, c_in, n_in, m_in, out):
    qs, ks, vs, bcum, bcum_t, i_t, g_tot, w_end = prep
    hs = slice(h * ML_HEAD_DIM, (h + 1) * ML_HEAD_DIM)
    q, v = qs[h], vs[h]
    qb = q.astype(BF16)
    kf = ks[h] * ML_KSCALE
    kb = kf.astype(BF16)
    bc = bcum[:, h:h + 1]
    dmat = jnp.where(causal, bc - bcum_t[h:h + 1, :] + i_t[h:h + 1, :], -jnp.inf)
    inter_log = bc + m_in
    m_t = jnp.maximum(jnp.max(dmat, axis=1, keepdims=True), inter_log)
    qk_raw = _dot_nt(qb, kb)
    inter = _dot(qb, c_in.astype(BF16))
    we = w_end[:, h:h + 1]
    m_loc = jnp.max(we, axis=0, keepdims=True)
    yield
    p_end = jnp.exp2(we - m_loc)
    c_loc = _dot_tn(kb, (v * p_end).astype(BF16))
    n_loc = jnp.sum(kf * p_end, axis=0, keepdims=True)
    s_inter = jnp.exp2(inter_log - m_t)
    qk = qk_raw * jnp.exp2(dmat - m_t)
    num = _dot(qk.astype(BF16), v.astype(BF16)) + s_inter * inter
    den = (jnp.sum(qk, axis=1, keepdims=True)
           + s_inter * jnp.sum(q * n_in, axis=1, keepdims=True))
    yield
    o = num / jnp.maximum(jnp.abs(den), jnp.exp2(-m_t))
    mu = jnp.mean(o, axis=1, keepdims=True)
    cen = o - mu
    var = jnp.mean(cen * cen, axis=1, keepdims=True)
    yield
    hn = cen * lax.rsqrt(var + EPS) * nw_ref[:, hs]
    y = (_sigmoid(proj_ref[:, COL_ML_O + hs.start:COL_ML_O + hs.stop]) * hn).astype(BF16)
    g_h = g_tot[:, h:h + 1]
    m_new = jnp.maximum(g_h + m_in, m_loc)
    s_old = jnp.exp2(g_h + m_in - m_new)
    s_loc = jnp.exp2(m_loc - m_new)
    out[h] = (y, s_old * c_in + s_loc * c_loc, s_old * n_in + s_loc * n_loc, m_new)


def _lru_scan(a, u, h_carry):
    def tile(x, k):
        return x[k * SUBLANES:(k + 1) * SUBLANES, :]

    decay, local = [tile(a, 0)], [tile(u, 0)]
    for k in range(1, TILES):
        a_k = tile(a, k)
        local.append(a_k * local[-1] + tile(u, k))
        decay.append(a_k * decay[-1])
    p, q = decay[-1], local[-1]
    sub = lax.broadcasted_iota(jnp.int32, p.shape, 0)
    d = 1
    while d < SUBLANES:
        p_prev = jnp.where(sub >= d, pltpu.roll(p, d, axis=0), 1.0)
        q_prev = jnp.where(sub >= d, pltpu.roll(q, d, axis=0), 0.0)
        q = p * q_prev + q
        p = p * p_prev
        d *= 2
    h_prev = jnp.broadcast_to(h_carry, p.shape)
    run_end = q + p * h_prev
    run_in = jnp.where(sub == 0, h_prev, pltpu.roll(run_end, 1, axis=0))
    hs = jnp.concatenate([local[k] + decay[k] * run_in for k in range(TILES)], axis=0)
    return hs, run_end[SUBLANES - 1:SUBLANES, :]


def _lru_tile(t, conv, proj_ref, wa_ref, ba_ref, wx_ref, bx_ref, lam_ref, h_carry, out):
    ts = slice(t * MXU_DIM, (t + 1) * MXU_DIM)
    xc = conv(COL_LRU_X + ts.start, MXU_DIM)[0]
    xcb = xc.astype(BF16)
    ra = _dot(xcb, wa_ref[t])
    ix = _dot(xcb, wx_ref[t])
    yield
    r = _sigmoid(ra + ba_ref[:, ts])
    i = _sigmoid(ix + bx_ref[:, ts])
    log_a = (-LRU_C * r) * _softplus(-lam_ref[:, ts])
    a = jnp.exp(log_a)
    u = jnp.sqrt(1.0 - jnp.exp(2.0 * log_a)) * (i * xc)
    yield
    hs, h_last = _lru_scan(a, u, h_carry)
    gate = proj_ref[:, COL_LRU_GATE + ts.start:COL_LRU_GATE + ts.stop]
    out[t] = ((hs * jax.nn.gelu(gate, approximate=True)).astype(BF16), h_last)


def _splat(x11, shape):
    return jnp.broadcast_to(x11, shape)


def _mixer_kernel(proj_ref, dt_ref, cw_ref, cb_ref,
                  dtb_ref, alog_ref, dsk_ref, snw_ref, e_ref,
                  wa_ref, ba_ref, wx_ref, bx_ref, lam_ref,
                  wq_ref, wk_ref, wv_ref, wif_ref, bif_ref, mnw_ref,
                  y_ref,
                  hist_ref, state_ref, h_ref, c_ref, n_ref, m_ref):
    @pl.when(pl.program_id(1) == 0)
    def _():
        for ref in (hist_ref, state_ref, h_ref, c_ref, n_ref, m_ref):
            ref[...] = jnp.zeros(ref.shape, F32)

    causal = _causal_mask()
    causal01 = causal.astype(BF16)
    conv = functools.partial(_conv_columns, proj_ref, hist_ref, cw_ref, cb_ref)
    n_lru = LRU_WIDTH // MXU_DIM

    ml_out, ssd_out, lru_out = {}, {}, {}
    ml_states = [(c_ref[h], n_ref[h][0:1, :], m_ref[h][0:1, 0:1]) for h in range(ML_HEADS)]
    ssd_prep = _ssd_prepare(dt_ref[...], causal01, dtb_ref, alog_ref)
    ssd = [_ssd_group(g, conv, proj_ref, ssd_prep, causal, dsk_ref, snw_ref, e_ref, state_ref[g], ssd_out)
           for g in range(SSD_GROUPS)]
    lru = [_lru_tile(t, conv, proj_ref, wa_ref, ba_ref, wx_ref, bx_ref, lam_ref,
                     h_ref[0:1, t * MXU_DIM:(t + 1) * MXU_DIM], lru_out) for t in range(n_lru)]
    ml_prep = _mlstm_prepare(conv, causal01, wq_ref, wk_ref, wv_ref, wif_ref, bif_ref)
    for t in ssd + lru:
        next(t)
    heads = [_mlstm_head(h, ml_prep, proj_ref, causal, mnw_ref, *ml_states[h], ml_out)
             for h in range(ML_HEADS)]
    per_head = SSD_GROUPS // ML_HEADS
    order = []
    for h in range(ML_HEADS):
        order += [heads[h]] + ssd[h * per_head:(h + 1) * per_head] + lru[h:h + 1]
    _interleave(order)

    hist_ref[...] = proj_ref[CHUNK - HIST:, :CONV_WIDTH]
    for g in range(SSD_GROUPS):
        y, new_state = ssd_out[g]
        y_ref[:, OUT_SSD + g * SSD_GROUP_WIDTH:OUT_SSD + (g + 1) * SSD_GROUP_WIDTH] = y
        state_ref[g] = new_state
    for t in range(n_lru):
        y, h_last = lru_out[t]
        y_ref[:, OUT_LRU + t * MXU_DIM:OUT_LRU + (t + 1) * MXU_DIM] = y
        h_ref[0:1, t * MXU_DIM:(t + 1) * MXU_DIM] = h_last
    for h in range(ML_HEADS):
        y, c_new, n_new, m_new = ml_out[h]
        y_ref[:, OUT_ML + h * ML_HEAD_DIM:OUT_ML + (h + 1) * ML_HEAD_DIM] = y
        c_ref[h] = c_new
        n_ref[h] = _splat(n_new, (SUBLANES, ML_HEAD_DIM))
        m_ref[h] = _splat(m_new, (SUBLANES, LANES))


def _mixers(proj, dt_raw, p):
    b, s, _ = proj.shape
    pad = LANES - SSD_HEADS
    head = jnp.arange(LANES)[:, None]
    expand = (head == (jnp.arange(SSD_WIDTH)[None, :] // SSD_HEAD_DIM)).astype(BF16)
    w_if, b_if = _pad_gate_weights(p["ml_w_if"], p["ml_b_if"])
    operands = [
        jnp.concatenate([p["ssd_conv_w"], p["lru_conv_w"], p["ml_conv_w"]], axis=1),
        jnp.concatenate([p["ssd_conv_b"], p["lru_conv_b"], p["ml_conv_b"]])[None, :],
        jnp.pad(p["ssd_dt_bias"], (0, pad))[None, :],
        jnp.pad(p["ssd_a_log"], (0, pad))[None, :],
        jnp.repeat(p["ssd_d"], SSD_HEAD_DIM)[None, :],
        p["ssd_norm_w"][None, :],
        expand,
        _block_diag_tiles(p["lru_w_a"], MXU_DIM), p["lru_b_a"][None, :],
        _block_diag_tiles(p["lru_w_x"], MXU_DIM), p["lru_b_x"][None, :],
        p["lru_lambda"][None, :],
        _block_diag_tiles(p["ml_w_q"], ML_HEAD_DIM),
        _block_diag_tiles(p["ml_w_k"], ML_HEAD_DIM),
        _block_diag_tiles(p["ml_w_v"], ML_HEAD_DIM),
        w_if.astype(BF16), b_if, p["ml_norm_w"][None, :],
    ]

    def const(a):
        return pl.BlockSpec(a.shape, lambda i, c, nd=a.ndim: (0,) * nd)

    return pl.pallas_call(
        _mixer_kernel,
        grid=(b, s // CHUNK),
        in_specs=[pl.BlockSpec((None, CHUNK, D_PROJ), lambda i, c: (i, c, 0)),
                  pl.BlockSpec((None, CHUNK, LANES), lambda i, c: (i, c, 0))]
                 + [const(a) for a in operands],
        out_specs=pl.BlockSpec((None, CHUNK, D_MIX), lambda i, c: (i, c, 0)),
        out_shape=jax.ShapeDtypeStruct((b, s, D_MIX), BF16),
        scratch_shapes=[pltpu.VMEM((HIST, CONV_WIDTH), F32),
                        pltpu.VMEM((SSD_GROUPS, SSD_STATE, SSD_GROUP_WIDTH), F32),
                        pltpu.VMEM((SUBLANES, LRU_WIDTH), F32),
                        pltpu.VMEM((ML_HEADS, ML_HEAD_DIM, ML_HEAD_DIM), F32),
                        pltpu.VMEM((ML_HEADS, SUBLANES, ML_HEAD_DIM), F32),
                        pltpu.VMEM((ML_HEADS, SUBLANES, LANES), F32)],
        compiler_params=_params("parallel", "arbitrary"),
        name="mixers",
    )(proj, dt_raw, *operands)


def _out_proj_kernel(h_ref, y_ref, w_ref, wgu32_ref, wd32_ref, o_ref, wgu_ref, wd_ref):
    o_ref[...] = h_ref[...] + _dot(y_ref[...], w_ref[...])

    wgu_ref[...] = wgu32_ref[...].astype(BF16)

    @pl.when(pl.program_id(0) == 0)
    def _():
        wd_ref[...] = wd32_ref[...].astype(BF16)


def _out_proj(h, y, w_out, w_gate_up, w_down, layer, tm, tn):
    m, n = h.shape
    k = y.shape[1]
    nj, ni = n // tn, m // tm
    _, d, f2 = w_gate_up.shape
    f = w_down.shape[1]
    bf16_rows = 2 * SUBLANES
    assert d % (bf16_rows * ni * nj) == 0 and f % (bf16_rows * ni) == 0, (d, f, ni, nj)

    def down_block(j, i):
        return jnp.where(j == 0, i, ni - 1)

    return pl.pallas_call(
        _out_proj_kernel,
        grid=(nj, ni),
        in_specs=[pl.BlockSpec((tm, tn), lambda j, i: (i, j)),
                  pl.BlockSpec((tm, k), lambda j, i: (i, 0)),
                  pl.BlockSpec((k, tn), lambda j, i: (0, j)),
                  pl.BlockSpec((None, d // (ni * nj), f2), lambda j, i: (layer, j * ni + i, 0)),
                  pl.BlockSpec((None, f // ni, d), lambda j, i: (layer, down_block(j, i), 0))],
        out_specs=[pl.BlockSpec((tm, tn), lambda j, i: (i, j)),
                   pl.BlockSpec((d // (ni * nj), f2), lambda j, i: (j * ni + i, 0)),
                   pl.BlockSpec((f // ni, d), lambda j, i: (down_block(j, i), 0))],
        out_shape=[jax.ShapeDtypeStruct((m, n), F32),
                   jax.ShapeDtypeStruct((d, f2), BF16),
                   jax.ShapeDtypeStruct((f, d), BF16)],
        compiler_params=_params("arbitrary", "arbitrary"),
        name="out_proj",
    )(h, y, w_out, w_gate_up, w_down)


def _ffn_kernel(h_ref, nw_ref, wg_ref, wu_ref, wd_ref, fw_ref, o_ref, u_ref, *stage, last_layer):
    j = pl.program_id(1)

    @pl.when(j == 0)
    def _():
        h = h_ref[...]
        u_ref[...] = _rmsnorm(h, nw_ref[...]).astype(BF16)
        o_ref[...] = h

    u = u_ref[...]
    gate = _dot(u, wg_ref[...])
    up = _dot(u, wu_ref[...])
    o_ref[...] += _dot((_silu(gate) * up).astype(BF16), wd_ref[...])

    if last_layer:
        @pl.when(j == pl.num_programs(1) - 1)
        def _():
            _store_chunk_unpermuted(_rmsnorm(o_ref[...], fw_ref[...]), *stage, o_ref)


def _ffn(h, nw, w_gate_up, w_down, fw, tm, tf, last_layer):
    m, d = h.shape
    f = w_down.shape[0]
    nf = f // tf
    return pl.pallas_call(
        functools.partial(_ffn_kernel, last_layer=last_layer),
        grid=(m // tm, nf),
        in_specs=[pl.BlockSpec((tm, d), lambda i, j: (i, 0)),
                  pl.BlockSpec((1, d), lambda i, j: (0, 0)),
                  pl.BlockSpec((d, tf), lambda i, j: (0, j)),
                  pl.BlockSpec((d, tf), lambda i, j: (0, j + nf)),
                  pl.BlockSpec((tf, d), lambda i, j: (j, 0)),
                  pl.BlockSpec((1, d), lambda i, j: (0, 0))],
        out_specs=pl.BlockSpec((tm, d), lambda i, j: (i, 0)),
        out_shape=jax.ShapeDtypeStruct((m, d), F32),
        scratch_shapes=[pltpu.VMEM((tm, d), BF16)] + ([_stage_scratch(tm)] if last_layer else []),
        compiler_params=_params("parallel", "arbitrary"),
        name="ffn_last" if last_layer else "ffn",
    )(h, nw, w_gate_up, w_gate_up, w_down, fw)


def _block_diag_tiles(w, tile):
    nblk, c, _ = w.shape
    col = jnp.arange(tile)
    spread = (jnp.arange(c)[:, None] == (col % c)[None, :]).astype(w.dtype)
    same_block = ((col // c)[:, None] == (col // c)[None, :]).astype(w.dtype)
    rep = jnp.dot(w.reshape(nblk * c, c), spread, precision=lax.Precision.HIGHEST)
    return (rep.reshape(nblk * c // tile, tile, tile) * same_block).astype(BF16)


def _pad_gate_weights(w_if, b_if):
    zw = jnp.zeros((w_if.shape[0], LANES - ML_HEADS), w_if.dtype)
    w = jnp.concatenate([w_if[:, :ML_HEADS], zw, w_if[:, ML_HEADS:], zw], axis=1)
    zb = jnp.zeros((LANES - ML_HEADS,), b_if.dtype)
    b = jnp.concatenate([b_if[:ML_HEADS], zb, b_if[ML_HEADS:], zb])[None, :]
    return w, b


IN_PROJ_ROWS = 1024
IN_PROJ_COLS = D_PROJ // 8
ROW_TILE = 512
FFN_ROWS = 1024
OUT_PROJ_COLS = 1024
FFN_COLS = 512
PREP_COLS = 256
PERMUTE_ROWS = 4096


def _tile(total, want):
    return want if total % want == 0 else total


def kernel(x, norm1_w, w_in, ssd_conv_w, ssd_conv_b, ssd_dt_bias, ssd_a_log, ssd_d, ssd_norm_w, lru_conv_w, lru_conv_b, lru_w_a, lru_b_a, lru_w_x, lru_b_x, lru_lambda, ml_conv_w, ml_conv_b, ml_w_q, ml_w_k, ml_w_v, ml_w_if, ml_b_if, ml_norm_w, w_out, norm2_w, w_gate_up, w_down, norm_f_w):
    b, s, d = x.shape
    m = b * s
    depth = w_in.shape[0]
    mixer_params = dict(
        ssd_conv_w=ssd_conv_w, ssd_conv_b=ssd_conv_b, ssd_dt_bias=ssd_dt_bias, ssd_a_log=ssd_a_log,
        ssd_d=ssd_d, ssd_norm_w=ssd_norm_w, lru_conv_w=lru_conv_w, lru_conv_b=lru_conv_b,
        lru_w_a=lru_w_a, lru_b_a=lru_b_a, lru_w_x=lru_w_x, lru_b_x=lru_b_x, lru_lambda=lru_lambda,
        ml_conv_w=ml_conv_w, ml_conv_b=ml_conv_b, ml_w_q=ml_w_q, ml_w_k=ml_w_k, ml_w_v=ml_w_v,
        ml_w_if=ml_w_if, ml_b_if=ml_b_if, ml_norm_w=ml_norm_w)
    w_main, w_dt = _prep_w_in(w_in, PREP_COLS)
    tm = _tile(m, ROW_TILE)
    h = _permute_rows(x.reshape(m, d), _tile(m, PERMUTE_ROWS))
    for l in range(depth):
        proj, dt_raw, w_out_b = _norm_matmul(h, norm1_w[l][None, :], w_main, w_dt, w_out, l,
                                             _tile(m, IN_PROJ_ROWS), IN_PROJ_COLS)
        y = _mixers(proj.reshape(b, s, D_PROJ), dt_raw.reshape(b, s, LANES),
                    {k: v[l] for k, v in mixer_params.items()})
        h, w_gate_up_b, w_down_b = _out_proj(h, y.reshape(m, D_MIX), w_out_b, w_gate_up, w_down, l, tm,
                                             OUT_PROJ_COLS)
        h = _ffn(h, norm2_w[l][None, :], w_gate_up_b, w_down_b, norm_f_w[None, :], _tile(m, FFN_ROWS),
                 FFN_COLS, last_layer=(l == depth - 1))
    return h.reshape(b, s, d)
```

```python
import functools

import jax
import jax.numpy as jnp
from jax import lax
from jax.experimental import pallas as pl
from jax.experimental.pallas import tpu as pltpu

F32 = jnp.float32
BF16 = jnp.bfloat16

EPS = 1e-6
LOG2E = 1.4426950408889634
CONV_K = 4
CHUNK = 128
SUBLANES = 8
LANES = 128
MXU_DIM = 256
TILES = CHUNK // SUBLANES
HIST = (CONV_K - 1) * SUBLANES

SSD_WIDTH = 2048
SSD_HEAD_DIM = 64
SSD_HEADS = 32
SSD_GROUPS = 8
SSD_STATE = 128
SSD_BC = SSD_GROUPS * SSD_STATE
SSD_XBC = SSD_WIDTH + 2 * SSD_BC
SSD_GROUP_WIDTH = SSD_WIDTH // SSD_GROUPS
SSD_HEADS_PER_GROUP = SSD_HEADS // SSD_GROUPS
LRU_WIDTH = 1024
LRU_C = 8.0
ML_WIDTH = 1024
ML_HEADS = 4
ML_HEAD_DIM = 256
ML_KSCALE = ML_HEAD_DIM ** -0.5
D_MIX = SSD_WIDTH + LRU_WIDTH + ML_WIDTH

COL_XBC = 0
COL_LRU_X = COL_XBC + SSD_XBC
COL_ML_X = COL_LRU_X + LRU_WIDTH
CONV_WIDTH = COL_ML_X + ML_WIDTH
COL_Z = CONV_WIDTH
COL_LRU_GATE = COL_Z + SSD_WIDTH
COL_ML_O = COL_LRU_GATE + LRU_WIDTH
D_PROJ = COL_ML_O + ML_WIDTH
SRC_Z = 0
SRC_XBC = SRC_Z + SSD_WIDTH
SRC_DT = SRC_XBC + SSD_XBC
SRC_LRU_GATE = SRC_DT + SSD_HEADS
SRC_LRU_X = SRC_LRU_GATE + LRU_WIDTH
SRC_ML_X = SRC_LRU_X + LRU_WIDTH
SRC_ML_O = SRC_ML_X + ML_WIDTH
D_IN = SRC_ML_O + ML_WIDTH
OUT_SSD = 0
OUT_LRU = OUT_SSD + SSD_WIDTH
OUT_ML = OUT_LRU + LRU_WIDTH

VMEM_LIMIT_BYTES = 60 * 1024 * 1024


def _params(*semantics):
    return pltpu.CompilerParams(dimension_semantics=semantics, vmem_limit_bytes=VMEM_LIMIT_BYTES)


def _dot(a, b):
    return jnp.dot(a, b, preferred_element_type=F32)


def _dot_nt(a, b):
    return lax.dot_general(a, b, (((1,), (1,)), ((), ())), preferred_element_type=F32)


def _dot_tn(a, b):
    return lax.dot_general(a, b, (((0,), (0,)), ((), ())), preferred_element_type=F32)


def _split3(x):
    hi = x.astype(BF16)
    r1 = x - hi.astype(F32)
    mid = r1.astype(BF16)
    lo = (r1 - mid.astype(F32)).astype(BF16)
    return hi, mid, lo


def _dot_exact_lhs01(m01, x):
    hi, mid, lo = _split3(x)
    return (_dot(m01, hi) + _dot(m01, mid)) + _dot(m01, lo)


def _sigmoid(x):
    return 0.5 * jnp.tanh(0.5 * x) + 0.5


def _silu(x):
    half = 0.5 * x
    return half * jnp.tanh(half) + half


def _softplus(x):
    return jnp.maximum(x, 0.0) + jnp.log1p(jnp.exp(-jnp.abs(x)))


def _rmsnorm(x, w):
    return x * lax.rsqrt(jnp.mean(x * x, axis=-1, keepdims=True) + EPS) * w


def _store_chunk_unpermuted(val, stage_ref, dst_ref):
    rows, width = val.shape
    for i in range(width // LANES):
        slot = i % stage_ref.shape[0]
        stage_ref[slot] = val[:, i * LANES:(i + 1) * LANES]
        for c in range(rows // CHUNK):
            for j in range(SUBLANES):
                dst_ref[c * CHUNK + j * TILES:c * CHUNK + (j + 1) * TILES, i * LANES:(i + 1) * LANES] = (
                    stage_ref[slot, pl.ds(c * CHUNK + j, TILES, stride=SUBLANES), :])


def _stage_scratch(rows):
    return pltpu.VMEM((2, rows, LANES), F32)


def _causal_mask():
    def pos(dim):
        r = lax.broadcasted_iota(jnp.int32, (CHUNK, CHUNK), dim)
        return (r % SUBLANES) * TILES + r // SUBLANES
    return pos(0) >= pos(1)


def _permute_rows_kernel(x_ref, o_ref):
    for c in range(x_ref.shape[0] // CHUNK):
        for k in range(TILES):
            o_ref[c * CHUNK + k * SUBLANES:c * CHUNK + (k + 1) * SUBLANES, :] = (
                x_ref[pl.ds(c * CHUNK + k, SUBLANES, stride=TILES), :])


def _permute_rows(x, rows):
    m, d = x.shape
    return pl.pallas_call(
        _permute_rows_kernel,
        grid=(m // rows, d // LANES),
        in_specs=[pl.BlockSpec((rows, LANES), lambda i, j: (i, j))],
        out_specs=pl.BlockSpec((rows, LANES), lambda i, j: (i, j)),
        out_shape=jax.ShapeDtypeStruct((m, d), x.dtype),
        compiler_params=_params("parallel", "parallel"),
        name="permute_rows",
    )(x)


def _prep_w_in_kernel(w_ref, wm_ref, wdt_ref):
    for dst, src, width in ((COL_XBC, SRC_XBC, SSD_XBC), (COL_LRU_X, SRC_LRU_X, LRU_WIDTH),
                            (COL_ML_X, SRC_ML_X, ML_WIDTH), (COL_Z, SRC_Z, SSD_WIDTH),
                            (COL_LRU_GATE, SRC_LRU_GATE, LRU_WIDTH), (COL_ML_O, SRC_ML_O, ML_WIDTH)):
        wm_ref[dst:dst + width, :] = w_ref[src:src + width, :].astype(BF16)
    wdt_ref[0:SSD_HEADS, :] = w_ref[SRC_DT:SRC_DT + SSD_HEADS, :].astype(BF16)
    wdt_ref[SSD_HEADS:, :] = jnp.zeros((LANES - SSD_HEADS, wdt_ref.shape[1]), BF16)


def _prep_w_in(w_in, cols):
    depth, d, _ = w_in.shape
    return pl.pallas_call(
        _prep_w_in_kernel,
        grid=(depth, d // cols),
        in_specs=[pl.BlockSpec((None, D_IN, cols), lambda l, i: (l, 0, i))],
        out_specs=[pl.BlockSpec((None, D_PROJ, cols), lambda l, i: (l, 0, i)),
                   pl.BlockSpec((None, LANES, cols), lambda l, i: (l, 0, i))],
        out_shape=[jax.ShapeDtypeStruct((depth, D_PROJ, d), BF16),
                   jax.ShapeDtypeStruct((depth, LANES, d), BF16)],
        compiler_params=_params("parallel", "parallel"),
        name="prep_w_in",
    )(jnp.swapaxes(w_in, 1, 2))


def _norm_matmul_kernel(x_ref, nw_ref, w_ref, wdt_ref, wout32_ref, o_ref, dt_ref, wout_ref, u_ref):
    @pl.when(pl.program_id(1) == 0)
    def _():
        u = _rmsnorm(x_ref[...], nw_ref[...]).astype(BF16)
        u_ref[...] = u
        dt_ref[...] = _dot_nt(u, wdt_ref[...])

    o_ref[...] = _dot_nt(u_ref[...], w_ref[...])
    wout_ref[...] = wout32_ref[...].astype(BF16)


def _norm_matmul(x, nw, w, wdt, w_out, layer, tm, tn):
    m, k = x.shape
    n = w.shape[1]
    ni, nj = m // tm, n // tn
    _, ko, no = w_out.shape
    rows = ko // (ni * nj)
    assert rows * ni * nj == ko and rows % (2 * SUBLANES) == 0, (ko, ni, nj)
    return pl.pallas_call(
        _norm_matmul_kernel,
        grid=(ni, nj),
        in_specs=[pl.BlockSpec((tm, k), lambda i, j: (i, 0)),
                  pl.BlockSpec((1, k), lambda i, j: (0, 0)),
                  pl.BlockSpec((None, tn, k), lambda i, j: (layer, j, 0)),
                  pl.BlockSpec((None, LANES, k), lambda i, j: (layer, 0, 0)),
                  pl.BlockSpec((None, rows, no), lambda i, j: (layer, i * nj + j, 0))],
        out_specs=[pl.BlockSpec((tm, tn), lambda i, j: (i, j)),
                   pl.BlockSpec((tm, LANES), lambda i, j: (i, 0)),
                   pl.BlockSpec((rows, no), lambda i, j: (i * nj + j, 0))],
        out_shape=[jax.ShapeDtypeStruct((m, n), F32), jax.ShapeDtypeStruct((m, LANES), F32),
                   jax.ShapeDtypeStruct((ko, no), BF16)],
        scratch_shapes=[pltpu.VMEM((tm, k), BF16)],
        compiler_params=_params("arbitrary", "arbitrary"),
        name="norm_in_proj",
    )(x, nw, w, wdt, w_out)


def _conv_columns(proj_ref, hist_ref, w_ref, b_ref, lo, width):
    cur = proj_ref[:, lo:lo + width]
    sub = lax.broadcasted_iota(jnp.int32, (SUBLANES, width), 0)
    wrapped = []
    for i in range(CONV_K - 1):
        r0 = CHUNK - HIST + i * SUBLANES
        from_cur = pltpu.roll(cur[r0:r0 + SUBLANES, :], 1, axis=0)
        from_prev = pltpu.roll(hist_ref[i * SUBLANES:(i + 1) * SUBLANES, lo:lo + width], 1, axis=0)
        wrapped.append(jnp.where(sub == 0, from_prev, from_cur))
    acc = cur * w_ref[CONV_K - 1:CONV_K, lo:lo + width] + b_ref[:, lo:lo + width]
    for s in range(1, CONV_K):
        shifted = jnp.concatenate(wrapped[CONV_K - 1 - s:] + [cur[0:CHUNK - s * SUBLANES, :]], axis=0)
        acc = acc + shifted * w_ref[CONV_K - 1 - s:CONV_K - s, lo:lo + width]
    return acc, cur


def _interleave(tasks):
    tasks = list(tasks)
    while tasks:
        for t in list(tasks):
            try:
                next(t)
            except StopIteration:
                tasks.remove(t)


def _ssd_prepare(dt_raw, causal01, dtb_ref, alog_ref):
    dt = _softplus(dt_raw + dtb_ref[...])
    da = dt * (-jnp.exp(alog_ref[...]))
    cs = _dot_exact_lhs01(causal01, da) * LOG2E
    return _split3(dt), _split3(cs), cs, cs.T


def _ssd_group(g, conv, proj_ref, prep, causal, dsk_ref, nw_ref, e_ref, state, out):
    dt_parts, cs_parts, cs, cs_t = prep
    gs = slice(g * SSD_GROUP_WIDTH, (g + 1) * SSD_GROUP_WIDTH)
    xs = _silu(conv(COL_XBC + gs.start, SSD_GROUP_WIDTH)[0])
    b_g = _silu(conv(COL_XBC + SSD_WIDTH + g * SSD_STATE, SSD_STATE)[0]).astype(BF16)
    c_g = _silu(conv(COL_XBC + SSD_WIDTH + SSD_BC + g * SSD_STATE, SSD_STATE)[0]).astype(BF16)
    scores = _dot_nt(c_g, b_g)
    expand = e_ref[:, gs]
    dt_e = (_dot(dt_parts[0], expand) + _dot(dt_parts[1], expand)) + _dot(dt_parts[2], expand)
    cs_e = (_dot(cs_parts[0], expand) + _dot(cs_parts[1], expand)) + _dot(cs_parts[2], expand)
    yield
    cs_last_e = cs_e[CHUNK - 1:CHUNK, :]
    xd = xs * dt_e
    head_of_lane = lax.broadcasted_iota(jnp.int32, (CHUNK, SSD_GROUP_WIDTH), 1) // SSD_HEAD_DIM
    lhs, rhs = [], []
    for j in range(SSD_HEADS_PER_GROUP):
        h = g * SSD_HEADS_PER_GROUP + j
        seg = cs[:, h:h + 1] - cs_t[h:h + 1, :]
        decay = jnp.exp2(jnp.where(causal, seg, -jnp.inf))
        lhs.append((scores * decay).astype(BF16))
        rhs.append(jnp.where(head_of_lane == j, xd, 0.0).astype(BF16))
    y_diag = _dot(jnp.concatenate(lhs, axis=1), jnp.concatenate(rhs, axis=0))
    y_off = _dot(c_g, state.astype(BF16))
    xd_to_end = (xd * jnp.exp2(cs_last_e - cs_e)).astype(BF16)
    new_state = state * jnp.exp2(cs_last_e) + _dot_tn(b_g, xd_to_end)
    yield
    y = y_diag + y_off * jnp.exp2(cs_e) + dsk_ref[:, gs] * xs
    y = y * _silu(proj_ref[:, COL_Z + gs.start:COL_Z + gs.stop])
    yn = y * lax.rsqrt(jnp.mean(y * y, axis=-1, keepdims=True) + EPS)
    out[g] = ((yn * nw_ref[:, gs]).astype(BF16), new_state)


def _mlstm_prepare(conv, causal01, wq_ref, wk_ref, wv_ref, wif_ref, bif_ref):
    qs, ks, vs = [], [], []
    for h in range(ML_HEADS):
        mc, mx = conv(COL_ML_X + h * ML_HEAD_DIM, ML_HEAD_DIM)
        mcb = _silu(mc).astype(BF16)
        qs.append(_dot(mcb, wq_ref[h]))
        ks.append(_dot(mcb, wk_ref[h]))
        vs.append(_dot(mx.astype(BF16), wv_ref[h]))
    gates = bif_ref[...]
    for part, vals in enumerate((qs, ks, vs)):
        for h in range(ML_HEADS):
            r0 = part * ML_WIDTH + h * ML_HEAD_DIM
            gates = gates + _dot(vals[h].astype(BF16), wif_ref[r0:r0 + ML_HEAD_DIM, :])
    i_pre = gates[:, :LANES] * LOG2E
    log_f = -_softplus(-gates[:, LANES:])
    bcum = _dot_exact_lhs01(causal01, log_f) * LOG2E
    g_tot = bcum[CHUNK - 1:CHUNK, :]
    return qs, ks, vs, bcum, bcum.T, i_pre.T, g_tot, g_tot - bcum + i_pre


def _mlstm_head(h, prep, proj_ref, causal, nw_ref, c_in, n_in, m_in, out):
    qs, ks, vs, bcum, bcum_t, i_t, g_tot, w_end = prep
    hs = slice(h * ML_HEAD_DIM, (h + 1) * ML_HEAD_DIM)
    q, v = qs[h], vs[h]
    qb = q.astype(BF16)
    kf = ks[h] * ML_KSCALE
    kb = kf.astype(BF16)
    bc = bcum[:, h:h + 1]
    dmat = jnp.where(causal, bc - bcum_t[h:h + 1, :] + i_t[h:h + 1, :], -jnp.inf)
    inter_log = bc + m_in
    m_t = jnp.maximum(jnp.max(dmat, axis=1, keepdims=True), inter_log)
    qk_raw = _dot_nt(qb, kb)
    inter = _dot(qb, c_in.astype(BF16))
    we = w_end[:, h:h + 1]
    m_loc = jnp.max(we, axis=0, keepdims=True)
    yield
    p_end = jnp.exp2(we - m_loc)
    c_loc = _dot_tn(kb, (v * p_end).astype(BF16))
    n_loc = jnp.sum(kf * p_end, axis=0, keepdims=True)
    s_inter = jnp.exp2(inter_log - m_t)
    qk = qk_raw * jnp.exp2(dmat - m_t)
    num = _dot(qk.astype(BF16), v.astype(BF16)) + s_inter * inter
    den = (jnp.sum(qk, axis=1, keepdims=True)
           + s_inter * jnp.sum(q * n_in, axis=1, keepdims=True))
    yield
    o = num / jnp.maximum(jnp.abs(den), jnp.exp2(-m_t))
    mu = jnp.mean(o, axis=1, keepdims=True)
    cen = o - mu
    var = jnp.mean(cen * cen, axis=1, keepdims=True)
    yield
    hn = cen * lax.rsqrt(var + EPS) * nw_ref[:, hs]
    y = (_sigmoid(proj_ref[:, COL_ML_O + hs.start:COL_ML_O + hs.stop]) * hn).astype(BF16)
    g_h = g_tot[:, h:h + 1]
    m_new = jnp.maximum(g_h + m_in, m_loc)
    s_old = jnp.exp2(g_h + m_in - m_new)
    s_loc = jnp.exp2(m_loc - m_new)
    out[h] = (y, s_old * c_in + s_loc * c_loc, s_old * n_in + s_loc * n_loc, m_new)


def _lru_scan(a, u, h_carry):
    def tile(x, k):
        return x[k * SUBLANES:(k + 1) * SUBLANES, :]

    decay, local = [tile(a, 0)], [tile(u, 0)]
    for k in range(1, TILES):
        a_k = tile(a, k)
        local.append(a_k * local[-1] + tile(u, k))
        decay.append(a_k * decay[-1])
    p, q = decay[-1], local[-1]
    sub = lax.broadcasted_iota(jnp.int32, p.shape, 0)
    d = 1
    while d < SUBLANES:
        p_prev = jnp.where(sub >= d, pltpu.roll(p, d, axis=0), 1.0)
        q_prev = jnp.where(sub >= d, pltpu.roll(q, d, axis=0), 0.0)
        q = p * q_prev + q
        p = p * p_prev
        d *= 2
    h_prev = jnp.broadcast_to(h_carry, p.shape)
    run_end = q + p * h_prev
    run_in = jnp.where(sub == 0, h_prev, pltpu.roll(run_end, 1, axis=0))
    hs = jnp.concatenate([local[k] + decay[k] * run_in for k in range(TILES)], axis=0)
    return hs, run_end[SUBLANES - 1:SUBLANES, :]


def _lru_tile(t, conv, proj_ref, wa_ref, ba_ref, wx_ref, bx_ref, lam_ref, h_carry, out):
    ts = slice(t * MXU_DIM, (t + 1) * MXU_DIM)
    xc = conv(COL_LRU_X + ts.start, MXU_DIM)[0]
    xcb = xc.astype(BF16)
    ra = _dot(xcb, wa_ref[t])
    ix = _dot(xcb, wx_ref[t])
    yield
    r = _sigmoid(ra + ba_ref[:, ts])
    i = _sigmoid(ix + bx_ref[:, ts])
    log_a = (-LRU_C * r) * _softplus(-lam_ref[:, ts])
    a = jnp.exp(log_a)
    u = jnp.sqrt(1.0 - jnp.exp(2.0 * log_a)) * (i * xc)
    yield
    hs, h_last = _lru_scan(a, u, h_carry)
    gate = proj_ref[:, COL_LRU_GATE + ts.start:COL_LRU_GATE + ts.stop]
    out[t] = ((hs * jax.nn.gelu(gate, approximate=True)).astype(BF16), h_last)


def _splat(x11, shape):
    return jnp.broadcast_to(x11, shape)


N_MIXER_PARAMS = 18


def _mixer_kernel(proj_ref, dt_ref, *rest):
    params, y_ref, scratch = rest[:N_MIXER_PARAMS], rest[N_MIXER_PARAMS], rest[N_MIXER_PARAMS + 1:]

    @pl.when(pl.program_id(1) == 0)
    def _():
        for ref in scratch:
            ref[...] = jnp.zeros(ref.shape, F32)

    for c in range(proj_ref.shape[0] // CHUNK):
        rows = pl.ds(c * CHUNK, CHUNK)
        _mixer_chunk(proj_ref.at[rows], dt_ref.at[rows], *params, y_ref.at[rows], *scratch)


def _mixer_chunk(proj_ref, dt_ref, cw_ref, cb_ref,
                 dtb_ref, alog_ref, dsk_ref, snw_ref, e_ref,
                 wa_ref, ba_ref, wx_ref, bx_ref, lam_ref,
                 wq_ref, wk_ref, wv_ref, wif_ref, bif_ref, mnw_ref,
                 y_ref,
                 hist_ref, state_ref, h_ref, c_ref, n_ref, m_ref):
    causal = _causal_mask()
    causal01 = causal.astype(BF16)
    conv = functools.partial(_conv_columns, proj_ref, hist_ref, cw_ref, cb_ref)
    n_lru = LRU_WIDTH // MXU_DIM

    ml_out, ssd_out, lru_out = {}, {}, {}
    ml_states = [(c_ref[h], n_ref[h][0:1, :], m_ref[h][0:1, 0:1]) for h in range(ML_HEADS)]
    ssd_prep = _ssd_prepare(dt_ref[...], causal01, dtb_ref, alog_ref)
    ssd = [_ssd_group(g, conv, proj_ref, ssd_prep, causal, dsk_ref, snw_ref, e_ref, state_ref[g], ssd_out)
           for g in range(SSD_GROUPS)]
    lru = [_lru_tile(t, conv, proj_ref, wa_ref, ba_ref, wx_ref, bx_ref, lam_ref,
                     h_ref[0:1, t * MXU_DIM:(t + 1) * MXU_DIM], lru_out) for t in range(n_lru)]
    ml_prep = _mlstm_prepare(conv, causal01, wq_ref, wk_ref, wv_ref, wif_ref, bif_ref)
    for t in ssd + lru:
        next(t)
    heads = [_mlstm_head(h, ml_prep, proj_ref, causal, mnw_ref, *ml_states[h], ml_out)
             for h in range(ML_HEADS)]
    per_head = SSD_GROUPS // ML_HEADS
    order = []
    for h in range(ML_HEADS):
        order += [heads[h]] + ssd[h * per_head:(h + 1) * per_head] + lru[h:h + 1]
    _interleave(order)

    hist_ref[...] = proj_ref[CHUNK - HIST:, :CONV_WIDTH]
    for g in range(SSD_GROUPS):
        y, new_state = ssd_out[g]
        y_ref[:, OUT_SSD + g * SSD_GROUP_WIDTH:OUT_SSD + (g + 1) * SSD_GROUP_WIDTH] = y
        state_ref[g] = new_state
    for t in range(n_lru):
        y, h_last = lru_out[t]
        y_ref[:, OUT_LRU + t * MXU_DIM:OUT_LRU + (t + 1) * MXU_DIM] = y
        h_ref[0:1, t * MXU_DIM:(t + 1) * MXU_DIM] = h_last
    for h in range(ML_HEADS):
        y, c_new, n_new, m_new = ml_out[h]
        y_ref[:, OUT_ML + h * ML_HEAD_DIM:OUT_ML + (h + 1) * ML_HEAD_DIM] = y
        c_ref[h] = c_new
        n_ref[h] = _splat(n_new, (SUBLANES, ML_HEAD_DIM))
        m_ref[h] = _splat(m_new, (SUBLANES, LANES))


def _mixers(proj, dt_raw, p):
    b, s, _ = proj.shape
    pad = LANES - SSD_HEADS
    head = jnp.arange(LANES)[:, None]
    expand = (head == (jnp.arange(SSD_WIDTH)[None, :] // SSD_HEAD_DIM)).astype(BF16)
    w_if, b_if = _pad_gate_weights(p["ml_w_if"], p["ml_b_if"])
    operands = [
        jnp.concatenate([p["ssd_conv_w"], p["lru_conv_w"], p["ml_conv_w"]], axis=1),
        jnp.concatenate([p["ssd_conv_b"], p["lru_conv_b"], p["ml_conv_b"]])[None, :],
        jnp.pad(p["ssd_dt_bias"], (0, pad))[None, :],
        jnp.pad(p["ssd_a_log"], (0, pad))[None, :],
        jnp.repeat(p["ssd_d"], SSD_HEAD_DIM)[None, :],
        p["ssd_norm_w"][None, :],
        expand,
        _block_diag_tiles(p["lru_w_a"], MXU_DIM), p["lru_b_a"][None, :],
        _block_diag_tiles(p["lru_w_x"], MXU_DIM), p["lru_b_x"][None, :],
        p["lru_lambda"][None, :],
        _block_diag_tiles(p["ml_w_q"], ML_HEAD_DIM),
        _block_diag_tiles(p["ml_w_k"], ML_HEAD_DIM),
        _block_diag_tiles(p["ml_w_v"], ML_HEAD_DIM),
        w_if.astype(BF16), b_if, p["ml_norm_w"][None, :],
    ]

    assert len(operands) == N_MIXER_PARAMS

    def const(a):
        return pl.BlockSpec(a.shape, lambda i, c, nd=a.ndim: (0,) * nd)

    rows = _tile(s, MIXER_ROWS)
    return pl.pallas_call(
        _mixer_kernel,
        grid=(b, s // rows),
        in_specs=[pl.BlockSpec((None, rows, D_PROJ), lambda i, c: (i, c, 0)),
                  pl.BlockSpec((None, rows, LANES), lambda i, c: (i, c, 0))]
                 + [const(a) for a in operands],
        out_specs=pl.BlockSpec((None, rows, D_MIX), lambda i, c: (i, c, 0)),
        out_shape=jax.ShapeDtypeStruct((b, s, D_MIX), BF16),
        scratch_shapes=[pltpu.VMEM((HIST, CONV_WIDTH), F32),
                        pltpu.VMEM((SSD_GROUPS, SSD_STATE, SSD_GROUP_WIDTH), F32),
                        pltpu.VMEM((SUBLANES, LRU_WIDTH), F32),
                        pltpu.VMEM((ML_HEADS, ML_HEAD_DIM, ML_HEAD_DIM), F32),
                        pltpu.VMEM((ML_HEADS, SUBLANES, ML_HEAD_DIM), F32),
                        pltpu.VMEM((ML_HEADS, SUBLANES, LANES), F32)],
        compiler_params=_params("parallel", "arbitrary"),
        name="mixers",
    )(proj, dt_raw, *operands)


def _out_proj_kernel(h_ref, y_ref, w_ref, wgu32_ref, wd32_ref, o_ref, wgu_ref, wd_ref):
    o_ref[...] = h_ref[...] + _dot(y_ref[...], w_ref[...])

    wgu_ref[...] = wgu32_ref[...].astype(BF16)

    @pl.when(pl.program_id(0) == 0)
    def _():
        wd_ref[...] = wd32_ref[...].astype(BF16)


def _out_proj(h, y, w_out, w_gate_up, w_down, layer, tm, tn):
    m, n = h.shape
    k = y.shape[1]
    nj, ni = n // tn, m // tm
    _, d, f2 = w_gate_up.shape
    f = w_down.shape[1]
    bf16_rows = 2 * SUBLANES
    assert d % (bf16_rows * ni * nj) == 0 and f % (bf16_rows * ni) == 0, (d, f, ni, nj)

    def down_block(j, i):
        return jnp.where(j == 0, i, ni - 1)

    return pl.pallas_call(
        _out_proj_kernel,
        grid=(nj, ni),
        in_specs=[pl.BlockSpec((tm, tn), lambda j, i: (i, j)),
                  pl.BlockSpec((tm, k), lambda j, i: (i, 0)),
                  pl.BlockSpec((k, tn), lambda j, i: (0, j)),
                  pl.BlockSpec((None, d // (ni * nj), f2), lambda j, i: (layer, j * ni + i, 0)),
                  pl.BlockSpec((None, f // ni, d), lambda j, i: (layer, down_block(j, i), 0))],
        out_specs=[pl.BlockSpec((tm, tn), lambda j, i: (i, j)),
                   pl.BlockSpec((d // (ni * nj), f2), lambda j, i: (j * ni + i, 0)),
                   pl.BlockSpec((f // ni, d), lambda j, i: (down_block(j, i), 0))],
        out_shape=[jax.ShapeDtypeStruct((m, n), F32),
                   jax.ShapeDtypeStruct((d, f2), BF16),
                   jax.ShapeDtypeStruct((f, d), BF16)],
        compiler_params=_params("arbitrary", "arbitrary"),
        name="out_proj",
    )(h, y, w_out, w_gate_up, w_down)


def _ffn_kernel(h_ref, nw_ref, wg_ref, wu_ref, wd_ref, fw_ref, o_ref, u_ref, *stage, last_layer):
    j = pl.program_id(1)

    @pl.when(j == 0)
    def _():
        h = h_ref[...]
        u_ref[...] = _rmsnorm(h, nw_ref[...]).astype(BF16)
        o_ref[...] = h

    u = u_ref[...]
    gate = _dot(u, wg_ref[...])
    up = _dot(u, wu_ref[...])
    o_ref[...] += _dot((_silu(gate) * up).astype(BF16), wd_ref[...])

    if last_layer:
        @pl.when(j == pl.num_programs(1) - 1)
        def _():
            _store_chunk_unpermuted(_rmsnorm(o_ref[...], fw_ref[...]), *stage, o_ref)


def _ffn(h, nw, w_gate_up, w_down, fw, tm, tf, last_layer):
    m, d = h.shape
    f = w_down.shape[0]
    nf = f // tf
    return pl.pallas_call(
        functools.partial(_ffn_kernel, last_layer=last_layer),
        grid=(m // tm, nf),
        in_specs=[pl.BlockSpec((tm, d), lambda i, j: (i, 0)),
                  pl.BlockSpec((1, d), lambda i, j: (0, 0)),
                  pl.BlockSpec((d, tf), lambda i, j: (0, j)),
                  pl.BlockSpec((d, tf), lambda i, j: (0, j + nf)),
                  pl.BlockSpec((tf, d), lambda i, j: (j, 0)),
                  pl.BlockSpec((1, d), lambda i, j: (0, 0))],
        out_specs=pl.BlockSpec((tm, d), lambda i, j: (i, 0)),
        out_shape=jax.ShapeDtypeStruct((m, d), F32),
        scratch_shapes=[pltpu.VMEM((tm, d), BF16)] + ([_stage_scratch(tm)] if last_layer else []),
        compiler_params=_params("parallel", "arbitrary"),
        name="ffn_last" if last_layer else "ffn",
    )(h, nw, w_gate_up, w_gate_up, w_down, fw)


def _block_diag_tiles(w, tile):
    nblk, c, _ = w.shape
    col = jnp.arange(tile)
    spread = (jnp.arange(c)[:, None] == (col % c)[None, :]).astype(w.dtype)
    same_block = ((col // c)[:, None] == (col // c)[None, :]).astype(w.dtype)
    rep = jnp.dot(w.reshape(nblk * c, c), spread, precision=lax.Precision.HIGHEST)
    return (rep.reshape(nblk * c // tile, tile, tile) * same_block).astype(BF16)


def _pad_gate_weights(w_if, b_if):
    zw = jnp.zeros((w_if.shape[0], LANES - ML_HEADS), w_if.dtype)
    w = jnp.concatenate([w_if[:, :ML_HEADS], zw, w_if[:, ML_HEADS:], zw], axis=1)
    zb = jnp.zeros((LANES - ML_HEADS,), b_if.dtype)
    b = jnp.concatenate([b_if[:ML_HEADS], zb, b_if[ML_HEADS:], zb])[None, :]
    return w, b


IN_PROJ_ROWS = 1024
IN_PROJ_COLS = D_PROJ // 8
ROW_TILE = 512
FFN_ROWS = 1024
MIXER_ROWS = 2 * CHUNK
OUT_PROJ_COLS = 1024
FFN_COLS = 512
PREP_COLS = 256
PERMUTE_ROWS = 4096


def _tile(total, want):
    return want if total % want == 0 else total


def kernel(x, norm1_w, w_in, ssd_conv_w, ssd_conv_b, ssd_dt_bias, ssd_a_log, ssd_d, ssd_norm_w, lru_conv_w, lru_conv_b, lru_w_a, lru_b_a, lru_w_x, lru_b_x, lru_lambda, ml_conv_w, ml_conv_b, ml_w_q, ml_w_k, ml_w_v, ml_w_if, ml_b_if, ml_norm_w, w_out, norm2_w, w_gate_up, w_down, norm_f_w):
    b, s, d = x.shape
    m = b * s
    depth = w_in.shape[0]
    mixer_params = dict(
        ssd_conv_w=ssd_conv_w, ssd_conv_b=ssd_conv_b, ssd_dt_bias=ssd_dt_bias, ssd_a_log=ssd_a_log,
        ssd_d=ssd_d, ssd_norm_w=ssd_norm_w, lru_conv_w=lru_conv_w, lru_conv_b=lru_conv_b,
        lru_w_a=lru_w_a, lru_b_a=lru_b_a, lru_w_x=lru_w_x, lru_b_x=lru_b_x, lru_lambda=lru_lambda,
        ml_conv_w=ml_conv_w, ml_conv_b=ml_conv_b, ml_w_q=ml_w_q, ml_w_k=ml_w_k, ml_w_v=ml_w_v,
        ml_w_if=ml_w_if, ml_b_if=ml_b_if, ml_norm_w=ml_norm_w)
    w_main, w_dt = _prep_w_in(w_in, PREP_COLS)
    tm = _tile(m, ROW_TILE)
    h = _permute_rows(x.reshape(m, d), _tile(m, PERMUTE_ROWS))
    for l in range(depth):
        proj, dt_raw, w_out_b = _norm_matmul(h, norm1_w[l][None, :], w_main, w_dt, w_out, l,
                                             _tile(m, IN_PROJ_ROWS), IN_PROJ_COLS)
        y = _mixers(proj.reshape(b, s, D_PROJ), dt_raw.reshape(b, s, LANES),
                    {k: v[l] for k, v in mixer_params.items()})
        h, w_gate_up_b, w_down_b = _out_proj(h, y.reshape(m, D_MIX), w_out_b, w_gate_up, w_down, l, tm,
                                             OUT_PROJ_COLS)
        h = _ffn(h, norm2_w[l][None, :], w_gate_up_b, w_down_b, norm_f_w[None, :], _tile(m, FFN_ROWS),
                 FFN_COLS, last_layer=(l == depth - 1))
    return h.reshape(b, s, d)
```

```python
import functools

import jax
import jax.numpy as jnp
from jax import lax
from jax.experimental import pallas as pl
from jax.experimental.pallas import tpu as pltpu

F32 = jnp.float32
BF16 = jnp.bfloat16

EPS = 1e-6
LOG2E = 1.4426950408889634
CONV_K = 4
CHUNK = 128
SUBLANES = 8
LANES = 128
MXU_DIM = 256
TILES = CHUNK // SUBLANES
HIST = (CONV_K - 1) * SUBLANES

SSD_WIDTH = 2048
SSD_HEAD_DIM = 64
SSD_HEADS = 32
SSD_GROUPS = 8
SSD_STATE = 128
SSD_BC = SSD_GROUPS * SSD_STATE
SSD_XBC = SSD_WIDTH + 2 * SSD_BC
SSD_GROUP_WIDTH = SSD_WIDTH // SSD_GROUPS
SSD_HEADS_PER_GROUP = SSD_HEADS // SSD_GROUPS
LRU_WIDTH = 1024
LRU_C = 8.0
ML_WIDTH = 1024
ML_HEADS = 4
ML_HEAD_DIM = 256
ML_KSCALE = ML_HEAD_DIM ** -0.5
D_MIX = SSD_WIDTH + LRU_WIDTH + ML_WIDTH

COL_XBC = 0
COL_LRU_X = COL_XBC + SSD_XBC
COL_ML_X = COL_LRU_X + LRU_WIDTH
CONV_WIDTH = COL_ML_X + ML_WIDTH
COL_Z = CONV_WIDTH
COL_LRU_GATE = COL_Z + SSD_WIDTH
COL_ML_O = COL_LRU_GATE + LRU_WIDTH
D_PROJ = COL_ML_O + ML_WIDTH
SRC_Z = 0
SRC_XBC = SRC_Z + SSD_WIDTH
SRC_DT = SRC_XBC + SSD_XBC
SRC_LRU_GATE = SRC_DT + SSD_HEADS
SRC_LRU_X = SRC_LRU_GATE + LRU_WIDTH
SRC_ML_X = SRC_LRU_X + LRU_WIDTH
SRC_ML_O = SRC_ML_X + ML_WIDTH
D_IN = SRC_ML_O + ML_WIDTH
OUT_SSD = 0
OUT_LRU = OUT_SSD + SSD_WIDTH
OUT_ML = OUT_LRU + LRU_WIDTH

VMEM_LIMIT_BYTES = 60 * 1024 * 1024


def _params(*semantics):
    return pltpu.CompilerParams(dimension_semantics=semantics, vmem_limit_bytes=VMEM_LIMIT_BYTES)


def _dot(a, b):
    return jnp.dot(a, b, preferred_element_type=F32)


def _dot_nt(a, b):
    return lax.dot_general(a, b, (((1,), (1,)), ((), ())), preferred_element_type=F32)


def _dot_tn(a, b):
    return lax.dot_general(a, b, (((0,), (0,)), ((), ())), preferred_element_type=F32)


def _split3(x):
    hi = x.astype(BF16)
    r1 = x - hi.astype(F32)
    mid = r1.astype(BF16)
    lo = (r1 - mid.astype(F32)).astype(BF16)
    return hi, mid, lo


def _dot_exact_lhs01(m01, x):
    hi, mid, lo = _split3(x)
    return (_dot(m01, hi) + _dot(m01, mid)) + _dot(m01, lo)


def _sigmoid(x):
    return 0.5 * jnp.tanh(0.5 * x) + 0.5


def _silu(x):
    half = 0.5 * x
    return half * jnp.tanh(half) + half


def _softplus(x):
    return jnp.maximum(x, 0.0) + jnp.log1p(jnp.exp(-jnp.abs(x)))


def _rmsnorm(x, w):
    return x * lax.rsqrt(jnp.mean(x * x, axis=-1, keepdims=True) + EPS) * w


def _store_chunk_unpermuted(val, stage_ref, dst_ref):
    rows, width = val.shape
    for i in range(width // LANES):
        slot = i % stage_ref.shape[0]
        stage_ref[slot] = val[:, i * LANES:(i + 1) * LANES]
        for c in range(rows // CHUNK):
            for j in range(SUBLANES):
                dst_ref[c * CHUNK + j * TILES:c * CHUNK + (j + 1) * TILES, i * LANES:(i + 1) * LANES] = (
                    stage_ref[slot, pl.ds(c * CHUNK + j, TILES, stride=SUBLANES), :])


def _stage_scratch(rows):
    return pltpu.VMEM((2, rows, LANES), F32)


def _causal_mask():
    def pos(dim):
        r = lax.broadcasted_iota(jnp.int32, (CHUNK, CHUNK), dim)
        return (r % SUBLANES) * TILES + r // SUBLANES
    return pos(0) >= pos(1)


def _permute_rows_kernel(x_ref, o_ref):
    for c in range(x_ref.shape[0] // CHUNK):
        for k in range(TILES):
            o_ref[c * CHUNK + k * SUBLANES:c * CHUNK + (k + 1) * SUBLANES, :] = (
                x_ref[pl.ds(c * CHUNK + k, SUBLANES, stride=TILES), :])


def _permute_rows(x, rows):
    m, d = x.shape
    return pl.pallas_call(
        _permute_rows_kernel,
        grid=(m // rows, d // LANES),
        in_specs=[pl.BlockSpec((rows, LANES), lambda i, j: (i, j))],
        out_specs=pl.BlockSpec((rows, LANES), lambda i, j: (i, j)),
        out_shape=jax.ShapeDtypeStruct((m, d), x.dtype),
        compiler_params=_params("parallel", "parallel"),
        name="permute_rows",
    )(x)


def _prep_w_in_kernel(w_ref, wm_ref, wdt_ref):
    for dst, src, width in ((COL_XBC, SRC_XBC, SSD_XBC), (COL_LRU_X, SRC_LRU_X, LRU_WIDTH),
                            (COL_ML_X, SRC_ML_X, ML_WIDTH), (COL_Z, SRC_Z, SSD_WIDTH),
                            (COL_LRU_GATE, SRC_LRU_GATE, LRU_WIDTH), (COL_ML_O, SRC_ML_O, ML_WIDTH)):
        wm_ref[dst:dst + width, :] = w_ref[src:src + width, :].astype(BF16)
    wdt_ref[0:SSD_HEADS, :] = w_ref[SRC_DT:SRC_DT + SSD_HEADS, :].astype(BF16)
    wdt_ref[SSD_HEADS:, :] = jnp.zeros((LANES - SSD_HEADS, wdt_ref.shape[1]), BF16)


def _prep_w_in(w_in, cols):
    depth, d, _ = w_in.shape
    return pl.pallas_call(
        _prep_w_in_kernel,
        grid=(depth, d // cols),
        in_specs=[pl.BlockSpec((None, D_IN, cols), lambda l, i: (l, 0, i))],
        out_specs=[pl.BlockSpec((None, D_PROJ, cols), lambda l, i: (l, 0, i)),
                   pl.BlockSpec((None, LANES, cols), lambda l, i: (l, 0, i))],
        out_shape=[jax.ShapeDtypeStruct((depth, D_PROJ, d), BF16),
                   jax.ShapeDtypeStruct((depth, LANES, d), BF16)],
        compiler_params=_params("parallel", "parallel"),
        name="prep_w_in",
    )(jnp.swapaxes(w_in, 1, 2))


def _norm_matmul_kernel(x_ref, nw_ref, w_ref, wdt_ref, wout32_ref, o_ref, dt_ref, wout_ref, u_ref):
    @pl.when(pl.program_id(1) == 0)
    def _():
        u = _rmsnorm(x_ref[...], nw_ref[...]).astype(BF16)
        u_ref[...] = u
        dt_ref[...] = _dot_nt(u, wdt_ref[...])

    o_ref[...] = _dot_nt(u_ref[...], w_ref[...])
    wout_ref[...] = wout32_ref[...].astype(BF16)


def _norm_matmul(x, nw, w, wdt, w_out, layer, tm, tn):
    m, k = x.shape
    n = w.shape[1]
    ni, nj = m // tm, n // tn
    _, ko, no = w_out.shape
    rows = ko // (ni * nj)
    assert rows * ni * nj == ko and rows % (2 * SUBLANES) == 0, (ko, ni, nj)
    return pl.pallas_call(
        _norm_matmul_kernel,
        grid=(ni, nj),
        in_specs=[pl.BlockSpec((tm, k), lambda i, j: (i, 0)),
                  pl.BlockSpec((1, k), lambda i, j: (0, 0)),
                  pl.BlockSpec((None, tn, k), lambda i, j: (layer, j, 0)),
                  pl.BlockSpec((None, LANES, k), lambda i, j: (layer, 0, 0)),
                  pl.BlockSpec((None, rows, no), lambda i, j: (layer, i * nj + j, 0))],
        out_specs=[pl.BlockSpec((tm, tn), lambda i, j: (i, j)),
                   pl.BlockSpec((tm, LANES), lambda i, j: (i, 0)),
                   pl.BlockSpec((rows, no), lambda i, j: (i * nj + j, 0))],
        out_shape=[jax.ShapeDtypeStruct((m, n), F32), jax.ShapeDtypeStruct((m, LANES), F32),
                   jax.ShapeDtypeStruct((ko, no), BF16)],
        scratch_shapes=[pltpu.VMEM((tm, k), BF16)],
        compiler_params=_params("arbitrary", "arbitrary"),
        name="norm_in_proj",
    )(x, nw, w, wdt, w_out)


def _conv_columns(proj_ref, hist_ref, w_ref, b_ref, lo, width):
    cur = proj_ref[:, lo:lo + width]
    sub = lax.broadcasted_iota(jnp.int32, (SUBLANES, width), 0)
    wrapped = []
    for i in range(CONV_K - 1):
        r0 = CHUNK - HIST + i * SUBLANES
        from_cur = pltpu.roll(cur[r0:r0 + SUBLANES, :], 1, axis=0)
        from_prev = pltpu.roll(hist_ref[i * SUBLANES:(i + 1) * SUBLANES, lo:lo + width], 1, axis=0)
        wrapped.append(jnp.where(sub == 0, from_prev, from_cur))
    acc = cur * w_ref[CONV_K - 1:CONV_K, lo:lo + width] + b_ref[:, lo:lo + width]
    for s in range(1, CONV_K):
        shifted = jnp.concatenate(wrapped[CONV_K - 1 - s:] + [cur[0:CHUNK - s * SUBLANES, :]], axis=0)
        acc = acc + shifted * w_ref[CONV_K - 1 - s:CONV_K - s, lo:lo + width]
    return acc, cur


def _interleave(tasks):
    tasks = list(tasks)
    while tasks:
        for t in list(tasks):
            try:
                next(t)
            except StopIteration:
                tasks.remove(t)


def _ssd_prepare(dt_raw, causal01, dtb_ref, alog_ref):
    dt = _softplus(dt_raw + dtb_ref[...])
    da = dt * (-jnp.exp(alog_ref[...]))
    cs = _dot_exact_lhs01(causal01, da) * LOG2E
    return _split3(dt), _split3(cs), cs, cs.T


def _ssd_group(g, conv, proj_ref, prep, causal, dsk_ref, nw_ref, e_ref, state, out):
    dt_parts, cs_parts, cs, cs_t = prep
    gs = slice(g * SSD_GROUP_WIDTH, (g + 1) * SSD_GROUP_WIDTH)
    halves = range(SSD_GROUP_WIDTH // LANES)
    heads_per_half = LANES // SSD_HEAD_DIM

    def half(x, i):
        return x[:, i * LANES:(i + 1) * LANES]

    xs_full = _silu(conv(COL_XBC + gs.start, SSD_GROUP_WIDTH)[0])
    xs = [half(xs_full, i) for i in halves]
    b_g = _silu(conv(COL_XBC + SSD_WIDTH + g * SSD_STATE, SSD_STATE)[0]).astype(BF16)
    c_g = _silu(conv(COL_XBC + SSD_WIDTH + SSD_BC + g * SSD_STATE, SSD_STATE)[0]).astype(BF16)
    scores = _dot_nt(c_g, b_g)
    expand = e_ref[:, gs]
    dt_e = (_dot(dt_parts[0], expand) + _dot(dt_parts[1], expand)) + _dot(dt_parts[2], expand)
    cs_e = (_dot(cs_parts[0], expand) + _dot(cs_parts[1], expand)) + _dot(cs_parts[2], expand)
    yield
    cs_last_e = cs_e[CHUNK - 1:CHUNK, :]
    head_of_lane = lax.broadcasted_iota(jnp.int32, (CHUNK, LANES), 1) // SSD_HEAD_DIM
    zeros = jnp.zeros((CHUNK, LANES), BF16)
    xd_full = xs_full * dt_e
    xd_to_end = [(xd_full * jnp.exp2(cs_last_e - cs_e)).astype(BF16)]
    rhs = [[] for _ in range(SSD_HEADS_PER_GROUP)]
    for i in halves:
        xd = half(xd_full, i)
        for j in range(SSD_HEADS_PER_GROUP):
            mine = j // heads_per_half == i
            rhs[j].append(jnp.where(head_of_lane == j % heads_per_half, xd, 0.0).astype(BF16) if mine else zeros)
    lhs = []
    for j in range(SSD_HEADS_PER_GROUP):
        h = g * SSD_HEADS_PER_GROUP + j
        seg = cs[:, h:h + 1] - cs_t[h:h + 1, :]
        decay = jnp.exp2(jnp.where(causal, seg, -jnp.inf))
        lhs.append((scores * decay).astype(BF16))
    y_diag = _dot(jnp.concatenate(lhs, axis=1),
                  jnp.concatenate([jnp.concatenate(r, axis=1) for r in rhs], axis=0))
    y_off = _dot(c_g, state.astype(BF16))
    new_state = state * jnp.exp2(cs_last_e) + _dot_tn(b_g, jnp.concatenate(xd_to_end, axis=1))
    yield
    y = y_diag + y_off * jnp.exp2(cs_e) + dsk_ref[:, gs] * xs_full
    y = y * _silu(proj_ref[:, COL_Z + gs.start:COL_Z + gs.stop])
    yn = y * lax.rsqrt(jnp.mean(y * y, axis=-1, keepdims=True) + EPS)
    out[g] = ((yn * nw_ref[:, gs]).astype(BF16), new_state)


def _mlstm_prepare(conv, causal01, wq_ref, wk_ref, wv_ref, wif_ref, bif_ref):
    qs, ks, vs = [], [], []
    for h in range(ML_HEADS):
        mc, mx = conv(COL_ML_X + h * ML_HEAD_DIM, ML_HEAD_DIM)
        mcb = _silu(mc).astype(BF16)
        qs.append(_dot(mcb, wq_ref[h]))
        ks.append(_dot(mcb, wk_ref[h]))
        vs.append(_dot(mx.astype(BF16), wv_ref[h]))
    gates = bif_ref[...]
    for part, vals in enumerate((qs, ks, vs)):
        for h in range(ML_HEADS):
            r0 = part * ML_WIDTH + h * ML_HEAD_DIM
            gates = gates + _dot(vals[h].astype(BF16), wif_ref[r0:r0 + ML_HEAD_DIM, :])
    i_pre = gates[:, :LANES] * LOG2E
    log_f = -_softplus(-gates[:, LANES:])
    bcum = _dot_exact_lhs01(causal01, log_f) * LOG2E
    g_tot = bcum[CHUNK - 1:CHUNK, :]
    return qs, ks, vs, bcum, bcum.T, i_pre.T, g_tot, g_tot - bcum + i_pre


def _mlstm_head(h, prep, proj_ref, causal, nw_ref, c_in, n_in, m_in, out):
    qs, ks, vs, bcum, bcum_t, i_t, g_tot, w_end = prep
    hs = slice(h * ML_HEAD_DIM, (h + 1) * ML_HEAD_DIM)
    q, v = qs[h], vs[h]
    qb = q.astype(BF16)
    kf = ks[h] * ML_KSCALE
    kb = kf.astype(BF16)
    bc = bcum[:, h:h + 1]
    dmat = jnp.where(causal, bc - bcum_t[h:h + 1, :] + i_t[h:h + 1, :], -jnp.inf)
    inter_log = bc + m_in
    m_t = jnp.maximum(jnp.max(dmat, axis=1, keepdims=True), inter_log)
    qk_raw = _dot_nt(qb, kb)
    inter = _dot(qb, c_in.astype(BF16))
    we = w_end[:, h:h + 1]
    m_loc = jnp.max(we, axis=0, keepdims=True)
    yield
    p_end = jnp.exp2(we - m_loc)
    c_loc = _dot_tn(kb, (v * p_end).astype(BF16))
    n_loc = jnp.sum(kf * p_end, axis=0, keepdims=True)
    s_inter = jnp.exp2(inter_log - m_t)
    qk = qk_raw * jnp.exp2(dmat - m_t)
    num = _dot(qk.astype(BF16), v.astype(BF16)) + s_inter * inter
    den = (jnp.sum(qk, axis=1, keepdims=True)
           + s_inter * jnp.sum(q * n_in, axis=1, keepdims=True))
    yield
    o = num / jnp.maximum(jnp.abs(den), jnp.exp2(-m_t))
    mu = jnp.mean(o, axis=1, keepdims=True)
    cen = o - mu
    var = jnp.mean(cen * cen, axis=1, keepdims=True)
    yield
    hn = cen * lax.rsqrt(var + EPS) * nw_ref[:, hs]
    y = (_sigmoid(proj_ref[:, COL_ML_O + hs.start:COL_ML_O + hs.stop]) * hn).astype(BF16)
    g_h = g_tot[:, h:h + 1]
    m_new = jnp.maximum(g_h + m_in, m_loc)
    s_old = jnp.exp2(g_h + m_in - m_new)
    s_loc = jnp.exp2(m_loc - m_new)
    out[h] = (y, s_old * c_in + s_loc * c_loc, s_old * n_in + s_loc * n_loc, m_new)


def _lru_scan(a, u, h_carry):
    def tile(x, k):
        return x[k * SUBLANES:(k + 1) * SUBLANES, :]

    decay, local = [tile(a, 0)], [tile(u, 0)]
    for k in range(1, TILES):
        a_k = tile(a, k)
        local.append(a_k * local[-1] + tile(u, k))
        decay.append(a_k * decay[-1])
    p, q = decay[-1], local[-1]
    sub = lax.broadcasted_iota(jnp.int32, p.shape, 0)
    d = 1
    while d < SUBLANES:
        p_prev = jnp.where(sub >= d, pltpu.roll(p, d, axis=0), 1.0)
        q_prev = jnp.where(sub >= d, pltpu.roll(q, d, axis=0), 0.0)
        q = p * q_prev + q
        p = p * p_prev
        d *= 2
    h_prev = jnp.broadcast_to(h_carry, p.shape)
    run_end = q + p * h_prev
    run_in = jnp.where(sub == 0, h_prev, pltpu.roll(run_end, 1, axis=0))
    hs = jnp.concatenate([local[k] + decay[k] * run_in for k in range(TILES)], axis=0)
    return hs, run_end[SUBLANES - 1:SUBLANES, :]


def _lru_tile(t, conv, proj_ref, wa_ref, ba_ref, wx_ref, bx_ref, lam_ref, h_carry, out):
    ts = slice(t * MXU_DIM, (t + 1) * MXU_DIM)
    xc = conv(COL_LRU_X + ts.start, MXU_DIM)[0]
    xcb = xc.astype(BF16)
    ra = _dot(xcb, wa_ref[t])
    ix = _dot(xcb, wx_ref[t])
    yield
    r = _sigmoid(ra + ba_ref[:, ts])
    i = _sigmoid(ix + bx_ref[:, ts])
    log_a = (-LRU_C * r) * _softplus(-lam_ref[:, ts])
    a = jnp.exp(log_a)
    u = jnp.sqrt(1.0 - jnp.exp(2.0 * log_a)) * (i * xc)
    yield
    hs, h_last = _lru_scan(a, u, h_carry)
    gate = proj_ref[:, COL_LRU_GATE + ts.start:COL_LRU_GATE + ts.stop]
    out[t] = ((hs * jax.nn.gelu(gate, approximate=True)).astype(BF16), h_last)


def _splat(x11, shape):
    return jnp.broadcast_to(x11, shape)


def _mixer_kernel(proj_ref, dt_ref, cw_ref, cb_ref,
                  dtb_ref, alog_ref, dsk_ref, snw_ref, e_ref,
                  wa_ref, ba_ref, wx_ref, bx_ref, lam_ref,
                  wq_ref, wk_ref, wv_ref, wif_ref, bif_ref, mnw_ref,
                  y_ref,
                  hist_ref, state_ref, h_ref, c_ref, n_ref, m_ref):
    @pl.when(pl.program_id(1) == 0)
    def _():
        for ref in (hist_ref, state_ref, h_ref, c_ref, n_ref, m_ref):
            ref[...] = jnp.zeros(ref.shape, F32)

    causal = _causal_mask()
    causal01 = causal.astype(BF16)
    conv = functools.partial(_conv_columns, proj_ref, hist_ref, cw_ref, cb_ref)
    n_lru = LRU_WIDTH // MXU_DIM

    ml_out, ssd_out, lru_out = {}, {}, {}
    ml_states = [(c_ref[h], n_ref[h][0:1, :], m_ref[h][0:1, 0:1]) for h in range(ML_HEADS)]
    ssd_prep = _ssd_prepare(dt_ref[...], causal01, dtb_ref, alog_ref)
    ssd = [_ssd_group(g, conv, proj_ref, ssd_prep, causal, dsk_ref, snw_ref, e_ref, state_ref[g], ssd_out)
           for g in range(SSD_GROUPS)]
    lru = [_lru_tile(t, conv, proj_ref, wa_ref, ba_ref, wx_ref, bx_ref, lam_ref,
                     h_ref[0:1, t * MXU_DIM:(t + 1) * MXU_DIM], lru_out) for t in range(n_lru)]
    ml_prep = _mlstm_prepare(conv, causal01, wq_ref, wk_ref, wv_ref, wif_ref, bif_ref)
    for t in ssd + lru:
        next(t)
    heads = [_mlstm_head(h, ml_prep, proj_ref, causal, mnw_ref, *ml_states[h], ml_out)
             for h in range(ML_HEADS)]
    per_head = SSD_GROUPS // ML_HEADS
    order = []
    for h in range(ML_HEADS):
        order += [heads[h]] + ssd[h * per_head:(h + 1) * per_head] + lru[h:h + 1]
    _interleave(order)

    hist_ref[...] = proj_ref[CHUNK - HIST:, :CONV_WIDTH]
    for g in range(SSD_GROUPS):
        y, new_state = ssd_out[g]
        y_ref[:, OUT_SSD + g * SSD_GROUP_WIDTH:OUT_SSD + (g + 1) * SSD_GROUP_WIDTH] = y
        state_ref[g] = new_state
    for t in range(n_lru):
        y, h_last = lru_out[t]
        y_ref[:, OUT_LRU + t * MXU_DIM:OUT_LRU + (t + 1) * MXU_DIM] = y
        h_ref[0:1, t * MXU_DIM:(t + 1) * MXU_DIM] = h_last
    for h in range(ML_HEADS):
        y, c_new, n_new, m_new = ml_out[h]
        y_ref[:, OUT_ML + h * ML_HEAD_DIM:OUT_ML + (h + 1) * ML_HEAD_DIM] = y
        c_ref[h] = c_new
        n_ref[h] = _splat(n_new, (SUBLANES, ML_HEAD_DIM))
        m_ref[h] = _splat(m_new, (SUBLANES, LANES))


def _mixers(proj, dt_raw, p):
    b, s, _ = proj.shape
    pad = LANES - SSD_HEADS
    head = jnp.arange(LANES)[:, None]
    expand = (head == (jnp.arange(SSD_WIDTH)[None, :] // SSD_HEAD_DIM)).astype(BF16)
    w_if, b_if = _pad_gate_weights(p["ml_w_if"], p["ml_b_if"])
    operands = [
        jnp.concatenate([p["ssd_conv_w"], p["lru_conv_w"], p["ml_conv_w"]], axis=1),
        jnp.concatenate([p["ssd_conv_b"], p["lru_conv_b"], p["ml_conv_b"]])[None, :],
        jnp.pad(p["ssd_dt_bias"], (0, pad))[None, :],
        jnp.pad(p["ssd_a_log"], (0, pad))[None, :],
        jnp.repeat(p["ssd_d"], SSD_HEAD_DIM)[None, :],
        p["ssd_norm_w"][None, :],
        expand,
        _block_diag_tiles(p["lru_w_a"], MXU_DIM), p["lru_b_a"][None, :],
        _block_diag_tiles(p["lru_w_x"], MXU_DIM), p["lru_b_x"][None, :],
        p["lru_lambda"][None, :],
        _block_diag_tiles(p["ml_w_q"], ML_HEAD_DIM),
        _block_diag_tiles(p["ml_w_k"], ML_HEAD_DIM),
        _block_diag_tiles(p["ml_w_v"], ML_HEAD_DIM),
        w_if.astype(BF16), b_if, p["ml_norm_w"][None, :],
    ]

    def const(a):
        return pl.BlockSpec(a.shape, lambda i, c, nd=a.ndim: (0,) * nd)

    return pl.pallas_call(
        _mixer_kernel,
        grid=(b, s // CHUNK),
        in_specs=[pl.BlockSpec((None, CHUNK, D_PROJ), lambda i, c: (i, c, 0)),
                  pl.BlockSpec((None, CHUNK, LANES), lambda i, c: (i, c, 0))]
                 + [const(a) for a in operands],
        out_specs=pl.BlockSpec((None, CHUNK, D_MIX), lambda i, c: (i, c, 0)),
        out_shape=jax.ShapeDtypeStruct((b, s, D_MIX), BF16),
        scratch_shapes=[pltpu.VMEM((HIST, CONV_WIDTH), F32),
                        pltpu.VMEM((SSD_GROUPS, SSD_STATE, SSD_GROUP_WIDTH), F32),
                        pltpu.VMEM((SUBLANES, LRU_WIDTH), F32),
                        pltpu.VMEM((ML_HEADS, ML_HEAD_DIM, ML_HEAD_DIM), F32),
                        pltpu.VMEM((ML_HEADS, SUBLANES, ML_HEAD_DIM), F32),
                        pltpu.VMEM((ML_HEADS, SUBLANES, LANES), F32)],
        compiler_params=_params("parallel", "arbitrary"),
        name="mixers",
    )(proj, dt_raw, *operands)


def _out_proj_kernel(h_ref, y_ref, w_ref, wgu32_ref, wd32_ref, o_ref, wgu_ref, wd_ref):
    o_ref[...] = h_ref[...] + _dot(y_ref[...], w_ref[...])

    wgu_ref[...] = wgu32_ref[...].astype(BF16)

    @pl.when(pl.program_id(0) == 0)
    def _():
        wd_ref[...] = wd32_ref[...].astype(BF16)


def _out_proj(h, y, w_out, w_gate_up, w_down, layer, tm, tn):
    m, n = h.shape
    k = y.shape[1]
    nj, ni = n // tn, m // tm
    _, d, f2 = w_gate_up.shape
    f = w_down.shape[1]
    bf16_rows = 2 * SUBLANES
    assert d % (bf16_rows * ni * nj) == 0 and f % (bf16_rows * ni) == 0, (d, f, ni, nj)

    def down_block(j, i):
        return jnp.where(j == 0, i, ni - 1)

    return pl.pallas_call(
        _out_proj_kernel,
        grid=(nj, ni),
        in_specs=[pl.BlockSpec((tm, tn), lambda j, i: (i, j)),
                  pl.BlockSpec((tm, k), lambda j, i: (i, 0)),
                  pl.BlockSpec((k, tn), lambda j, i: (0, j)),
                  pl.BlockSpec((None, d // (ni * nj), f2), lambda j, i: (layer, j * ni + i, 0)),
                  pl.BlockSpec((None, f // ni, d), lambda j, i: (layer, down_block(j, i), 0))],
        out_specs=[pl.BlockSpec((tm, tn), lambda j, i: (i, j)),
                   pl.BlockSpec((d // (ni * nj), f2), lambda j, i: (j * ni + i, 0)),
                   pl.BlockSpec((f // ni, d), lambda j, i: (down_block(j, i), 0))],
        out_shape=[jax.ShapeDtypeStruct((m, n), F32),
                   jax.ShapeDtypeStruct((d, f2), BF16),
                   jax.ShapeDtypeStruct((f, d), BF16)],
        compiler_params=_params("arbitrary", "arbitrary"),
        name="out_proj",
    )(h, y, w_out, w_gate_up, w_down)


def _ffn_kernel(h_ref, nw_ref, wg_ref, wu_ref, wd_ref, fw_ref, o_ref, u_ref, *stage, last_layer):
    j = pl.program_id(1)

    @pl.when(j == 0)
    def _():
        h = h_ref[...]
        u_ref[...] = _rmsnorm(h, nw_ref[...]).astype(BF16)
        o_ref[...] = h

    u = u_ref[...]
    gate = _dot(u, wg_ref[...])
    up = _dot(u, wu_ref[...])
    o_ref[...] += _dot((_silu(gate) * up).astype(BF16), wd_ref[...])

    if last_layer:
        @pl.when(j == pl.num_programs(1) - 1)
        def _():
            _store_chunk_unpermuted(_rmsnorm(o_ref[...], fw_ref[...]), *stage, o_ref)


def _ffn(h, nw, w_gate_up, w_down, fw, tm, tf, last_layer):
    m, d = h.shape
    f = w_down.shape[0]
    nf = f // tf
    return pl.pallas_call(
        functools.partial(_ffn_kernel, last_layer=last_layer),
        grid=(m // tm, nf),
        in_specs=[pl.BlockSpec((tm, d), lambda i, j: (i, 0)),
                  pl.BlockSpec((1, d), lambda i, j: (0, 0)),
                  pl.BlockSpec((d, tf), lambda i, j: (0, j)),
                  pl.BlockSpec((d, tf), lambda i, j: (0, j + nf)),
                  pl.BlockSpec((tf, d), lambda i, j: (j, 0)),
                  pl.BlockSpec((1, d), lambda i, j: (0, 0))],
        out_specs=pl.BlockSpec((tm, d), lambda i, j: (i, 0)),
        out_shape=jax.ShapeDtypeStruct((m, d), F32),
        scratch_shapes=[pltpu.VMEM((tm, d), BF16)] + ([_stage_scratch(tm)] if last_layer else []),
        compiler_params=_params("parallel", "arbitrary"),
        name="ffn_last" if last_layer else "ffn",
    )(h, nw, w_gate_up, w_gate_up, w_down, fw)


def _block_diag_tiles(w, tile):
    nblk, c, _ = w.shape
    col = jnp.arange(tile)
    spread = (jnp.arange(c)[:, None] == (col % c)[None, :]).astype(w.dtype)
    same_block = ((col // c)[:, None] == (col // c)[None, :]).astype(w.dtype)
    rep = jnp.dot(w.reshape(nblk * c, c), spread, precision=lax.Precision.HIGHEST)
    return (rep.reshape(nblk * c // tile, tile, tile) * same_block).astype(BF16)


def _pad_gate_weights(w_if, b_if):
    zw = jnp.zeros((w_if.shape[0], LANES - ML_HEADS), w_if.dtype)
    w = jnp.concatenate([w_if[:, :ML_HEADS], zw, w_if[:, ML_HEADS:], zw], axis=1)
    zb = jnp.zeros((LANES - ML_HEADS,), b_if.dtype)
    b = jnp.concatenate([b_if[:ML_HEADS], zb, b_if[ML_HEADS:], zb])[None, :]
    return w, b


IN_PROJ_ROWS = 1024
IN_PROJ_COLS = D_PROJ // 8
ROW_TILE = 512
FFN_ROWS = 1024
OUT_PROJ_COLS = 1024
FFN_COLS = 512
PREP_COLS = 256
PERMUTE_ROWS = 4096


def _tile(total, want):
    return want if total % want == 0 else total


def kernel(x, norm1_w, w_in, ssd_conv_w, ssd_conv_b, ssd_dt_bias, ssd_a_log, ssd_d, ssd_norm_w, lru_conv_w, lru_conv_b, lru_w_a, lru_b_a, lru_w_x, lru_b_x, lru_lambda, ml_conv_w, ml_conv_b, ml_w_q, ml_w_k, ml_w_v, ml_w_if, ml_b_if, ml_norm_w, w_out, norm2_w, w_gate_up, w_down, norm_f_w):
    b, s, d = x.shape
    m = b * s
    depth = w_in.shape[0]
    mixer_params = dict(
        ssd_conv_w=ssd_conv_w, ssd_conv_b=ssd_conv_b, ssd_dt_bias=ssd_dt_bias, ssd_a_log=ssd_a_log,
        ssd_d=ssd_d, ssd_norm_w=ssd_norm_w, lru_conv_w=lru_conv_w, lru_conv_b=lru_conv_b,
        lru_w_a=lru_w_a, lru_b_a=lru_b_a, lru_w_x=lru_w_x, lru_b_x=lru_b_x, lru_lambda=lru_lambda,
        ml_conv_w=ml_conv_w, ml_conv_b=ml_conv_b, ml_w_q=ml_w_q, ml_w_k=ml_w_k, ml_w_v=ml_w_v,
        ml_w_if=ml_w_if, ml_b_if=ml_b_if, ml_norm_w=ml_norm_w)
    w_main, w_dt = _prep_w_in(w_in, PREP_COLS)
    tm = _tile(m, ROW_TILE)
    h = _permute_rows(x.reshape(m, d), _tile(m, PERMUTE_ROWS))
    for l in range(depth):
        proj, dt_raw, w_out_b = _norm_matmul(h, norm1_w[l][None, :], w_main, w_dt, w_out, l,
                                             _tile(m, IN_PROJ_ROWS), IN_PROJ_COLS)
        y = _mixers(proj.reshape(b, s, D_PROJ), dt_raw.reshape(b, s, LANES),
                    {k: v[l] for k, v in mixer_params.items()})
        h, w_gate_up_b, w_down_b = _out_proj(h, y.reshape(m, D_MIX), w_out_b, w_gate_up, w_down, l, tm,
                                             OUT_PROJ_COLS)
        h = _ffn(h, norm2_w[l][None, :], w_gate_up_b, w_down_b, norm_f_w[None, :], _tile(m, FFN_ROWS),
                 FFN_COLS, last_layer=(l == depth - 1))
    return h.reshape(b, s, d)
```
